```python
import math
import jax, jax.numpy as jnp
from jax import lax
import numpy as np

D_MODEL = 1024
BATCH = 16
SEQ = 2048
DEPTH = 1

CTX_LEN = 256
GRID_W = 64
MIX_WIDTH = D_MODEL
HY_WIDTH = MIX_WIDTH // 2
HG_WIDTH = MIX_WIDTH - HY_WIDTH
HG_HEADS = 4
HG_HEAD_DIM = HG_WIDTH // HG_HEADS
HY_ORDER = 2
HY_SHORT = 3
HY_BANDS = 16
HY_POS_DIM = 2 * HY_BANDS + 1
HY_FILTER_HIDDEN = 64
HY_DECAY_TARGET = 1e-2
HY_FAST_DECAY = 0.3
HY_SLOW_DECAY = 1.5
HY_WINDOW_SHIFT = 0.05
CHUNK = 64
N_EXPERTS = 16
CAPACITY_FACTOR = 2
EXPERT_FF = 2 * D_MODEL
IN_COLS = 3 * HY_WIDTH + 5 * HG_WIDTH
EPS = 1e-6

kernel_name = "hyena_hgrn2_ec_moe_prefix_dit_block"


def rmsnorm(x, g):
    xf = x.astype(jnp.float32)
    y = xf * lax.rsqrt(jnp.mean(xf * xf, axis=-1, keepdims=True) + EPS)
    return (y * g.astype(jnp.float32)).astype(x.dtype)


def modulate(h, shift, scale):
    return h * (1 + scale) + shift


def short_conv(u, w, b, n_rows):
    bn, L, C = u.shape
    row_len = L // n_rows
    pad = HY_SHORT // 2
    up = jnp.pad(u.reshape(bn, n_rows, row_len, C), ((0, 0), (0, 0), (pad, pad), (0, 0)))
    y = b + sum(up[:, :, j:j + row_len] * w[j] for j in range(HY_SHORT))
    return y.reshape(bn, L, C)


def hyena_filter_freqs(L, w1, b1, fr1, w2, b2, fr2, w3):
    C = w3.shape[1] // (HY_ORDER * 2)
    t = jnp.linspace(0.0, 1.0, L, dtype=jnp.float32)[:, None]
    w = 2.0 * math.pi * jnp.arange(L, dtype=jnp.float32)[:, None] / L
    f = jnp.linspace(1e-4, HY_BANDS - 1, HY_BANDS, dtype=jnp.float32)[None, :]
    z = jnp.concatenate([t, jnp.cos(f * w), -jnp.sin(f * w)], axis=-1)
    h = jnp.sin(fr1.astype(jnp.float32) * (z @ w1.astype(jnp.float32) + b1.astype(jnp.float32)))
    h = jnp.sin(fr2.astype(jnp.float32) * (h @ w2.astype(jnp.float32) + b2.astype(jnp.float32)))
    h = (h @ w3.astype(jnp.float32)).reshape(L, HY_ORDER, 2, C)
    min_decay = math.log(HY_DECAY_TARGET) / HY_SLOW_DECAY
    max_decay = math.log(HY_DECAY_TARGET) / HY_FAST_DECAY
    deltas = jnp.abs(jnp.linspace(min_decay, max_decay, C, dtype=jnp.float32))
    window = jnp.exp(-t * deltas[None, :]) + HY_WINDOW_SHIFT
    h = h * window[:, None, None, :]
    full = jnp.concatenate([h[:, :, 0], jnp.zeros((1, HY_ORDER, C), jnp.float32), h[:0:-1, :, 1]], axis=0)
    full = full / (jnp.sum(jnp.abs(full), axis=0, keepdims=True) + EPS)
    return jnp.fft.rfft(full, axis=0)


def long_conv(u, hf, bias):
    L = u.shape[1]
    U = jnp.fft.rfft(u, n=2 * L, axis=1)
    y = jnp.fft.irfft(U * hf[None], n=2 * L, axis=1)[:, :L]
    return y + u * bias.astype(jnp.float32)


def hyena_mixer(p, n_rows, conv_w, conv_b, filt, bias):
    L = p.shape[1]
    u = short_conv(p, conv_w, conv_b, n_rows).astype(jnp.float32)
    v, x1, x2 = jnp.split(u, 3, axis=-1)
    hf = hyena_filter_freqs(L, *filt)
    z = x1 * long_conv(v, hf[:, 0], bias[0])
    y = x2 * long_conv(z, hf[:, 1], bias[1])
    return y.astype(p.dtype)


def chunk_scan(q, k, v, logf, s0):
    bn, L, H, K = q.shape
    n = L // CHUNK

    def blocks(a):
        return a.reshape(bn, n, CHUNK, H, a.shape[-1]).swapaxes(0, 1)

    earlier_or_same = jnp.tril(jnp.ones((CHUNK, CHUNK), dtype=bool))[None, :, :, None, None]

    def step(S, inp):
        qb, kb, vb, gb = inp
        G = jnp.cumsum(gb, axis=1)
        o_inter = jnp.einsum('bthk,bhkv->bthv', qb * jnp.exp(G), S)
        decay = jnp.exp(jnp.where(earlier_or_same, G[:, :, None] - G[:, None, :], -jnp.inf))
        A = jnp.einsum('bthk,bshk,btshk->bhts', qb, kb, decay)
        o_intra = jnp.einsum('bhts,bshv->bthv', A, vb)
        G_last = G[:, -1]
        S_new = jnp.exp(G_last)[..., None] * S + jnp.einsum('bshk,bshv->bhkv', kb * jnp.exp(G_last[:, None] - G), vb)
        return S_new, o_inter + o_intra

    S_fin, o = lax.scan(step, s0, (blocks(q), blocks(k), blocks(v), blocks(logf)))
    return o.swapaxes(0, 1).reshape(bn, L, H, v.shape[-1]), S_fin


def hgrn_project(p, lb):
    bn, L, _ = p.shape
    q, i, zf, zb, g = jnp.split(p, 5, axis=-1)

    def heads(a):
        return a.reshape(bn, L, HG_HEADS, HG_HEAD_DIM)

    q = heads(jax.nn.silu(q.astype(jnp.float32)) * HG_HEAD_DIM ** -0.5)
    v = heads(i.astype(jnp.float32))

    def gate(z, lb_d):
        f = lb_d + (1.0 - lb_d) * jax.nn.sigmoid(z.astype(jnp.float32))
        return heads(1.0 - f), heads(jnp.log(f))

    return q, v, gate(zf, lb[0]), gate(zb, lb[1]), g


def hgrn_readout(o, g, norm_g):
    bn, L = o.shape[:2]
    o = rmsnorm(o, norm_g).reshape(bn, L, HG_WIDTH)
    return (o * jax.nn.silu(g.astype(jnp.float32))).astype(g.dtype)


def hgrn_mixer(p_lat, p_ctx, lb, norm_g, need_ctx):
    ql, vl, (kfl, gfl), (kbl, gbl), g_lat = hgrn_project(p_lat, lb)
    qc, vc, (kfc, gfc), (kbc, gbc), g_ctx = hgrn_project(p_ctx, lb)
    s0 = jnp.zeros((p_lat.shape[0], HG_HEADS, HG_HEAD_DIM, HG_HEAD_DIM), jnp.float32)
    flip = lambda a: a[:, ::-1]
    oc_f, S_f = chunk_scan(qc, kfc, vc, gfc, s0)
    oc_b, S_b = chunk_scan(flip(qc), flip(kbc), flip(vc), flip(gbc), s0)
    ol_f, _ = chunk_scan(ql, kfl, vl, gfl, S_f)
    ol_b, _ = chunk_scan(flip(ql), flip(kbl), flip(vl), flip(gbl), S_b)
    out_lat = hgrn_readout(ol_f + flip(ol_b), g_lat, norm_g)
    out_ctx = hgrn_readout(oc_f + flip(oc_b), g_ctx, norm_g) if need_ctx else None
    return out_lat, out_ctx


def ec_moe(h, router_w, w1, w3, w2):
    bn, N, D = h.shape
    cap = CAPACITY_FACTOR * N // N_EXPERTS
    aff = jax.nn.softmax(jnp.einsum('bnd,de->bne', h, router_w).astype(jnp.float32), axis=-1)
    gates, idx = lax.top_k(jnp.swapaxes(aff, 1, 2), cap)
    xg = jax.vmap(lambda hb, ib: hb[ib])(h, idx)
    a = jnp.einsum('becd,edf->becf', xg, w1)
    b = jnp.einsum('becd,edf->becf', xg, w3)
    y = jnp.einsum('becf,efd->becd', jax.nn.silu(a) * b, w2)
    y = y * gates[..., None].astype(y.dtype)
    return jax.vmap(lambda yb, ib: jnp.zeros((N, D), yb.dtype).at[ib.reshape(-1)].add(yb.reshape(-1, D)))(y, idx)


def setup_inputs(seed: int = 0) -> dict:
    key = jax.random.key(seed)
    ks = jax.random.split(key, 28)

    def nrm(k, shape, s):
        return jax.random.normal(k, shape, jnp.float32) * s

    D, L_ = D_MODEL, DEPTH
    return {
        "x": nrm(ks[0], (BATCH, SEQ, D), 1.0),
        "c": nrm(ks[1], (BATCH, D), 1.0),
        "ctx": nrm(ks[2], (BATCH, CTX_LEN, D), 1.0),
        "c_ctx": nrm(ks[3], (D,), 1.0),
        "ada_w": nrm(ks[4], (L_, D, 6 * D), 0.5 * D ** -0.5),
        "ada_b": nrm(ks[5], (L_, 6 * D), 0.02),
        "norm1_g": 1.0 + nrm(ks[6], (L_, D), 0.05),
        "w_in": nrm(ks[7], (L_, D, IN_COLS), D ** -0.5),
        "hy_conv_w": nrm(ks[8], (L_, HY_SHORT, 3 * HY_WIDTH), HY_SHORT ** -0.5),
        "hy_conv_b": nrm(ks[9], (L_, 3 * HY_WIDTH), 0.02),
        "hy_pos_w1": nrm(ks[10], (L_, HY_POS_DIM, HY_FILTER_HIDDEN), HY_POS_DIM ** -0.5),
        "hy_pos_b1": nrm(ks[11], (L_, HY_FILTER_HIDDEN), 0.1),
        "hy_freq1": 1.0 + nrm(ks[12], (L_, HY_FILTER_HIDDEN), 0.1),
        "hy_pos_w2": nrm(ks[13], (L_, HY_FILTER_HIDDEN, HY_FILTER_HIDDEN), HY_FILTER_HIDDEN ** -0.5),
        "hy_pos_b2": nrm(ks[14], (L_, HY_FILTER_HIDDEN), 0.1),
        "hy_freq2": 1.0 + nrm(ks[15], (L_, HY_FILTER_HIDDEN), 0.1),
        "hy_pos_w3": nrm(ks[16], (L_, HY_FILTER_HIDDEN, HY_ORDER * 2 * HY_WIDTH), HY_FILTER_HIDDEN ** -0.5),
        "hy_bias": nrm(ks[17], (L_, HY_ORDER, HY_WIDTH), 0.5),
        "hy_norm_g": 1.0 + nrm(ks[18], (L_, HY_WIDTH), 0.05),
        "hg_lb": nrm(ks[19], (L_ + 1, 2, HG_WIDTH), 1.0),
        "hg_norm_g": 1.0 + nrm(ks[20], (L_, HG_HEAD_DIM), 0.05),
        "w_out": nrm(ks[21], (L_, MIX_WIDTH, D), MIX_WIDTH ** -0.5),
        "norm2_g": 1.0 + nrm(ks[22], (L_, D), 0.05),
        "router_w": nrm(ks[23], (L_, D, N_EXPERTS), D ** -0.5),
        "exp_w1": nrm(ks[24], (L_, N_EXPERTS, D, EXPERT_FF), D ** -0.5),
        "exp_w3": nrm(ks[25], (L_, N_EXPERTS, D, EXPERT_FF), D ** -0.5),
        "exp_w2": nrm(ks[26], (L_, N_EXPERTS, EXPERT_FF, D), EXPERT_FF ** -0.5),
        "final_g": 1.0 + nrm(ks[27], (D,), 0.05),
    }


def reference(x, c, ctx, c_ctx, ada_w, ada_b, norm1_g, w_in, hy_conv_w, hy_conv_b, hy_pos_w1, hy_pos_b1,
              hy_freq1, hy_pos_w2, hy_pos_b2, hy_freq2, hy_pos_w3, hy_bias, hy_norm_g, hg_lb, hg_norm_g,
              w_out, norm2_g, router_w, exp_w1, exp_w3, exp_w2, final_g):
    rows = x.shape[1] // GRID_W
    lb_all = jnp.cumsum(jax.nn.softmax(hg_lb.astype(jnp.float32), axis=0), axis=0)
    xc = ctx
    s_lat = jax.nn.silu(c)
    s_ctx = jax.nn.silu(c_ctx)
    for l in range(DEPTH):
        last = l == DEPTH - 1
        sh1, sc1, g1, sh2, sc2, g2 = jnp.split((s_lat @ ada_w[l] + ada_b[l])[:, None, :], 6, axis=-1)
        sh1c, sc1c, g1c, sh2c, sc2c, g2c = jnp.split(s_ctx @ ada_w[l] + ada_b[l], 6, axis=-1)

        h = modulate(rmsnorm(x, norm1_g[l]), sh1, sc1)
        hc = modulate(rmsnorm(xc, norm1_g[l]), sh1c, sc1c)
        p = h @ w_in[l]
        pc = hc @ w_in[l]
        filt = (hy_pos_w1[l], hy_pos_b1[l], hy_freq1[l], hy_pos_w2[l], hy_pos_b2[l], hy_freq2[l], hy_pos_w3[l])
        y_hy = hyena_mixer(p[..., :3 * HY_WIDTH], rows, hy_conv_w[l], hy_conv_b[l], filt, hy_bias[l])
        y_hg, y_hg_c = hgrn_mixer(p[..., 3 * HY_WIDTH:], pc[..., 3 * HY_WIDTH:], lb_all[l], hg_norm_g[l], not last)
        mix = jnp.concatenate([rmsnorm(y_hy, hy_norm_g[l]).astype(x.dtype), y_hg.astype(x.dtype)], axis=-1) @ w_out[l]
        x = x + g1 * mix
        if not last:
            y_hy_c = hyena_mixer(pc[..., :3 * HY_WIDTH], 1, hy_conv_w[l], hy_conv_b[l], filt, hy_bias[l])
            mix_c = jnp.concatenate([rmsnorm(y_hy_c, hy_norm_g[l]).astype(xc.dtype), y_hg_c.astype(xc.dtype)], axis=-1) @ w_out[l]
            xc = xc + g1c * mix_c

        h2 = modulate(rmsnorm(x, norm2_g[l]), sh2, sc2)
        x = x + g2 * ec_moe(h2, router_w[l], exp_w1[l], exp_w3[l], exp_w2[l])
        if not last:
            h2c = modulate(rmsnorm(xc, norm2_g[l]), sh2c, sc2c)
            xc = xc + g2c * ec_moe(h2c, router_w[l], exp_w1[l], exp_w3[l], exp_w2[l])
    return rmsnorm(x, final_g)
```

```python
import functools
import math

import numpy as np
import jax
import jax.numpy as jnp
from jax import lax
from jax.experimental import pallas as pl
from jax.experimental.pallas import tpu as pltpu

F32 = jnp.float32
BF16 = jnp.bfloat16
HIGHEST = lax.Precision.HIGHEST

GRID_W = 64
HG_HEADS = 4
HY_SHORT = 3
HY_BANDS = 16
HY_DECAY_TARGET = 1e-2
HY_FAST_DECAY = 0.3
HY_SLOW_DECAY = 1.5
HY_WINDOW_SHIFT = 0.05
N_EXPERTS = 16
CAPACITY_FACTOR = 2
EPS = 1e-6

SCAN_CHUNK = 64
SCAN_BLOCK = 16
FAST_BLOCK = 8
MAX_FAST_EXPONENT = 80.0
LANES = 128
VMEM_LIMIT = 56 * 1024 * 1024


def _cparams(sem):
    return pltpu.CompilerParams(dimension_semantics=sem, vmem_limit_bytes=VMEM_LIMIT)


def _resident(shape, grid_rank):
    zeros = (0,) * len(shape)
    return pl.BlockSpec(shape, {1: lambda a: zeros, 2: lambda a, b: zeros, 3: lambda a, b, c: zeros}[grid_rank],
                        pipeline_mode=pl.Buffered(1))


def _dot(a, b):
    return jnp.dot(a, b, preferred_element_type=F32)


def _dot_nt(a, b):
    return lax.dot_general(a, b, (((1,), (1,)), ((), ())), preferred_element_type=F32)


def _dot_tn(a, b):
    return lax.dot_general(a, b, (((0,), (0,)), ((), ())), preferred_element_type=F32)


def _silu(x):
    return x * jax.nn.sigmoid(x)


def _ada_kernel(s_ref, w_ref, b_ref, o_ref):
    s = _silu(s_ref[...])
    o_ref[...] = jnp.dot(s, w_ref[...], preferred_element_type=F32, precision=HIGHEST) + b_ref[...]


def _ada(s_rows, ada_w, ada_b):
    rows, d = s_rows.shape
    n = ada_w.shape[1]
    tn = 1024
    return pl.pallas_call(
        _ada_kernel,
        grid=(n // tn,),
        in_specs=[pl.BlockSpec((rows, d), lambda j: (0, 0)),
                  pl.BlockSpec((d, tn), lambda j: (0, j)),
                  pl.BlockSpec((1, tn), lambda j: (0, j))],
        out_specs=pl.BlockSpec((rows, tn), lambda j: (0, j)),
        out_shape=jax.ShapeDtypeStruct((rows, n), F32),
        compiler_params=_cparams(("arbitrary",)),
        name="ada",
    )(s_rows, ada_w, ada_b.reshape(1, n))


def _inproj_kernel(x_ref, sh_ref, sc_ref, g_ref, w_ref, o_ref, h_scr):
    @pl.when(pl.program_id(2) == 0)
    def _():
        x = x_ref[0]
        y = x * lax.rsqrt(jnp.mean(x * x, axis=-1, keepdims=True) + EPS) * g_ref[...]
        h_scr[...] = (y * (1.0 + sc_ref[0]) + sh_ref[0]).astype(BF16)

    o_ref[0] = _dot(h_scr[...], w_ref[...])


def _inproj(x, mods3, mod_row, norm_g, w_bf, col_blk0, n_col_blk, tm, tn):
    bsz, seq, d = x.shape
    return pl.pallas_call(
        _inproj_kernel,
        grid=(bsz, seq // tm, n_col_blk),
        in_specs=[pl.BlockSpec((1, tm, d), lambda b, i, j: (b, i, 0)),
                  pl.BlockSpec((1, 1, d), lambda b, i, j: (mod_row(b), 0, 0)),
                  pl.BlockSpec((1, 1, d), lambda b, i, j: (mod_row(b), 0, 1)),
                  pl.BlockSpec((1, d), lambda b, i, j: (0, 0)),
                  pl.BlockSpec((d, tn), lambda b, i, j: (0, col_blk0 + j))],
        out_specs=pl.BlockSpec((1, tm, tn), lambda b, i, j: (b, i, j)),
        out_shape=jax.ShapeDtypeStruct((bsz, seq, n_col_blk * tn), F32),
        scratch_shapes=[pltpu.VMEM((tm, d), BF16)],
        compiler_params=_cparams(("arbitrary", "arbitrary", "arbitrary")),
        name="inproj",
    )(x, mods3, mods3, norm_g.reshape(1, d), w_bf)


def _dft_tables(seq):
    half = seq // 2
    k = np.arange(half, dtype=np.int64)
    idx = (k[:, None] * k[None, :]) % seq
    ang = 2.0 * np.pi * np.arange(seq, dtype=np.float64) / seq
    cos_t = jnp.asarray(np.cos(ang)[idx].astype(np.float32))
    sin_t = jnp.asarray(np.sin(ang)[idx].astype(np.float32))
    return cos_t.astype(BF16), sin_t.astype(BF16)


def _pos_features(seq):
    t = np.linspace(0.0, 1.0, seq, dtype=np.float32)[:, None]
    w = (2.0 * math.pi * np.arange(seq, dtype=np.float32)[:, None] / seq).astype(np.float32)
    f = np.linspace(1e-4, HY_BANDS - 1, HY_BANDS, dtype=np.float32)[None, :]
    fw = (f * w).astype(np.float32)
    return np.concatenate([t, np.cos(fw), -np.sin(fw)], axis=-1).astype(np.float32)


def _filter_kernel(z_ref, w1_ref, b1_ref, f1_ref, w2_ref, b2_ref, f2_ref, w3f_ref, w3b_ref, t_ref, dl_ref,
                   sgn_ref, wk_ref, c2_ref, s2_ref, c_ref, s_ref, p_ref, hm_ref):
    half = c_ref.shape[0]
    n = 4 * half
    taps = {}
    for par in range(2):
        h = jnp.sin(f1_ref[...] * (jnp.dot(z_ref[par], w1_ref[...], preferred_element_type=F32, precision=HIGHEST)
                                   + b1_ref[...]))
        h = jnp.sin(f2_ref[...] * (jnp.dot(h, w2_ref[...], preferred_element_type=F32, precision=HIGHEST)
                                   + b2_ref[...]))
        window = jnp.exp(-t_ref[par] * dl_ref[...]) + HY_WINDOW_SHIFT
        taps["f", par] = jnp.dot(h, w3f_ref[...], preferred_element_type=F32, precision=HIGHEST) * window
        taps["b", par] = jnp.dot(h, w3b_ref[...], preferred_element_type=F32, precision=HIGHEST) * window
    row = lax.broadcasted_iota(jnp.int32, taps["b", 0].shape, 0)
    taps["b", 0] = jnp.where(row == 0, 0.0, taps["b", 0])
    inv = 1.0 / (sum(jnp.sum(jnp.abs(x), axis=0, keepdims=True) for x in taps.values()) + EPS)
    sgn = sgn_ref[...]
    spec = {}
    for key, x in taps.items():
        x = x * inv
        xb = x.astype(BF16)
        spec[key] = (_dot(c_ref[...], xb), _dot(s_ref[...], xb), jnp.sum(x * sgn, axis=0, keepdims=True))
    (fec, fes, fen), (foc, fos, fon) = spec["f", 0], spec["f", 1]
    (bec, bes, ben), (boc, bos, bon) = spec["b", 0], spec["b", 1]
    c2, s2 = c2_ref[...], s2_ref[...]
    wk = 2.0 * wk_ref[...]
    p_ref[0, 0] = (fec + bec) * wk
    p_ref[0, 1] = (bes - fes) * wk
    p_ref[0, 2] = (c2 * foc - s2 * fos + boc) * wk
    p_ref[0, 3] = (bos - c2 * fos - s2 * foc) * wk
    p_ref[0, 4] = (foc + c2 * boc - s2 * bos) * wk
    p_ref[0, 5] = (c2 * bos + s2 * boc - fos) * wk
    hm_ref[0] = jnp.concatenate([fen + ben, bon - fon], axis=0) * (2.0 / n)


def _hyena_filters(seq, pos_w1, pos_b1, freq1, pos_w2, pos_b2, freq2, pos_w3, dft_c, dft_s, ct):
    orders = 2
    half = seq // 2
    ch = pos_w3.shape[1] // (orders * 2)
    zt = _pos_features(seq)
    pdim = -(-zt.shape[1] // LANES) * LANES
    hid = -(-pos_w1.shape[1] // LANES) * LANES
    zt = np.pad(zt, ((0, 0), (0, pdim - zt.shape[1])))
    zt = np.stack([zt[0::2], zt[1::2]])
    pad2 = lambda a, r, c_: jnp.pad(a, ((0, r - a.shape[0]), (0, c_ - a.shape[1])))
    pos_w1 = pad2(pos_w1, pdim, hid)
    pos_w2 = pad2(pos_w2, hid, hid)
    pos_w3 = pad2(pos_w3, hid, pos_w3.shape[1])
    pos_b1, freq1, pos_b2, freq2 = (pad2(a.reshape(1, -1), 1, hid) for a in (pos_b1, freq1, pos_b2, freq2))
    t_col = np.linspace(0.0, 1.0, seq, dtype=np.float32)[:, None]
    t_col = np.stack([t_col[0::2], t_col[1::2]])
    min_decay = math.log(HY_DECAY_TARGET) / HY_SLOW_DECAY
    max_decay = math.log(HY_DECAY_TARGET) / HY_FAST_DECAY
    deltas = np.abs(np.linspace(min_decay, max_decay, ch, dtype=np.float32))[None, :]
    sgn = np.where(np.arange(half) % 2 == 0, 1.0, -1.0).astype(np.float32)[:, None]
    wk = np.full((half, 1), 2.0 / (2 * seq), np.float32)
    wk[0, 0] = 1.0 / (2 * seq)
    ang = 2.0 * np.pi * np.arange(half, dtype=np.float64)[:, None] / seq
    ncb = ch // ct
    const = lambda shape: pl.BlockSpec(shape, lambda o, c: (0,) * len(shape))
    return pl.pallas_call(
        _filter_kernel,
        grid=(orders, ncb),
        in_specs=[const((2, half, pdim)), const((pdim, hid)), const((1, hid)), const((1, hid)),
                  const((hid, hid)), const((1, hid)), const((1, hid)),
                  pl.BlockSpec((hid, ct), lambda o, c: (0, (2 * o) * ncb + c)),
                  pl.BlockSpec((hid, ct), lambda o, c: (0, (2 * o + 1) * ncb + c)),
                  const((2, half, 1)),
                  pl.BlockSpec((1, ct), lambda o, c: (0, c)),
                  const((half, 1)), const((half, 1)), const((half, 1)), const((half, 1)),
                  const((half, half)), const((half, half))],
        out_specs=[pl.BlockSpec((1, 6, half, ct), lambda o, c: (o, 0, 0, c)),
                   pl.BlockSpec((1, 2, ct), lambda o, c: (o, 0, c))],
        out_shape=[jax.ShapeDtypeStruct((orders, 6, half, ch), F32),
                   jax.ShapeDtypeStruct((orders, 2, ch), F32)],
        compiler_params=_cparams(("arbitrary", "arbitrary")),
        name="hyena_filter",
    )(jnp.asarray(zt), pos_w1, pos_b1, freq1, pos_w2, pos_b2, freq2, pos_w3, pos_w3, jnp.asarray(t_col),
      jnp.asarray(deltas), jnp.asarray(sgn), jnp.asarray(wk), jnp.asarray(np.cos(ang).astype(np.float32)),
      jnp.asarray(np.sin(ang).astype(np.float32)), dft_c, dft_s)


def _split_rows(x_ref):
    half = x_ref.shape[1] // 2
    return x_ref[0, pl.ds(0, half, stride=2), :], x_ref[0, pl.ds(1, half, stride=2), :]


def _short_conv_split(p_ref, w_ref, b_ref):
    pe, po = _split_rows(p_ref)
    half = pe.shape[0]
    pos = lax.broadcasted_iota(jnp.int32, pe.shape, 0) % (GRID_W // 2)
    before_even = jnp.where(pos == 0, 0.0, pltpu.roll(po, 1, axis=0))
    after_odd = jnp.where(pos == GRID_W // 2 - 1, 0.0, pltpu.roll(pe, half - 1, axis=0))
    w0, w1, w2 = w_ref[0:1, :], w_ref[1:2, :], w_ref[2:3, :]
    return (b_ref[...] + before_even * w0 + pe * w1 + po * w2,
            b_ref[...] + pe * w0 + po * w1 + after_odd * w2)


def _hyena_kernel(u_ref, g_ref, cwu_ref, cbu_ref, cwg_ref, cbg_ref, c_ref, s_ref, p_ref, hm_ref,
                  bias_ref, sgn_ref, o_ref, acc_ref, *, conv_u, freq_tile):
    half = c_ref.shape[0]
    ct = o_ref.shape[2]
    ue, uo = _short_conv_split(u_ref, cwu_ref, cbu_ref) if conv_u else _split_rows(u_ref)
    ub = jnp.concatenate([ue, uo], axis=1).astype(BF16)
    sgn = sgn_ref[...]
    en = jnp.sum(ue * sgn, axis=0, keepdims=True)
    on = jnp.sum(uo * sgn, axis=0, keepdims=True)
    hm_r, hm_i = hm_ref[0, 0:1, :], hm_ref[0, 1:2, :]
    bias = bias_ref[0]
    acc_ref[...] = jnp.concatenate([ue * bias + sgn * (en * hm_r + on * hm_i),
                                    uo * bias + sgn * (on * hm_r - en * hm_i)], axis=1)
    for ft in range(half // freq_tile):
        lo, hi = ft * freq_tile, (ft + 1) * freq_tile
        dc = _dot(c_ref[lo:hi, :], ub)
        ds = _dot(s_ref[lo:hi, :], ub)
        er, orr, es, os_ = dc[:, :ct], dc[:, ct:], ds[:, :ct], ds[:, ct:]
        p1r, p1i, p2r, p2i, p3r, p3i = (p_ref[0, i, lo:hi, :] for i in range(6))
        ar = er * p1r + es * p1i + orr * p2r + os_ * p2i
        br = er * p3r + es * p3i + orr * p1r + os_ * p1i
        ai = es * p1r - er * p1i + os_ * p2r - orr * p2i
        bi = es * p3r - er * p3i + os_ * p1r - orr * p1i
        acc_ref[...] += (_dot(c_ref[:, lo:hi], jnp.concatenate([ar, br], axis=1).astype(BF16))
                         + _dot(s_ref[:, lo:hi], jnp.concatenate([ai, bi], axis=1).astype(BF16)))
    ge, go = _short_conv_split(g_ref, cwg_ref, cbg_ref)
    o_ref[0, pl.ds(0, half, stride=2), :] = ge * acc_ref[:, :ct]
    o_ref[0, pl.ds(1, half, stride=2), :] = go * acc_ref[:, ct:]


def _hyena_conv(u, u_col_blk, p_hy, gate_col_blk, conv_w, conv_b, dft_c, dft_s, ptab, hm, bias, order,
                conv_u, ct):
    bsz, seq, _ = p_hy.shape
    half = seq // 2
    ch = ptab.shape[3]
    ncb = ch // ct
    sgn = jnp.asarray(np.where(np.arange(half) % 2 == 0, 1.0, -1.0).astype(np.float32)[:, None])
    taps = conv_w.shape[0]
    cbias = conv_b.reshape(1, -1)
    const = lambda shape: pl.BlockSpec(shape, lambda c, b: (0,) * len(shape))
    return pl.pallas_call(
        functools.partial(_hyena_kernel, conv_u=conv_u, freq_tile=half // 4),
        grid=(ncb, bsz),
        in_specs=[pl.BlockSpec((1, seq, ct), lambda c, b: (b, 0, u_col_blk * ncb + c)),
                  pl.BlockSpec((1, seq, ct), lambda c, b: (b, 0, gate_col_blk * ncb + c)),
                  pl.BlockSpec((taps, ct), lambda c, b: (0, u_col_blk * ncb + c)),
                  pl.BlockSpec((1, ct), lambda c, b: (0, u_col_blk * ncb + c)),
                  pl.BlockSpec((taps, ct), lambda c, b: (0, gate_col_blk * ncb + c)),
                  pl.BlockSpec((1, ct), lambda c, b: (0, gate_col_blk * ncb + c)),
                  const((half, half)), const((half, half)),
                  pl.BlockSpec((1, 6, half, ct), lambda c, b: (order, 0, 0, c)),
                  pl.BlockSpec((1, 2, ct), lambda c, b: (order, 0, c)),
                  pl.BlockSpec((1, 1, ct), lambda c, b: (order, 0, c)),
                  const((half, 1))],
        out_specs=pl.BlockSpec((1, seq, ct), lambda c, b: (b, 0, c)),
        out_shape=jax.ShapeDtypeStruct((bsz, seq, ch), F32),
        scratch_shapes=[pltpu.VMEM((half, 2 * ct), F32)],
        compiler_params=_cparams(("arbitrary", "arbitrary")),
        name=f"hyena_conv{order}",
    )(u, p_hy, conv_w, cbias, conv_w, cbias, dft_c, dft_s, ptab, hm, bias.reshape(bias.shape[0], 1, ch), sgn)


def _forget_gate(z, lb):
    f = lb + (1.0 - lb) * jax.nn.sigmoid(z)
    return 1.0 - f, jnp.log(f)


def _scan_consts():
    c, blk = SCAN_CHUNK, SCAN_BLOCK
    r = lax.broadcasted_iota(jnp.int32, (c, c), 0)
    s = lax.broadcasted_iota(jnp.int32, (c, c), 1)
    row = lax.broadcasted_iota(jnp.int32, (c, LANES), 0)
    fb = FAST_BLOCK

    def pair_mask(size, rev):
        same = (r // (2 * size)) == (s // (2 * size))
        hi_r, hi_s = r % (2 * size) >= size, s % (2 * size) >= size
        return same & ((~hi_r & hi_s) if rev else (hi_r & ~hi_s))

    return dict(
        cum_f=(s <= r).astype(F32), cum_b=(s >= r).astype(F32),
        m32_f=(r >= c // 2) & (s < c // 2), m32_b=(r < c // 2) & (s >= c // 2),
        m16_f=pair_mask(blk, False), m16_b=pair_mask(blk, True),
        m8_f=pair_mask(fb, False), m8_b=pair_mask(fb, True),
        md_f=(r // fb == s // fb) & (s <= r), md_b=(r // fb == s // fb) & (s >= r),
        sub=lax.broadcasted_iota(jnp.int32, (fb, LANES), 0),
        row=row, ones=jnp.ones((LANES, LANES), BF16))


def _state_update(st_ref, k, v, g, g_edge):
    kd = (k * jnp.exp(g_edge - g)).astype(BF16)
    st_ref[...] = st_ref[...] * jnp.exp(g_edge) + _dot_tn(v.astype(BF16), kd)


def _chunk_cumsum(logf, cst, reverse):
    return jnp.dot(cst["cum_b" if reverse else "cum_f"], logf, preferred_element_type=F32, precision=HIGHEST)


def _scan_chunk(st_ref, q, k, v, logf, cst, reverse):
    c, blk = SCAN_CHUNK, SCAN_BLOCK
    row = cst["row"]
    g = _chunk_cumsum(logf, cst, reverse)
    o = _dot_nt((q * jnp.exp(g)).astype(BF16), st_ref[...].astype(BF16))
    half = c // 2
    if not reverse:
        ref32 = g[half - 1:half, :]
        ref16 = jnp.where(row < half, g[blk - 1:blk, :], g[half + blk - 1:half + blk, :])
        t32, s32 = row >= half, row < half
        t16, s16 = row % (2 * blk) >= blk, row % (2 * blk) < blk
        g_edge = g[c - 1:c, :]
    else:
        ref32 = g[half:half + 1, :]
        ref16 = jnp.where(row < half, g[blk:blk + 1, :], g[half + blk:half + blk + 1, :])
        t32, s32 = row < half, row >= half
        t16, s16 = row % (2 * blk) < blk, row % (2 * blk) >= blk
        g_edge = g[0:1, :]
    a32 = jnp.where(t32, q * jnp.exp(jnp.minimum(g - ref32, 0.0)), 0.0).astype(BF16)
    b32 = jnp.where(s32, k * jnp.exp(jnp.minimum(ref32 - g, 0.0)), 0.0).astype(BF16)
    a16 = jnp.where(t16, q * jnp.exp(jnp.minimum(g - ref16, 0.0)), 0.0).astype(BF16)
    b16 = jnp.where(s16, k * jnp.exp(jnp.minimum(ref16 - g, 0.0)), 0.0).astype(BF16)
    att = (jnp.where(cst["m32_b" if reverse else "m32_f"], _dot_nt(a32, b32), 0.0)
           + jnp.where(cst["m16_b" if reverse else "m16_f"], _dot_nt(a16, b16), 0.0))
    o = o + _dot(att.astype(BF16), v.astype(BF16))
    pos = row % blk
    for d in range(blk):
        shift = d if not reverse else (c - d) % c
        if d == 0:
            ks, gs, vs = k, g, v
        else:
            ks = pltpu.roll(k, shift, axis=0)
            gs = pltpu.roll(g, shift, axis=0)
            vs = pltpu.roll(v, shift, axis=0)
        valid = (pos >= d) if not reverse else (pos + d < blk)
        w = jnp.where(valid, q * ks * jnp.exp(jnp.minimum(g - gs, 0.0)), 0.0)
        o = o + _dot(w.astype(BF16), cst["ones"]) * vs
    _state_update(st_ref, k, v, g, g_edge)
    return o


def _scan_chunk_factored(st_ref, q, k, v, logf, cst, reverse):
    c, fb = SCAN_CHUNK, FAST_BLOCK
    nb = c // fb
    d = "b" if reverse else "f"
    sub = cst["sub"]
    blocks = lambda x: [x[fb * j:fb * (j + 1), :] for j in range(nb)]
    qs, ks = blocks(q), blocks(k)
    ps = []
    for p in blocks(logf):
        for s in (1, 2, 4):
            if reverse:
                p = p + jnp.where(sub + s < fb, pltpu.roll(p, fb - s, axis=0), 0.0)
            else:
                p = p + jnp.where(sub >= s, pltpu.roll(p, s, axis=0), 0.0)
        ps.append(p)
    last = 0 if reverse else fb - 1
    tot = [jnp.broadcast_to(p[last:last + 1, :], p.shape) for p in ps]
    near, far = [None] * nb, [None] * nb
    run = jnp.zeros_like(tot[0])
    for j in (range(nb - 1, -1, -1) if reverse else range(nb)):
        near[j] = run
        run = run + tot[j]
        far[j] = run
    edge = run
    qe = [qs[j] * jnp.exp(ps[j]) for j in range(nb)]
    kb = [ks[j] * jnp.exp(tot[j] - ps[j]) for j in range(nb)]
    zero = jnp.zeros_like(run)
    cat = lambda parts: jnp.concatenate(parts, axis=0).astype(BF16)

    def level(n):
        a, b = [], []
        for j in range(nb):
            sb, later = j // (2 * n), (j % (2 * n)) >= n
            ref = far[sb * 2 * n + (n if reverse else n - 1)]
            if later != reverse:
                a.append(qe[j] * jnp.exp(near[j] - ref))
                b.append(zero)
            else:
                a.append(zero)
                b.append(kb[j] * jnp.exp(ref - far[j]))
        return _dot_nt(cat(a), cat(b))

    bd = cat([ks[j] * jnp.exp(-ps[j]) for j in range(nb)])
    att = (level(nb // 2)
           + jnp.where(cst["m16_" + d], level(nb // 4), 0.0)
           + jnp.where(cst["m8_" + d], level(nb // 8), 0.0)
           + jnp.where(cst["md_" + d], _dot_nt(cat(qe), bd), 0.0))
    qd = cat([qe[j] * jnp.exp(near[j]) for j in range(nb)])
    o = _dot_nt(qd, st_ref[...].astype(BF16)) + _dot(att.astype(BF16), v.astype(BF16))
    kd = cat([kb[j] * jnp.exp(edge - far[j]) for j in range(nb)])
    st_ref[...] = st_ref[...] * jnp.exp(edge[0:1, :]) + _dot_tn(v.astype(BF16), kd)
    return o


def _hgrn_kernel(q_ref, i_ref, zf_ref, zb_ref, g_ref, ic_ref, zfc_ref, zbc_ref, lb_ref, ng_ref, o_ref,
                 of_ref, ob_ref, st_ref):
    seq = q_ref.shape[1]
    ctx_len = ic_ref.shape[1]
    hd = ng_ref.shape[1]
    heads = q_ref.shape[2] // hd
    c = SCAN_CHUNK
    n_chunks, n_ctx = seq // c, ctx_len // c
    cst = _scan_consts()
    lbr = lb_ref[...]
    e = jnp.exp(lbr - jnp.max(lbr, axis=0, keepdims=True))
    lb = e[0] / jnp.sum(e, axis=0)
    qscale = hd ** -0.5
    lanes = [slice(h * hd, (h + 1) * hd) for h in range(heads)]
    st_f = [st_ref.at[2 * h] for h in range(heads)]
    st_b = [st_ref.at[2 * h + 1] for h in range(heads)]

    st_ref[...] = jnp.zeros_like(st_ref)

    def ctx_body(j, carry):
        rf = pl.multiple_of(j * c, c)
        rb = pl.multiple_of((n_ctx - 1 - j) * c, c)
        for h, ln in enumerate(lanes):
            kf, lf = _forget_gate(zfc_ref[0, pl.ds(rf, c), ln], lb[0:1, ln])
            gf = _chunk_cumsum(lf, cst, False)
            _state_update(st_f[h], kf, ic_ref[0, pl.ds(rf, c), ln], gf, gf[c - 1:c, :])
            kb, lbw = _forget_gate(zbc_ref[0, pl.ds(rb, c), ln], lb[1:2, ln])
            gb = _chunk_cumsum(lbw, cst, True)
            _state_update(st_b[h], kb, ic_ref[0, pl.ds(rb, c), ln], gb, gb[0:1, :])
        return carry

    lax.fori_loop(0, n_ctx, ctx_body, 0)

    def make_body(chunk_fn):
        def body(j, carry):
            rf = pl.multiple_of(j * c, c)
            rb = pl.multiple_of((n_chunks - 1 - j) * c, c)
            for h, ln in enumerate(lanes):
                kf, lf = _forget_gate(zf_ref[0, pl.ds(rf, c), ln], lb[0:1, ln])
                of_ref[pl.ds(rf, c), ln] = chunk_fn(st_f[h], _silu(q_ref[0, pl.ds(rf, c), ln]) * qscale, kf,
                                                    i_ref[0, pl.ds(rf, c), ln], lf, cst, False)
                kb, lbw = _forget_gate(zb_ref[0, pl.ds(rb, c), ln], lb[1:2, ln])
                ob_ref[pl.ds(rb, c), ln] = chunk_fn(st_b[h], _silu(q_ref[0, pl.ds(rb, c), ln]) * qscale, kb,
                                                    i_ref[0, pl.ds(rb, c), ln], lbw, cst, True)
            return carry
        return body

    bounded = jnp.min(lb) >= math.exp(-MAX_FAST_EXPONENT / FAST_BLOCK)

    @pl.when(bounded)
    def _():
        lax.fori_loop(0, n_chunks, make_body(_scan_chunk_factored), 0, unroll=2)

    @pl.when(jnp.logical_not(bounded))
    def _():
        lax.fori_loop(0, n_chunks, make_body(_scan_chunk), 0)

    for ln in lanes:
        o = of_ref[:, ln] + ob_ref[:, ln]
        y = o * lax.rsqrt(jnp.mean(o * o, axis=-1, keepdims=True) + EPS) * ng_ref[...]
        o_ref[0, :, ln] = y * _silu(g_ref[0, :, ln])


def _hgrn(p, col0, pc, hg_lb, norm_g, heads_per_step):
    bsz, seq, _ = p.shape
    ctx_len = pc.shape[1]
    hd = norm_g.shape[0]
    w = heads_per_step * hd
    steps = HG_HEADS // heads_per_step
    lat = lambda grp: pl.BlockSpec((1, seq, w), lambda b, h: (b, 0, (col0 + grp * HG_HEADS) // heads_per_step + h))
    ctx = lambda grp: pl.BlockSpec((1, ctx_len, w), lambda b, h: (b, 0, grp * steps + h))
    return pl.pallas_call(
        _hgrn_kernel,
        grid=(bsz, steps),
        in_specs=[lat(0), lat(1), lat(2), lat(3), lat(4), ctx(0), ctx(1), ctx(2),
                  pl.BlockSpec((hg_lb.shape[0], 2, w), lambda b, h: (0, 0, h)),
                  pl.BlockSpec((1, hd), lambda b, h: (0, 0))],
        out_specs=pl.BlockSpec((1, seq, w), lambda b, h: (b, 0, h)),
        out_shape=jax.ShapeDtypeStruct((bsz, seq, HG_HEADS * hd), F32),
        scratch_shapes=[pltpu.VMEM((seq, w), F32), pltpu.VMEM((seq, w), F32),
                        pltpu.VMEM((2 * heads_per_step, hd, hd), F32)],
        compiler_params=_cparams(("arbitrary", "arbitrary")),
        name="hgrn_scan",
    )(p, p, p, p, p, pc, pc, pc, hg_lb, norm_g.reshape(1, hd))


def _outproj_kernel(yhy_ref, yhg_ref, x_ref, g1_ref, sh2_ref, sc2_ref, hyg_ref, n2g_ref, wt_ref, wb_ref, rw_ref,
                    x1_ref, h2_ref, aff_ref):
    yhy = yhy_ref[0]
    yn = yhy * lax.rsqrt(jnp.mean(yhy * yhy, axis=-1, keepdims=True) + EPS) * hyg_ref[...]
    mix = _dot(yn.astype(BF16), wt_ref[...]) + _dot(yhg_ref[0].astype(BF16), wb_ref[...])
    x1 = x_ref[0] + g1_ref[0] * mix
    x1_ref[0] = x1
    h2 = (x1 * lax.rsqrt(jnp.mean(x1 * x1, axis=-1, keepdims=True) + EPS) * n2g_ref[...]
          * (1.0 + sc2_ref[0]) + sh2_ref[0])
    h2_ref[0] = h2.astype(BF16)
    logits = lax.dot_general(rw_ref[...], h2, (((1,), (1,)), ((), ())), preferred_element_type=F32,
                             precision=HIGHEST)
    ex = jnp.exp(logits - jnp.max(logits, axis=0, keepdims=True))
    aff_ref[0] = ex / jnp.sum(ex, axis=0, keepdims=True)


def _outproj(y_hy, y_hg, x, mods3, hy_norm_g, norm2_g, w_top, w_bot, router_wt, tm):
    bsz, seq, d = x.shape
    wh = y_hy.shape[2]
    ne = router_wt.shape[0]
    mod = lambda k: pl.BlockSpec((1, 1, d), lambda b, i: (b, 0, k))
    const = lambda shape: pl.BlockSpec(shape, lambda b, i: (0,) * len(shape))
    return pl.pallas_call(
        _outproj_kernel,
        grid=(bsz, seq // tm),
        in_specs=[pl.BlockSpec((1, tm, wh), lambda b, i: (b, i, 0)),
                  pl.BlockSpec((1, tm, wh), lambda b, i: (b, i, 0)),
                  pl.BlockSpec((1, tm, d), lambda b, i: (b, i, 0)),
                  mod(2), mod(3), mod(4),
                  const((1, wh)), const((1, d)), const((wh, d)), const((wh, d)), const((ne, d))],
        out_specs=[pl.BlockSpec((1, tm, d), lambda b, i: (b, i, 0)),
                   pl.BlockSpec((1, tm, d), lambda b, i: (b, i, 0)),
                   pl.BlockSpec((1, ne, tm), lambda b, i: (b, 0, i))],
        out_shape=[jax.ShapeDtypeStruct((bsz, seq, d), F32),
                   jax.ShapeDtypeStruct((bsz, seq, d), BF16),
                   jax.ShapeDtypeStruct((bsz, ne, seq), F32)],
        compiler_params=_cparams(("arbitrary", "arbitrary")),
        name="outproj_router",
    )(y_hy, y_hg, x, mods3, mods3, mods3, hy_norm_g.reshape(1, wh), norm2_g.reshape(1, d), w_top, w_bot, router_wt)


def _select_kernel(a_ref, slot_ref, *, cap, rows_per_expert):
    nr = a_ref.shape[1]
    affs = [a_ref[i] for i in range(a_ref.shape[0])]
    r = lax.broadcasted_iota(jnp.int32, (nr, nr), 0)
    s = lax.broadcasted_iota(jnp.int32, (nr, nr), 1)
    same = (r // rows_per_expert) == (s // rows_per_expert)
    grp = same.astype(BF16)
    before = (same & (s < r)).astype(BF16)
    li = lax.broadcasted_iota(jnp.int32, (LANES, LANES), 0)
    lj = lax.broadcasted_iota(jnp.int32, (LANES, LANES), 1)
    ones = jnp.ones((LANES, LANES), BF16)
    strict = (li < lj).astype(BF16)

    def row_total(ind):
        return _dot(ind, ones).astype(BF16)

    def expert_total(ind):
        return _dot(grp, row_total(ind))

    def prefix(ind):
        return _dot(ind, strict) + _dot(before, row_total(ind))

    def search(i, thrs):
        bit = jnp.left_shift(jnp.int32(1), 30 - i)
        out = []
        for a, t in zip(affs, thrs):
            cand = t | bit
            cnt = expert_total(jnp.where(a >= pltpu.bitcast(cand, F32), 1.0, 0.0).astype(BF16))
            out.append(jnp.where(cnt >= cap, cand, t))
        return tuple(out)

    thrs = lax.fori_loop(0, 31, search, tuple(jnp.zeros((nr, LANES), jnp.int32) for _ in affs))
    for i, (a, thr) in enumerate(zip(affs, thrs)):
        thr_f = pltpu.bitcast(thr, F32)
        gt = a > thr_f
        eq = a == thr_f
        need = cap - expert_total(jnp.where(gt, 1.0, 0.0).astype(BF16))
        eq_rank = prefix(jnp.where(eq, 1.0, 0.0).astype(BF16))
        sel = gt | (eq & (eq_rank < need))
        rank = prefix(jnp.where(sel, 1.0, 0.0).astype(BF16))
        slot_ref[i] = jnp.where(sel, rank, -1.0)


def _select(aff_t, cap, bt):
    bsz, ne, n = aff_t.shape
    rpe = n // LANES
    a2 = aff_t.reshape(bsz, ne * rpe, LANES)
    slot = pl.pallas_call(
        functools.partial(_select_kernel, cap=cap, rows_per_expert=rpe),
        grid=(bsz // bt,),
        in_specs=[pl.BlockSpec((bt, ne * rpe, LANES), lambda b: (b, 0, 0))],
        out_specs=pl.BlockSpec((bt, ne * rpe, LANES), lambda b: (b, 0, 0)),
        out_shape=jax.ShapeDtypeStruct((bsz, ne * rpe, LANES), F32),
        compiler_params=_cparams(("arbitrary",)),
        name="ec_select",
    )(a2)
    return slot.reshape(bsz, ne, n)


def _gather_kernel(slot_ref, h_ref, o_ref, *, cap):
    slot = slot_ref[0, 0]
    j = lax.broadcasted_iota(jnp.int32, (cap, slot.shape[1]), 0).astype(F32)
    onehot = jnp.where(slot == j, 1.0, 0.0).astype(BF16)
    o_ref[0, 0] = _dot(onehot, h_ref[0]).astype(BF16)


def _gather(slot4, h2, cap):
    bsz, ne, _, n = slot4.shape
    d = h2.shape[2]
    return pl.pallas_call(
        functools.partial(_gather_kernel, cap=cap),
        grid=(bsz, ne),
        in_specs=[pl.BlockSpec((1, 1, 1, n), lambda b, e: (b, e, 0, 0)),
                  pl.BlockSpec((1, n, d), lambda b, e: (b, 0, 0))],
        out_specs=pl.BlockSpec((1, 1, cap, d), lambda b, e: (b, e, 0, 0)),
        out_shape=jax.ShapeDtypeStruct((bsz, ne, cap, d), BF16),
        compiler_params=_cparams(("arbitrary", "arbitrary")),
        name="ec_gather",
    )(slot4, h2)


def _expert_kernel(x_ref, w1_ref, w3_ref, w2_ref, o_ref, acc_ref):
    f = pl.program_id(2)
    bt, _, cap, d = x_ref.shape
    x = x_ref[...].reshape(bt * cap, d)
    a = _dot(x, w1_ref[0].astype(BF16))
    b = _dot(x, w3_ref[0].astype(BF16))
    y = _dot((_silu(a) * b).astype(BF16), w2_ref[0].astype(BF16))

    @pl.when(f == 0)
    def _():
        acc_ref[...] = y

    @pl.when(f > 0)
    def _():
        acc_ref[...] += y

    @pl.when(f == pl.num_programs(2) - 1)
    def _():
        o_ref[...] = acc_ref[...].astype(BF16).reshape(bt, 1, cap, d)


def _experts(xg, w1, w3, w2, bt, ft):
    bsz, ne, cap, d = xg.shape
    ff = w1.shape[2]
    return pl.pallas_call(
        _expert_kernel,
        grid=(ne, bsz // bt, ff // ft),
        in_specs=[pl.BlockSpec((bt, 1, cap, d), lambda e, m, f: (m, e, 0, 0)),
                  pl.BlockSpec((1, d, ft), lambda e, m, f: (e, 0, f)),
                  pl.BlockSpec((1, d, ft), lambda e, m, f: (e, 0, f)),
                  pl.BlockSpec((1, ft, d), lambda e, m, f: (e, f, 0))],
        out_specs=pl.BlockSpec((bt, 1, cap, d), lambda e, m, f: (m, e, 0, 0)),
        out_shape=jax.ShapeDtypeStruct((bsz, ne, cap, d), BF16),
        scratch_shapes=[pltpu.VMEM((bt * cap, d), F32)],
        compiler_params=_cparams(("arbitrary", "arbitrary", "arbitrary")),
        name="ec_experts",
    )(xg, w1, w3, w2)


def _combine_kernel(y_ref, slot_ref, aff_ref, x1_ref, g2_ref, fg_ref, o_ref, *, cap):
    _, ne, _, d = y_ref.shape
    slot = slot_ref[0]
    aff = aff_ref[0]
    j = lax.broadcasted_iota(jnp.int32, (slot.shape[0], cap), 1).astype(F32)
    gated = jnp.concatenate([jnp.where(slot[:, e:e + 1] == j, aff[:, e:e + 1], 0.0).astype(BF16)
                             for e in range(ne)], axis=1)
    moe = _dot(gated, y_ref[0].reshape(ne * cap, d))
    x2 = x1_ref[0] + g2_ref[0] * moe
    o_ref[0] = x2 * lax.rsqrt(jnp.mean(x2 * x2, axis=-1, keepdims=True) + EPS) * fg_ref[...]


def _combine(y, slot_nt, aff_nt, x1, mods3, final_g, cap, tn):
    bsz, ne, _, d = y.shape
    n = x1.shape[1]
    return pl.pallas_call(
        functools.partial(_combine_kernel, cap=cap),
        grid=(bsz, n // tn),
        in_specs=[pl.BlockSpec((1, ne, cap, d), lambda b, t: (b, 0, 0, 0)),
                  pl.BlockSpec((1, tn, ne), lambda b, t: (b, t, 0)),
                  pl.BlockSpec((1, tn, ne), lambda b, t: (b, t, 0)),
                  pl.BlockSpec((1, tn, d), lambda b, t: (b, t, 0)),
                  pl.BlockSpec((1, 1, d), lambda b, t: (b, 0, 5)),
                  pl.BlockSpec((1, d), lambda b, t: (0, 0))],
        out_specs=pl.BlockSpec((1, tn, d), lambda b, t: (b, t, 0)),
        out_shape=jax.ShapeDtypeStruct((bsz, n, d), F32),
        compiler_params=_cparams(("arbitrary", "arbitrary")),
        name="ec_combine",
    )(y, slot_nt, aff_nt, x1, mods3, final_g.reshape(1, d))


def kernel(x, c, ctx, c_ctx, ada_w, ada_b, norm1_g, w_in, hy_conv_w, hy_conv_b, hy_pos_w1, hy_pos_b1, hy_freq1,
           hy_pos_w2, hy_pos_b2, hy_freq2, hy_pos_w3, hy_bias, hy_norm_g, hg_lb, hg_norm_g, w_out, norm2_g,
           router_w, exp_w1, exp_w3, exp_w2, final_g):
    bsz, seq, d = x.shape
    hyw = hy_norm_g.shape[1]
    hgw = w_in.shape[2] - 3 * hyw
    hg = hgw // 5
    cap = CAPACITY_FACTOR * seq // N_EXPERTS

    rows = -(-(bsz + 1) // 8) * 8
    s_rows = jnp.zeros((rows, d), F32).at[:bsz].set(c).at[bsz].set(c_ctx)
    mods3 = _ada(s_rows, ada_w[0], ada_b[0]).reshape(rows, 1, 6 * d)

    w_bf = w_in[0].astype(BF16)
    tn = 2048
    p = _inproj(x, mods3, lambda b: b, norm1_g[0], w_bf, 0, w_in.shape[2] // tn, 1024, tn)
    ctn = hg
    pc = _inproj(ctx, mods3, lambda b: bsz, norm1_g[0], w_bf, (3 * hyw + hg) // ctn, 3, ctx.shape[1], ctn)

    dft_c, dft_s = _dft_tables(seq)
    ct = LANES
    ptab, hm = _hyena_filters(seq, hy_pos_w1[0], hy_pos_b1[0], hy_freq1[0], hy_pos_w2[0], hy_pos_b2[0],
                              hy_freq2[0], hy_pos_w3[0], dft_c, dft_s, ct)
    z = _hyena_conv(p, 0, p, 1, hy_conv_w[0], hy_conv_b[0], dft_c, dft_s, ptab, hm, hy_bias[0], 0, True, ct)
    y_hy = _hyena_conv(z, 0, p, 2, hy_conv_w[0], hy_conv_b[0], dft_c, dft_s, ptab, hm, hy_bias[0], 1, False, ct)

    y_hg = _hgrn(p, 3 * hyw // (hg // HG_HEADS), pc, hg_lb, hg_norm_g[0], 2)

    w_o = w_out[0].astype(BF16)
    x1, h2, aff_t = _outproj(y_hy, y_hg, x, mods3, hy_norm_g[0], norm2_g[0], w_o[:hyw], w_o[hyw:],
                             router_w[0].T, 1024)

    slot = _select(aff_t, cap, 4)
    xg = _gather(slot.reshape(bsz, N_EXPERTS, 1, seq), h2, cap)
    y = _experts(xg, exp_w1[0], exp_w3[0], exp_w2[0], 8, 512)
    return _combine(y, jnp.swapaxes(slot, 1, 2), jnp.swapaxes(aff_t, 1, 2), x1, mods3, final_g, cap, 512)
```

```python
import functools
import math

import numpy as np
import jax
import jax.numpy as jnp
from jax import lax
from jax.experimental import pallas as pl
from jax.experimental.pallas import tpu as pltpu

F32 = jnp.float32
BF16 = jnp.bfloat16
HIGHEST = lax.Precision.HIGHEST

GRID_W = 64
HG_HEADS = 4
HY_SHORT = 3
HY_BANDS = 16
HY_DECAY_TARGET = 1e-2
HY_FAST_DECAY = 0.3
HY_SLOW_DECAY = 1.5
HY_WINDOW_SHIFT = 0.05
N_EXPERTS = 16
CAPACITY_FACTOR = 2
EPS = 1e-6

SCAN_CHUNK = 64
SCAN_BLOCK = 16
FAST_BLOCK = 8
MAX_FAST_EXPONENT = 80.0
LANES = 128
VMEM_LIMIT = 56 * 1024 * 1024


def _cparams(sem):
    return pltpu.CompilerParams(dimension_semantics=sem, vmem_limit_bytes=VMEM_LIMIT)


def _resident(shape, grid_rank):
    zeros = (0,) * len(shape)
    return pl.BlockSpec(shape, {1: lambda a: zeros, 2: lambda a, b: zeros, 3: lambda a, b, c: zeros}[grid_rank],
                        pipeline_mode=pl.Buffered(1))


def _dot(a, b):
    return jnp.dot(a, b, preferred_element_type=F32)


def _dot_nt(a, b):
    return lax.dot_general(a, b, (((1,), (1,)), ((), ())), preferred_element_type=F32)


def _dot_tn(a, b):
    return lax.dot_general(a, b, (((0,), (0,)), ((), ())), preferred_element_type=F32)


def _silu(x):
    return x * jax.nn.sigmoid(x)


def _ada_kernel(s_ref, w_ref, b_ref, o_ref):
    s = _silu(s_ref[...])
    o_ref[...] = jnp.dot(s, w_ref[...], preferred_element_type=F32, precision=HIGHEST) + b_ref[...]


def _ada(s_rows, ada_w, ada_b):
    rows, d = s_rows.shape
    n = ada_w.shape[1]
    tn = 1024
    return pl.pallas_call(
        _ada_kernel,
        grid=(n // tn,),
        in_specs=[pl.BlockSpec((rows, d), lambda j: (0, 0)),
                  pl.BlockSpec((d, tn), lambda j: (0, j)),
                  pl.BlockSpec((1, tn), lambda j: (0, j))],
        out_specs=pl.BlockSpec((rows, tn), lambda j: (0, j)),
        out_shape=jax.ShapeDtypeStruct((rows, n), F32),
        compiler_params=_cparams(("arbitrary",)),
        name="ada",
    )(s_rows, ada_w, ada_b.reshape(1, n))


def _inproj_kernel(x_ref, sh_ref, sc_ref, g_ref, w_ref, o_ref, h_scr):
    @pl.when(pl.program_id(2) == 0)
    def _():
        x = x_ref[0]
        y = x * lax.rsqrt(jnp.mean(x * x, axis=-1, keepdims=True) + EPS) * g_ref[...]
        h_scr[...] = (y * (1.0 + sc_ref[0]) + sh_ref[0]).astype(BF16)

    o_ref[0] = _dot(h_scr[...], w_ref[...])


def _inproj(x, mods3, mod_row, norm_g, w_bf, col_blk0, n_col_blk, tm, tn):
    bsz, seq, d = x.shape
    return pl.pallas_call(
        _inproj_kernel,
        grid=(bsz, seq // tm, n_col_blk),
        in_specs=[pl.BlockSpec((1, tm, d), lambda b, i, j: (b, i, 0)),
                  pl.BlockSpec((1, 1, d), lambda b, i, j: (mod_row(b), 0, 0)),
                  pl.BlockSpec((1, 1, d), lambda b, i, j: (mod_row(b), 0, 1)),
                  pl.BlockSpec((1, d), lambda b, i, j: (0, 0)),
                  pl.BlockSpec((d, tn), lambda b, i, j: (0, col_blk0 + j))],
        out_specs=pl.BlockSpec((1, tm, tn), lambda b, i, j: (b, i, j)),
        out_shape=jax.ShapeDtypeStruct((bsz, seq, n_col_blk * tn), F32),
        scratch_shapes=[pltpu.VMEM((tm, d), BF16)],
        compiler_params=_cparams(("arbitrary", "arbitrary", "arbitrary")),
        name="inproj",
    )(x, mods3, mods3, norm_g.reshape(1, d), w_bf)


def _dft_tables(seq):
    half = seq // 2
    k = np.arange(half, dtype=np.int64)
    idx = (k[:, None] * k[None, :]) % seq
    ang = 2.0 * np.pi * np.arange(seq, dtype=np.float64) / seq
    cos_t = jnp.asarray(np.cos(ang)[idx].astype(np.float32))
    sin_t = jnp.asarray(np.sin(ang)[idx].astype(np.float32))
    return cos_t.astype(BF16), sin_t.astype(BF16)


def _pos_features(seq):
    t = np.linspace(0.0, 1.0, seq, dtype=np.float32)[:, None]
    w = (2.0 * math.pi * np.arange(seq, dtype=np.float32)[:, None] / seq).astype(np.float32)
    f = np.linspace(1e-4, HY_BANDS - 1, HY_BANDS, dtype=np.float32)[None, :]
    fw = (f * w).astype(np.float32)
    return np.concatenate([t, np.cos(fw), -np.sin(fw)], axis=-1).astype(np.float32)


def _filter_kernel(z_ref, w1_ref, b1_ref, f1_ref, w2_ref, b2_ref, f2_ref, w3f_ref, w3b_ref, t_ref, dl_ref,
                   sgn_ref, wk_ref, c2_ref, s2_ref, c_ref, s_ref, p_ref, hm_ref):
    half = c_ref.shape[0]
    n = 4 * half
    taps = {}
    for par in range(2):
        h = jnp.sin(f1_ref[...] * (jnp.dot(z_ref[par], w1_ref[...], preferred_element_type=F32, precision=HIGHEST)
                                   + b1_ref[...]))
        h = jnp.sin(f2_ref[...] * (jnp.dot(h, w2_ref[...], preferred_element_type=F32, precision=HIGHEST)
                                   + b2_ref[...]))
        window = jnp.exp(-t_ref[par] * dl_ref[...]) + HY_WINDOW_SHIFT
        taps["f", par] = jnp.dot(h, w3f_ref[...], preferred_element_type=F32, precision=HIGHEST) * window
        taps["b", par] = jnp.dot(h, w3b_ref[...], preferred_element_type=F32, precision=HIGHEST) * window
    row = lax.broadcasted_iota(jnp.int32, taps["b", 0].shape, 0)
    taps["b", 0] = jnp.where(row == 0, 0.0, taps["b", 0])
    inv = 1.0 / (sum(jnp.sum(jnp.abs(x), axis=0, keepdims=True) for x in taps.values()) + EPS)
    sgn = sgn_ref[...]
    spec = {}
    for key, x in taps.items():
        x = x * inv
        xb = x.astype(BF16)
        spec[key] = (_dot(c_ref[...], xb), _dot(s_ref[...], xb), jnp.sum(x * sgn, axis=0, keepdims=True))
    (fec, fes, fen), (foc, fos, fon) = spec["f", 0], spec["f", 1]
    (bec, bes, ben), (boc, bos, bon) = spec["b", 0], spec["b", 1]
    c2, s2 = c2_ref[...], s2_ref[...]
    wk = 2.0 * wk_ref[...]
    p_ref[0, 0] = (fec + bec) * wk
    p_ref[0, 1] = (bes - fes) * wk
    p_ref[0, 2] = (c2 * foc - s2 * fos + boc) * wk
    p_ref[0, 3] = (bos - c2 * fos - s2 * foc) * wk
    p_ref[0, 4] = (foc + c2 * boc - s2 * bos) * wk
    p_ref[0, 5] = (c2 * bos + s2 * boc - fos) * wk
    hm_ref[0] = jnp.concatenate([fen + ben, bon - fon], axis=0) * (2.0 / n)


def _hyena_filters(seq, pos_w1, pos_b1, freq1, pos_w2, pos_b2, freq2, pos_w3, dft_c, dft_s, ct):
    orders = 2
    half = seq // 2
    ch = pos_w3.shape[1] // (orders * 2)
    zt = _pos_features(seq)
    pdim = -(-zt.shape[1] // LANES) * LANES
    hid = -(-pos_w1.shape[1] // LANES) * LANES
    zt = np.pad(zt, ((0, 0), (0, pdim - zt.shape[1])))
    zt = np.stack([zt[0::2], zt[1::2]])
    pad2 = lambda a, r, c_: jnp.pad(a, ((0, r - a.shape[0]), (0, c_ - a.shape[1])))
    pos_w1 = pad2(pos_w1, pdim, hid)
    pos_w2 = pad2(pos_w2, hid, hid)
    pos_w3 = pad2(pos_w3, hid, pos_w3.shape[1])
    pos_b1, freq1, pos_b2, freq2 = (pad2(a.reshape(1, -1), 1, hid) for a in (pos_b1, freq1, pos_b2, freq2))
    t_col = np.linspace(0.0, 1.0, seq, dtype=np.float32)[:, None]
    t_col = np.stack([t_col[0::2], t_col[1::2]])
    min_decay = math.log(HY_DECAY_TARGET) / HY_SLOW_DECAY
    max_decay = math.log(HY_DECAY_TARGET) / HY_FAST_DECAY
    deltas = np.abs(np.linspace(min_decay, max_decay, ch, dtype=np.float32))[None, :]
    sgn = np.where(np.arange(half) % 2 == 0, 1.0, -1.0).astype(np.float32)[:, None]
    wk = np.full((half, 1), 2.0 / (2 * seq), np.float32)
    wk[0, 0] = 1.0 / (2 * seq)
    ang = 2.0 * np.pi * np.arange(half, dtype=np.float64)[:, None] / seq
    ncb = ch // ct
    const = lambda shape: pl.BlockSpec(shape, lambda o, c: (0,) * len(shape))
    return pl.pallas_call(
        _filter_kernel,
        grid=(orders, ncb),
        in_specs=[const((2, half, pdim)), const((pdim, hid)), const((1, hid)), const((1, hid)),
                  const((hid, hid)), const((1, hid)), const((1, hid)),
                  pl.BlockSpec((hid, ct), lambda o, c: (0, (2 * o) * ncb + c)),
                  pl.BlockSpec((hid, ct), lambda o, c: (0, (2 * o + 1) * ncb + c)),
                  const((2, half, 1)),
                  pl.BlockSpec((1, ct), lambda o, c: (0, c)),
                  const((half, 1)), const((half, 1)), const((half, 1)), const((half, 1)),
                  const((half, half)), const((half, half))],
        out_specs=[pl.BlockSpec((1, 6, half, ct), lambda o, c: (o, 0, 0, c)),
                   pl.BlockSpec((1, 2, ct), lambda o, c: (o, 0, c))],
        out_shape=[jax.ShapeDtypeStruct((orders, 6, half, ch), F32),
                   jax.ShapeDtypeStruct((orders, 2, ch), F32)],
        compiler_params=_cparams(("arbitrary", "arbitrary")),
        name="hyena_filter",
    )(jnp.asarray(zt), pos_w1, pos_b1, freq1, pos_w2, pos_b2, freq2, pos_w3, pos_w3, jnp.asarray(t_col),
      jnp.asarray(deltas), jnp.asarray(sgn), jnp.asarray(wk), jnp.asarray(np.cos(ang).astype(np.float32)),
      jnp.asarray(np.sin(ang).astype(np.float32)), dft_c, dft_s)


def _split_rows(x_ref):
    half = x_ref.shape[1] // 2
    return x_ref[0, pl.ds(0, half, stride=2), :], x_ref[0, pl.ds(1, half, stride=2), :]


def _short_conv_split(p_ref, w_ref, b_ref):
    pe, po = _split_rows(p_ref)
    half = pe.shape[0]
    pos = lax.broadcasted_iota(jnp.int32, pe.shape, 0) % (GRID_W // 2)
    before_even = jnp.where(pos == 0, 0.0, pltpu.roll(po, 1, axis=0))
    after_odd = jnp.where(pos == GRID_W // 2 - 1, 0.0, pltpu.roll(pe, half - 1, axis=0))
    w0, w1, w2 = w_ref[0:1, :], w_ref[1:2, :], w_ref[2:3, :]
    return (b_ref[...] + before_even * w0 + pe * w1 + po * w2,
            b_ref[...] + pe * w0 + po * w1 + after_odd * w2)


def _hyena_kernel(u_ref, g_ref, cwu_ref, cbu_ref, cwg_ref, cbg_ref, c_ref, s_ref, p_ref, hm_ref,
                  bias_ref, sgn_ref, o_ref, acc_ref, *, conv_u, freq_tile):
    half = c_ref.shape[0]
    ct = o_ref.shape[2]
    ue, uo = _short_conv_split(u_ref, cwu_ref, cbu_ref) if conv_u else _split_rows(u_ref)
    ub = jnp.concatenate([ue, uo], axis=1).astype(BF16)
    sgn = sgn_ref[...]
    en = jnp.sum(ue * sgn, axis=0, keepdims=True)
    on = jnp.sum(uo * sgn, axis=0, keepdims=True)
    hm_r, hm_i = hm_ref[0, 0:1, :], hm_ref[0, 1:2, :]
    bias = bias_ref[0]
    acc_ref[...] = jnp.concatenate([ue * bias + sgn * (en * hm_r + on * hm_i),
                                    uo * bias + sgn * (on * hm_r - en * hm_i)], axis=1)
    for ft in range(half // freq_tile):
        lo, hi = ft * freq_tile, (ft + 1) * freq_tile
        dc = _dot(c_ref[lo:hi, :], ub)
        ds = _dot(s_ref[lo:hi, :], ub)
        er, orr, es, os_ = dc[:, :ct], dc[:, ct:], ds[:, :ct], ds[:, ct:]
        p1r, p1i, p2r, p2i, p3r, p3i = (p_ref[0, i, lo:hi, :] for i in range(6))
        ar = er * p1r + es * p1i + orr * p2r + os_ * p2i
        br = er * p3r + es * p3i + orr * p1r + os_ * p1i
        ai = es * p1r - er * p1i + os_ * p2r - orr * p2i
        bi = es * p3r - er * p3i + os_ * p1r - orr * p1i
        acc_ref[...] += (_dot(c_ref[:, lo:hi], jnp.concatenate([ar, br], axis=1).astype(BF16))
                         + _dot(s_ref[:, lo:hi], jnp.concatenate([ai, bi], axis=1).astype(BF16)))
    ge, go = _short_conv_split(g_ref, cwg_ref, cbg_ref)
    o_ref[0, pl.ds(0, half, stride=2), :] = ge * acc_ref[:, :ct]
    o_ref[0, pl.ds(1, half, stride=2), :] = go * acc_ref[:, ct:]


def _hyena_conv(u, u_col_blk, p_hy, gate_col_blk, conv_w, conv_b, dft_c, dft_s, ptab, hm, bias, order,
                conv_u, ct):
    bsz, seq, _ = p_hy.shape
    half = seq // 2
    ch = ptab.shape[3]
    ncb = ch // ct
    sgn = jnp.asarray(np.where(np.arange(half) % 2 == 0, 1.0, -1.0).astype(np.float32)[:, None])
    taps = conv_w.shape[0]
    cbias = conv_b.reshape(1, -1)
    const = lambda shape: pl.BlockSpec(shape, lambda c, b: (0,) * len(shape))
    return pl.pallas_call(
        functools.partial(_hyena_kernel, conv_u=conv_u, freq_tile=half // 4),
        grid=(ncb, bsz),
        in_specs=[pl.BlockSpec((1, seq, ct), lambda c, b: (b, 0, u_col_blk * ncb + c)),
                  pl.BlockSpec((1, seq, ct), lambda c, b: (b, 0, gate_col_blk * ncb + c)),
                  pl.BlockSpec((taps, ct), lambda c, b: (0, u_col_blk * ncb + c)),
                  pl.BlockSpec((1, ct), lambda c, b: (0, u_col_blk * ncb + c)),
                  pl.BlockSpec((taps, ct), lambda c, b: (0, gate_col_blk * ncb + c)),
                  pl.BlockSpec((1, ct), lambda c, b: (0, gate_col_blk * ncb + c)),
                  const((half, half)), const((half, half)),
                  pl.BlockSpec((1, 6, half, ct), lambda c, b: (order, 0, 0, c)),
                  pl.BlockSpec((1, 2, ct), lambda c, b: (order, 0, c)),
                  pl.BlockSpec((1, 1, ct), lambda c, b: (order, 0, c)),
                  const((half, 1))],
        out_specs=pl.BlockSpec((1, seq, ct), lambda c, b: (b, 0, c)),
        out_shape=jax.ShapeDtypeStruct((bsz, seq, ch), F32),
        scratch_shapes=[pltpu.VMEM((half, 2 * ct), F32)],
        compiler_params=_cparams(("arbitrary", "arbitrary")),
        name=f"hyena_conv{order}",
    )(u, p_hy, conv_w, cbias, conv_w, cbias, dft_c, dft_s, ptab, hm, bias.reshape(bias.shape[0], 1, ch), sgn)


def _forget_gate(z, lb):
    f = lb + (1.0 - lb) * jax.nn.sigmoid(z)
    return 1.0 - f, jnp.log(f)


def _scan_consts():
    c, blk = SCAN_CHUNK, SCAN_BLOCK
    r = lax.broadcasted_iota(jnp.int32, (c, c), 0)
    s = lax.broadcasted_iota(jnp.int32, (c, c), 1)
    row = lax.broadcasted_iota(jnp.int32, (c, LANES), 0)
    fb = FAST_BLOCK

    def pair_mask(size, rev):
        same = (r // (2 * size)) == (s // (2 * size))
        hi_r, hi_s = r % (2 * size) >= size, s % (2 * size) >= size
        return same & ((~hi_r & hi_s) if rev else (hi_r & ~hi_s))

    return dict(
        cum_f=(s <= r).astype(F32), cum_b=(s >= r).astype(F32),
        m32_f=(r >= c // 2) & (s < c // 2), m32_b=(r < c // 2) & (s >= c // 2),
        m16_f=pair_mask(blk, False), m16_b=pair_mask(blk, True),
        m8_f=pair_mask(fb, False), m8_b=pair_mask(fb, True),
        md_f=(r // fb == s // fb) & (s <= r), md_b=(r // fb == s // fb) & (s >= r),
        sub=lax.broadcasted_iota(jnp.int32, (fb, LANES), 0),
        row=row, ones=jnp.ones((LANES, LANES), BF16))


def _state_update(st_ref, k, v, g, g_edge):
    kd = (k * jnp.exp(g_edge - g)).astype(BF16)
    st_ref[...] = st_ref[...] * jnp.exp(g_edge) + _dot_tn(v.astype(BF16), kd)


def _chunk_cumsum(logf, cst, reverse):
    return jnp.dot(cst["cum_b" if reverse else "cum_f"], logf, preferred_element_type=F32, precision=HIGHEST)


def _scan_chunk(st_ref, q, k, v, logf, cst, reverse):
    c, blk = SCAN_CHUNK, SCAN_BLOCK
    row = cst["row"]
    g = _chunk_cumsum(logf, cst, reverse)
    o = _dot_nt((q * jnp.exp(g)).astype(BF16), st_ref[...].astype(BF16))
    half = c // 2
    if not reverse:
        ref32 = g[half - 1:half, :]
        ref16 = jnp.where(row < half, g[blk - 1:blk, :], g[half + blk - 1:half + blk, :])
        t32, s32 = row >= half, row < half
        t16, s16 = row % (2 * blk) >= blk, row % (2 * blk) < blk
        g_edge = g[c - 1:c, :]
    else:
        ref32 = g[half:half + 1, :]
        ref16 = jnp.where(row < half, g[blk:blk + 1, :], g[half + blk:half + blk + 1, :])
        t32, s32 = row < half, row >= half
        t16, s16 = row % (2 * blk) < blk, row % (2 * blk) >= blk
        g_edge = g[0:1, :]
    a32 = jnp.where(t32, q * jnp.exp(jnp.minimum(g - ref32, 0.0)), 0.0).astype(BF16)
    b32 = jnp.where(s32, k * jnp.exp(jnp.minimum(ref32 - g, 0.0)), 0.0).astype(BF16)
    a16 = jnp.where(t16, q * jnp.exp(jnp.minimum(g - ref16, 0.0)), 0.0).astype(BF16)
    b16 = jnp.where(s16, k * jnp.exp(jnp.minimum(ref16 - g, 0.0)), 0.0).astype(BF16)
    att = (jnp.where(cst["m32_b" if reverse else "m32_f"], _dot_nt(a32, b32), 0.0)
           + jnp.where(cst["m16_b" if reverse else "m16_f"], _dot_nt(a16, b16), 0.0))
    o = o + _dot(att.astype(BF16), v.astype(BF16))
    pos = row % blk
    for d in range(blk):
        shift = d if not reverse else (c - d) % c
        if d == 0:
            ks, gs, vs = k, g, v
        else:
            ks = pltpu.roll(k, shift, axis=0)
            gs = pltpu.roll(g, shift, axis=0)
            vs = pltpu.roll(v, shift, axis=0)
        valid = (pos >= d) if not reverse else (pos + d < blk)
        w = jnp.where(valid, q * ks * jnp.exp(jnp.minimum(g - gs, 0.0)), 0.0)
        o = o + _dot(w.astype(BF16), cst["ones"]) * vs
    _state_update(st_ref, k, v, g, g_edge)
    return o


def _scan_chunk_factored(st_ref, q, k, v, logf, cst, reverse):
    c, fb = SCAN_CHUNK, FAST_BLOCK
    nb = c // fb
    d = "b" if reverse else "f"
    sub = cst["sub"]
    blocks = lambda x: [x[fb * j:fb * (j + 1), :] for j in range(nb)]
    qs, ks = blocks(q), blocks(k)
    ps = []
    for p in blocks(logf):
        for s in (1, 2, 4):
            if reverse:
                p = p + jnp.where(sub + s < fb, pltpu.roll(p, fb - s, axis=0), 0.0)
            else:
                p = p + jnp.where(sub >= s, pltpu.roll(p, s, axis=0), 0.0)
        ps.append(p)
    last = 0 if reverse else fb - 1
    tot = [jnp.broadcast_to(p[last:last + 1, :], p.shape) for p in ps]
    near, far = [None] * nb, [None] * nb
    run = jnp.zeros_like(tot[0])
    for j in (range(nb - 1, -1, -1) if reverse else range(nb)):
        near[j] = run
        run = run + tot[j]
        far[j] = run
    edge = run
    qe = [qs[j] * jnp.exp(ps[j]) for j in range(nb)]
    kb = [ks[j] * jnp.exp(tot[j] - ps[j]) for j in range(nb)]
    zero = jnp.zeros_like(run)
    cat = lambda parts: jnp.concatenate(parts, axis=0).astype(BF16)

    def level(n):
        a, b = [], []
        for j in range(nb):
            sb, later = j // (2 * n), (j % (2 * n)) >= n
            ref = far[sb * 2 * n + (n if reverse else n - 1)]
            if later != reverse:
                a.append(qe[j] * jnp.exp(near[j] - ref))
                b.append(zero)
            else:
                a.append(zero)
                b.append(kb[j] * jnp.exp(ref - far[j]))
        return _dot_nt(cat(a), cat(b))

    bd = cat([ks[j] * jnp.exp(-ps[j]) for j in range(nb)])
    att = (level(nb // 2)
           + jnp.where(cst["m16_" + d], level(nb // 4), 0.0)
           + jnp.where(cst["m8_" + d], level(nb // 8), 0.0)
           + jnp.where(cst["md_" + d], _dot_nt(cat(qe), bd), 0.0))
    qd = cat([qe[j] * jnp.exp(near[j]) for j in range(nb)])
    o = _dot_nt(qd, st_ref[...].astype(BF16)) + _dot(att.astype(BF16), v.astype(BF16))
    kd = cat([kb[j] * jnp.exp(edge - far[j]) for j in range(nb)])
    st_ref[...] = st_ref[...] * jnp.exp(edge[0:1, :]) + _dot_tn(v.astype(BF16), kd)
    return o


def _hgrn_kernel(q_ref, i_ref, zf_ref, zb_ref, g_ref, ic_ref, zfc_ref, zbc_ref, lb_ref, ng_ref, o_ref,
                 of_ref, ob_ref, st_ref):
    seq = q_ref.shape[1]
    ctx_len = ic_ref.shape[1]
    hd = ng_ref.shape[1]
    heads = q_ref.shape[2] // hd
    c = SCAN_CHUNK
    n_chunks, n_ctx = seq // c, ctx_len // c
    cst = _scan_consts()
    lbr = lb_ref[...]
    e = jnp.exp(lbr - jnp.max(lbr, axis=0, keepdims=True))
    lb = e[0] / jnp.sum(e, axis=0)
    qscale = hd ** -0.5
    lanes = [slice(h * hd, (h + 1) * hd) for h in range(heads)]
    st_f = [st_ref.at[2 * h] for h in range(heads)]
    st_b = [st_ref.at[2 * h + 1] for h in range(heads)]

    st_ref[...] = jnp.zeros_like(st_ref)

    def ctx_body(j, carry):
        rf = pl.multiple_of(j * c, c)
        rb = pl.multiple_of((n_ctx - 1 - j) * c, c)
        for h, ln in enumerate(lanes):
            kf, lf = _forget_gate(zfc_ref[0, pl.ds(rf, c), ln], lb[0:1, ln])
            gf = _chunk_cumsum(lf, cst, False)
            _state_update(st_f[h], kf, ic_ref[0, pl.ds(rf, c), ln], gf, gf[c - 1:c, :])
            kb, lbw = _forget_gate(zbc_ref[0, pl.ds(rb, c), ln], lb[1:2, ln])
            gb = _chunk_cumsum(lbw, cst, True)
            _state_update(st_b[h], kb, ic_ref[0, pl.ds(rb, c), ln], gb, gb[0:1, :])
        return carry

    lax.fori_loop(0, n_ctx, ctx_body, 0)

    def make_body(chunk_fn):
        def body(j, carry):
            rf = pl.multiple_of(j * c, c)
            rb = pl.multiple_of((n_chunks - 1 - j) * c, c)
            for h, ln in enumerate(lanes):
                kf, lf = _forget_gate(zf_ref[0, pl.ds(rf, c), ln], lb[0:1, ln])
                of_ref[pl.ds(rf, c), ln] = chunk_fn(st_f[h], _silu(q_ref[0, pl.ds(rf, c), ln]) * qscale, kf,
                                                    i_ref[0, pl.ds(rf, c), ln], lf, cst, False)
                kb, lbw = _forget_gate(zb_ref[0, pl.ds(rb, c), ln], lb[1:2, ln])
                ob_ref[pl.ds(rb, c), ln] = chunk_fn(st_b[h], _silu(q_ref[0, pl.ds(rb, c), ln]) * qscale, kb,
                                                    i_ref[0, pl.ds(rb, c), ln], lbw, cst, True)
            return carry
        return body

    bounded = jnp.min(lb) >= math.exp(-MAX_FAST_EXPONENT / FAST_BLOCK)

    @pl.when(bounded)
    def _():
        lax.fori_loop(0, n_chunks, make_body(_scan_chunk_factored), 0, unroll=2)

    @pl.when(jnp.logical_not(bounded))
    def _():
        lax.fori_loop(0, n_chunks, make_body(_scan_chunk), 0)

    for ln in lanes:
        o = of_ref[:, ln] + ob_ref[:, ln]
        y = o * lax.rsqrt(jnp.mean(o * o, axis=-1, keepdims=True) + EPS) * ng_ref[...]
        o_ref[0, :, ln] = y * _silu(g_ref[0, :, ln])


def _hgrn(p, col0, pc, hg_lb, norm_g, heads_per_step):
    bsz, seq, _ = p.shape
    ctx_len = pc.shape[1]
    hd = norm_g.shape[0]
    w = heads_per_step * hd
    steps = HG_HEADS // heads_per_step
    lat = lambda grp: pl.BlockSpec((1, seq, w), lambda b, h: (b, 0, (col0 + grp * HG_HEADS) // heads_per_step + h))
    ctx = lambda grp: pl.BlockSpec((1, ctx_len, w), lambda b, h: (b, 0, grp * steps + h))
    return pl.pallas_call(
        _hgrn_kernel,
        grid=(bsz, steps),
        in_specs=[lat(0), lat(1), lat(2), lat(3), lat(4), ctx(0), ctx(1), ctx(2),
                  pl.BlockSpec((hg_lb.shape[0], 2, w), lambda b, h: (0, 0, h)),
                  pl.BlockSpec((1, hd), lambda b, h: (0, 0))],
        out_specs=pl.BlockSpec((1, seq, w), lambda b, h: (b, 0, h)),
        out_shape=jax.ShapeDtypeStruct((bsz, seq, HG_HEADS * hd), F32),
        scratch_shapes=[pltpu.VMEM((seq, w), F32), pltpu.VMEM((seq, w), F32),
                        pltpu.VMEM((2 * heads_per_step, hd, hd), F32)],
        compiler_params=_cparams(("arbitrary", "arbitrary")),
        name="hgrn_scan",
    )(p, p, p, p, p, pc, pc, pc, hg_lb, norm_g.reshape(1, hd))


def _outproj_kernel(yhy_ref, yhg_ref, x_ref, g1_ref, sh2_ref, sc2_ref, hyg_ref, n2g_ref, wt_ref, wb_ref, rw_ref,
                    x1_ref, h2_ref, aff_ref):
    yhy = yhy_ref[0]
    yn = yhy * lax.rsqrt(jnp.mean(yhy * yhy, axis=-1, keepdims=True) + EPS) * hyg_ref[...]
    mix = _dot(yn.astype(BF16), wt_ref[...]) + _dot(yhg_ref[0].astype(BF16), wb_ref[...])
    x1 = x_ref[0] + g1_ref[0] * mix
    x1_ref[0] = x1
    h2 = (x1 * lax.rsqrt(jnp.mean(x1 * x1, axis=-1, keepdims=True) + EPS) * n2g_ref[...]
          * (1.0 + sc2_ref[0]) + sh2_ref[0])
    h2_ref[0] = h2.astype(BF16)
    logits = lax.dot_general(rw_ref[...], h2, (((1,), (1,)), ((), ())), preferred_element_type=F32,
                             precision=HIGHEST)
    ex = jnp.exp(logits - jnp.max(logits, axis=0, keepdims=True))
    aff_ref[0] = ex / jnp.sum(ex, axis=0, keepdims=True)


def _outproj(y_hy, y_hg, x, mods3, hy_norm_g, norm2_g, w_top, w_bot, router_wt, tm):
    bsz, seq, d = x.shape
    wh = y_hy.shape[2]
    ne = router_wt.shape[0]
    mod = lambda k: pl.BlockSpec((1, 1, d), lambda b, i: (b, 0, k))
    const = lambda shape: pl.BlockSpec(shape, lambda b, i: (0,) * len(shape))
    return pl.pallas_call(
        _outproj_kernel,
        grid=(bsz, seq // tm),
        in_specs=[pl.BlockSpec((1, tm, wh), lambda b, i: (b, i, 0)),
                  pl.BlockSpec((1, tm, wh), lambda b, i: (b, i, 0)),
                  pl.BlockSpec((1, tm, d), lambda b, i: (b, i, 0)),
                  mod(2), mod(3), mod(4),
                  const((1, wh)), const((1, d)), const((wh, d)), const((wh, d)), const((ne, d))],
        out_specs=[pl.BlockSpec((1, tm, d), lambda b, i: (b, i, 0)),
                   pl.BlockSpec((1, tm, d), lambda b, i: (b, i, 0)),
                   pl.BlockSpec((1, ne, tm), lambda b, i: (b, 0, i))],
        out_shape=[jax.ShapeDtypeStruct((bsz, seq, d), F32),
                   jax.ShapeDtypeStruct((bsz, seq, d), BF16),
                   jax.ShapeDtypeStruct((bsz, ne, seq), F32)],
        compiler_params=_cparams(("arbitrary", "arbitrary")),
        name="outproj_router",
    )(y_hy, y_hg, x, mods3, mods3, mods3, hy_norm_g.reshape(1, wh), norm2_g.reshape(1, d), w_top, w_bot, router_wt)


def _select_kernel(a_ref, slot_ref, *, cap, rows_per_expert):
    nr = a_ref.shape[1]
    affs = [a_ref[i] for i in range(a_ref.shape[0])]
    r = lax.broadcasted_iota(jnp.int32, (nr, nr), 0)
    s = lax.broadcasted_iota(jnp.int32, (nr, nr), 1)
    same = (r // rows_per_expert) == (s // rows_per_expert)
    grp = same.astype(BF16)
    before = (same & (s < r)).astype(BF16)
    li = lax.broadcasted_iota(jnp.int32, (LANES, LANES), 0)
    lj = lax.broadcasted_iota(jnp.int32, (LANES, LANES), 1)
    ones = jnp.ones((LANES, LANES), BF16)
    strict = (li < lj).astype(BF16)

    def row_total(ind):
        return _dot(ind, ones).astype(BF16)

    def expert_total(ind):
        return _dot(grp, row_total(ind))

    def prefix(ind):
        return _dot(ind, strict) + _dot(before, row_total(ind))

    def search(i, thrs):
        bit = jnp.left_shift(jnp.int32(1), 30 - i)
        out = []
        for a, t in zip(affs, thrs):
            cand = t | bit
            cnt = expert_total(jnp.where(a >= pltpu.bitcast(cand, F32), 1.0, 0.0).astype(BF16))
            out.append(jnp.where(cnt >= cap, cand, t))
        return tuple(out)

    thrs = lax.fori_loop(0, 31, search, tuple(jnp.zeros((nr, LANES), jnp.int32) for _ in affs))
    for i, (a, thr) in enumerate(zip(affs, thrs)):
        thr_f = pltpu.bitcast(thr, F32)
        gt = a > thr_f
        eq = a == thr_f
        need = cap - expert_total(jnp.where(gt, 1.0, 0.0).astype(BF16))
        eq_rank = prefix(jnp.where(eq, 1.0, 0.0).astype(BF16))
        sel = gt | (eq & (eq_rank < need))
        rank = prefix(jnp.where(sel, 1.0, 0.0).astype(BF16))
        slot_ref[i] = jnp.where(sel, rank, -1.0)


def _select(aff_t, cap, bt):
    bsz, ne, n = aff_t.shape
    rpe = n // LANES
    a2 = aff_t.reshape(bsz, ne * rpe, LANES)
    slot = pl.pallas_call(
        functools.partial(_select_kernel, cap=cap, rows_per_expert=rpe),
        grid=(bsz // bt,),
        in_specs=[pl.BlockSpec((bt, ne * rpe, LANES), lambda b: (b, 0, 0))],
        out_specs=pl.BlockSpec((bt, ne * rpe, LANES), lambda b: (b, 0, 0)),
        out_shape=jax.ShapeDtypeStruct((bsz, ne * rpe, LANES), F32),
        compiler_params=_cparams(("arbitrary",)),
        name="ec_select",
    )(a2)
    return slot.reshape(bsz, ne, n)


def _gather_kernel(slot_ref, h_ref, o_ref, *, cap):
    slot = slot_ref[0, 0]
    j = lax.broadcasted_iota(jnp.int32, (cap, slot.shape[1]), 0).astype(F32)
    onehot = jnp.where(slot == j, 1.0, 0.0).astype(BF16)
    o_ref[0, 0] = _dot(onehot, h_ref[0]).astype(BF16)


def _gather(slot4, h2, cap):
    bsz, ne, _, n = slot4.shape
    d = h2.shape[2]
    return pl.pallas_call(
        functools.partial(_gather_kernel, cap=cap),
        grid=(bsz, ne),
        in_specs=[pl.BlockSpec((1, 1, 1, n), lambda b, e: (b, e, 0, 0)),
                  pl.BlockSpec((1, n, d), lambda b, e: (b, 0, 0))],
        out_specs=pl.BlockSpec((1, 1, cap, d), lambda b, e: (b, e, 0, 0)),
        out_shape=jax.ShapeDtypeStruct((bsz, ne, cap, d), BF16),
        compiler_params=_cparams(("arbitrary", "arbitrary")),
        name="ec_gather",
    )(slot4, h2)


def _expert_kernel(x_ref, w1_ref, w3_ref, w2_ref, o_ref, acc_ref):
    f = pl.program_id(2)
    bt, _, cap, d = x_ref.shape
    x = x_ref[...].reshape(bt * cap, d)

    @pl.when(f == 0)
    def _():
        acc_ref[...] = jnp.zeros_like(acc_ref)

    a = _dot(x, w1_ref[0].astype(BF16))
    b = _dot(x, w3_ref[0].astype(BF16))
    acc_ref[...] += _dot((_silu(a) * b).astype(BF16), w2_ref[0].astype(BF16))

    @pl.when(f == pl.num_programs(2) - 1)
    def _():
        o_ref[...] = acc_ref[...].astype(BF16).reshape(bt, 1, cap, d)


def _experts(xg, w1, w3, w2, bt, ft):
    bsz, ne, cap, d = xg.shape
    ff = w1.shape[2]
    return pl.pallas_call(
        _expert_kernel,
        grid=(ne, bsz // bt, ff // ft),
        in_specs=[pl.BlockSpec((bt, 1, cap, d), lambda e, m, f: (m, e, 0, 0)),
                  pl.BlockSpec((1, d, ft), lambda e, m, f: (e, 0, f)),
                  pl.BlockSpec((1, d, ft), lambda e, m, f: (e, 0, f)),
                  pl.BlockSpec((1, ft, d), lambda e, m, f: (e, f, 0))],
        out_specs=pl.BlockSpec((bt, 1, cap, d), lambda e, m, f: (m, e, 0, 0)),
        out_shape=jax.ShapeDtypeStruct((bsz, ne, cap, d), BF16),
        scratch_shapes=[pltpu.VMEM((bt * cap, d), F32)],
        compiler_params=_cparams(("arbitrary", "arbitrary", "arbitrary")),
        name="ec_experts",
    )(xg, w1, w3, w2)


def _combine_kernel(y_ref, slot_ref, aff_ref, x1_ref, g2_ref, fg_ref, o_ref, *, cap):
    _, ne, _, d = y_ref.shape
    slot = slot_ref[0]
    aff = aff_ref[0]
    j = lax.broadcasted_iota(jnp.int32, (slot.shape[0], cap), 1).astype(F32)
    gated = jnp.concatenate([jnp.where(slot[:, e:e + 1] == j, aff[:, e:e + 1], 0.0).astype(BF16)
                             for e in range(ne)], axis=1)
    moe = _dot(gated, y_ref[0].reshape(ne * cap, d))
    x2 = x1_ref[0] + g2_ref[0] * moe
    o_ref[0] = x2 * lax.rsqrt(jnp.mean(x2 * x2, axis=-1, keepdims=True) + EPS) * fg_ref[...]


def _combine(y, slot_nt, aff_nt, x1, mods3, final_g, cap, tn):
    bsz, ne, _, d = y.shape
    n = x1.shape[1]
    return pl.pallas_call(
        functools.partial(_combine_kernel, cap=cap),
        grid=(bsz, n // tn),
        in_specs=[pl.BlockSpec((1, ne, cap, d), lambda b, t: (b, 0, 0, 0)),
                  pl.BlockSpec((1, tn, ne), lambda b, t: (b, t, 0)),
                  pl.BlockSpec((1, tn, ne), lambda b, t: (b, t, 0)),
                  pl.BlockSpec((1, tn, d), lambda b, t: (b, t, 0)),
                  pl.BlockSpec((1, 1, d), lambda b, t: (b, 0, 5)),
                  pl.BlockSpec((1, d), lambda b, t: (0, 0))],
        out_specs=pl.BlockSpec((1, tn, d), lambda b, t: (b, t, 0)),
        out_shape=jax.ShapeDtypeStruct((bsz, n, d), F32),
        compiler_params=_cparams(("arbitrary", "arbitrary")),
        name="ec_combine",
    )(y, slot_nt, aff_nt, x1, mods3, final_g.reshape(1, d))


def kernel(x, c, ctx, c_ctx, ada_w, ada_b, norm1_g, w_in, hy_conv_w, hy_conv_b, hy_pos_w1, hy_pos_b1, hy_freq1,
           hy_pos_w2, hy_pos_b2, hy_freq2, hy_pos_w3, hy_bias, hy_norm_g, hg_lb, hg_norm_g, w_out, norm2_g,
           router_w, exp_w1, exp_w3, exp_w2, final_g):
    bsz, seq, d = x.shape
    hyw = hy_norm_g.shape[1]
    hgw = w_in.shape[2] - 3 * hyw
    hg = hgw // 5
    cap = CAPACITY_FACTOR * seq // N_EXPERTS

    rows = -(-(bsz + 1) // 8) * 8
    s_rows = jnp.zeros((rows, d), F32).at[:bsz].set(c).at[bsz].set(c_ctx)
    mods3 = _ada(s_rows, ada_w[0], ada_b[0]).reshape(rows, 1, 6 * d)

    w_bf = w_in[0].astype(BF16)
    tn = 1024
    p = _inproj(x, mods3, lambda b: b, norm1_g[0], w_bf, 0, w_in.shape[2] // tn, seq, tn)
    w_ctx = w_bf[:, 3 * hyw + hg:3 * hyw + 4 * hg]
    pc = _inproj(ctx, mods3, lambda b: bsz, norm1_g[0], w_ctx, 0, 1, ctx.shape[1], 3 * hg)

    dft_c, dft_s = _dft_tables(seq)
    ct = LANES
    ptab, hm = _hyena_filters(seq, hy_pos_w1[0], hy_pos_b1[0], hy_freq1[0], hy_pos_w2[0], hy_pos_b2[0],
                              hy_freq2[0], hy_pos_w3[0], dft_c, dft_s, 2 * LANES)
    z = _hyena_conv(p, 0, p, 1, hy_conv_w[0], hy_conv_b[0], dft_c, dft_s, ptab, hm, hy_bias[0], 0, True, ct)
    y_hy = _hyena_conv(z, 0, p, 2, hy_conv_w[0], hy_conv_b[0], dft_c, dft_s, ptab, hm, hy_bias[0], 1, False, ct)

    y_hg = _hgrn(p, 3 * hyw // (hg // HG_HEADS), pc, hg_lb, hg_norm_g[0], 2)

    w_o = w_out[0].astype(BF16)
    x1, h2, aff_t = _outproj(y_hy, y_hg, x, mods3, hy_norm_g[0], norm2_g[0], w_o[:hyw], w_o[hyw:],
                             router_w[0].T, 1024)

    slot = _select(aff_t, cap, 8)
    xg = _gather(slot.reshape(bsz, N_EXPERTS, 1, seq), h2, cap)
    y = _experts(xg, exp_w1[0], exp_w3[0], exp_w2[0], 8, 512)
    return _combine(y, jnp.swapaxes(slot, 1, 2), jnp.swapaxes(aff_t, 1, 2), x1, mods3, final_g, cap, 512)
```

```python
import functools
import math

import numpy as np
import jax
import jax.numpy as jnp
from jax import lax
from jax.experimental import pallas as pl
from jax.experimental.pallas import tpu as pltpu

F32 = jnp.float32
BF16 = jnp.bfloat16
HIGHEST = lax.Precision.HIGHEST

GRID_W = 64
HG_HEADS = 4
HY_SHORT = 3
HY_BANDS = 16
HY_DECAY_TARGET = 1e-2
HY_FAST_DECAY = 0.3
HY_SLOW_DECAY = 1.5
HY_WINDOW_SHIFT = 0.05
N_EXPERTS = 16
CAPACITY_FACTOR = 2
EPS = 1e-6

SCAN_CHUNK = 64
SCAN_BLOCK = 16
FAST_BLOCK = 8
MAX_FAST_EXPONENT = 80.0
LANES = 128
VMEM_LIMIT = 56 * 1024 * 1024


def _cparams(sem):
    return pltpu.CompilerParams(dimension_semantics=sem, vmem_limit_bytes=VMEM_LIMIT)


def _resident(shape, grid_rank):
    zeros = (0,) * len(shape)
    return pl.BlockSpec(shape, {1: lambda a: zeros, 2: lambda a, b: zeros, 3: lambda a, b, c: zeros}[grid_rank],
                        pipeline_mode=pl.Buffered(1))


def _dot(a, b):
    return jnp.dot(a, b, preferred_element_type=F32)


def _dot_nt(a, b):
    return lax.dot_general(a, b, (((1,), (1,)), ((), ())), preferred_element_type=F32)


def _dot_tn(a, b):
    return lax.dot_general(a, b, (((0,), (0,)), ((), ())), preferred_element_type=F32)


def _silu(x):
    return x * jax.nn.sigmoid(x)


def _ada_kernel(s_ref, w_ref, b_ref, o_ref):
    s = _silu(s_ref[...])
    o_ref[...] = jnp.dot(s, w_ref[...], preferred_element_type=F32, precision=HIGHEST) + b_ref[...]


def _ada(s_rows, ada_w, ada_b):
    rows, d = s_rows.shape
    n = ada_w.shape[1]
    tn = 1024
    return pl.pallas_call(
        _ada_kernel,
        grid=(n // tn,),
        in_specs=[pl.BlockSpec((rows, d), lambda j: (0, 0)),
                  pl.BlockSpec((d, tn), lambda j: (0, j)),
                  pl.BlockSpec((1, tn), lambda j: (0, j))],
        out_specs=pl.BlockSpec((rows, tn), lambda j: (0, j)),
        out_shape=jax.ShapeDtypeStruct((rows, n), F32),
        compiler_params=_cparams(("arbitrary",)),
        name="ada",
    )(s_rows, ada_w, ada_b.reshape(1, n))


def _inproj_kernel(x_ref, sh_ref, sc_ref, g_ref, w_ref, o_ref, h_scr):
    @pl.when(pl.program_id(2) == 0)
    def _():
        x = x_ref[0]
        y = x * lax.rsqrt(jnp.mean(x * x, axis=-1, keepdims=True) + EPS) * g_ref[...]
        h_scr[...] = (y * (1.0 + sc_ref[0]) + sh_ref[0]).astype(BF16)

    o_ref[0] = _dot(h_scr[...], w_ref[...])


def _inproj(x, mods3, mod_row, norm_g, w_bf, col_blk0, n_col_blk, tm, tn):
    bsz, seq, d = x.shape
    return pl.pallas_call(
        _inproj_kernel,
        grid=(bsz, seq // tm, n_col_blk),
        in_specs=[pl.BlockSpec((1, tm, d), lambda b, i, j: (b, i, 0)),
                  pl.BlockSpec((1, 1, d), lambda b, i, j: (mod_row(b), 0, 0)),
                  pl.BlockSpec((1, 1, d), lambda b, i, j: (mod_row(b), 0, 1)),
                  pl.BlockSpec((1, d), lambda b, i, j: (0, 0)),
                  pl.BlockSpec((d, tn), lambda b, i, j: (0, col_blk0 + j))],
        out_specs=pl.BlockSpec((1, tm, tn), lambda b, i, j: (b, i, j)),
        out_shape=jax.ShapeDtypeStruct((bsz, seq, n_col_blk * tn), F32),
        scratch_shapes=[pltpu.VMEM((tm, d), BF16)],
        compiler_params=_cparams(("arbitrary", "arbitrary", "arbitrary")),
        name="inproj",
    )(x, mods3, mods3, norm_g.reshape(1, d), w_bf)


def _dft_tables(seq):
    half = seq // 2
    k = np.arange(half, dtype=np.int64)
    idx = (k[:, None] * k[None, :]) % seq
    ang = 2.0 * np.pi * np.arange(seq, dtype=np.float64) / seq
    cos_t = jnp.asarray(np.cos(ang)[idx].astype(np.float32))
    sin_t = jnp.asarray(np.sin(ang)[idx].astype(np.float32))
    return cos_t.astype(BF16), sin_t.astype(BF16)


def _pos_features(seq):
    t = np.linspace(0.0, 1.0, seq, dtype=np.float32)[:, None]
    w = (2.0 * math.pi * np.arange(seq, dtype=np.float32)[:, None] / seq).astype(np.float32)
    f = np.linspace(1e-4, HY_BANDS - 1, HY_BANDS, dtype=np.float32)[None, :]
    fw = (f * w).astype(np.float32)
    return np.concatenate([t, np.cos(fw), -np.sin(fw)], axis=-1).astype(np.float32)


def _filter_kernel(z_ref, w1_ref, b1_ref, f1_ref, w2_ref, b2_ref, f2_ref, w3f_ref, w3b_ref, t_ref, dl_ref,
                   sgn_ref, wk_ref, c2_ref, s2_ref, c_ref, s_ref, p_ref, hm_ref):
    half = c_ref.shape[0]
    n = 4 * half
    taps = {}
    for par in range(2):
        h = jnp.sin(f1_ref[...] * (jnp.dot(z_ref[par], w1_ref[...], preferred_element_type=F32, precision=HIGHEST)
                                   + b1_ref[...]))
        h = jnp.sin(f2_ref[...] * (jnp.dot(h, w2_ref[...], preferred_element_type=F32, precision=HIGHEST)
                                   + b2_ref[...]))
        window = jnp.exp(-t_ref[par] * dl_ref[...]) + HY_WINDOW_SHIFT
        taps["f", par] = jnp.dot(h, w3f_ref[...], preferred_element_type=F32, precision=HIGHEST) * window
        taps["b", par] = jnp.dot(h, w3b_ref[...], preferred_element_type=F32, precision=HIGHEST) * window
    row = lax.broadcasted_iota(jnp.int32, taps["b", 0].shape, 0)
    taps["b", 0] = jnp.where(row == 0, 0.0, taps["b", 0])
    inv = 1.0 / (sum(jnp.sum(jnp.abs(x), axis=0, keepdims=True) for x in taps.values()) + EPS)
    sgn = sgn_ref[...]
    spec = {}
    for key, x in taps.items():
        x = x * inv
        xb = x.astype(BF16)
        spec[key] = (_dot(c_ref[...], xb), _dot(s_ref[...], xb), jnp.sum(x * sgn, axis=0, keepdims=True))
    (fec, fes, fen), (foc, fos, fon) = spec["f", 0], spec["f", 1]
    (bec, bes, ben), (boc, bos, bon) = spec["b", 0], spec["b", 1]
    c2, s2 = c2_ref[...], s2_ref[...]
    wk = 2.0 * wk_ref[...]
    p_ref[0, 0] = (fec + bec) * wk
    p_ref[0, 1] = (bes - fes) * wk
    p_ref[0, 2] = (c2 * foc - s2 * fos + boc) * wk
    p_ref[0, 3] = (bos - c2 * fos - s2 * foc) * wk
    p_ref[0, 4] = (foc + c2 * boc - s2 * bos) * wk
    p_ref[0, 5] = (c2 * bos + s2 * boc - fos) * wk
    hm_ref[0] = jnp.concatenate([fen + ben, bon - fon], axis=0) * (2.0 / n)


def _hyena_filters(seq, pos_w1, pos_b1, freq1, pos_w2, pos_b2, freq2, pos_w3, dft_c, dft_s, ct):
    orders = 2
    half = seq // 2
    ch = pos_w3.shape[1] // (orders * 2)
    zt = _pos_features(seq)
    pdim = -(-zt.shape[1] // LANES) * LANES
    hid = -(-pos_w1.shape[1] // LANES) * LANES
    zt = np.pad(zt, ((0, 0), (0, pdim - zt.shape[1])))
    zt = np.stack([zt[0::2], zt[1::2]])
    pad2 = lambda a, r, c_: jnp.pad(a, ((0, r - a.shape[0]), (0, c_ - a.shape[1])))
    pos_w1 = pad2(pos_w1, pdim, hid)
    pos_w2 = pad2(pos_w2, hid, hid)
    pos_w3 = pad2(pos_w3, hid, pos_w3.shape[1])
    pos_b1, freq1, pos_b2, freq2 = (pad2(a.reshape(1, -1), 1, hid) for a in (pos_b1, freq1, pos_b2, freq2))
    t_col = np.linspace(0.0, 1.0, seq, dtype=np.float32)[:, None]
    t_col = np.stack([t_col[0::2], t_col[1::2]])
    min_decay = math.log(HY_DECAY_TARGET) / HY_SLOW_DECAY
    max_decay = math.log(HY_DECAY_TARGET) / HY_FAST_DECAY
    deltas = np.abs(np.linspace(min_decay, max_decay, ch, dtype=np.float32))[None, :]
    sgn = np.where(np.arange(half) % 2 == 0, 1.0, -1.0).astype(np.float32)[:, None]
    wk = np.full((half, 1), 2.0 / (2 * seq), np.float32)
    wk[0, 0] = 1.0 / (2 * seq)
    ang = 2.0 * np.pi * np.arange(half, dtype=np.float64)[:, None] / seq
    ncb = ch // ct
    const = lambda shape: pl.BlockSpec(shape, lambda o, c: (0,) * len(shape))
    return pl.pallas_call(
        _filter_kernel,
        grid=(orders, ncb),
        in_specs=[const((2, half, pdim)), const((pdim, hid)), const((1, hid)), const((1, hid)),
                  const((hid, hid)), const((1, hid)), const((1, hid)),
                  pl.BlockSpec((hid, ct), lambda o, c: (0, (2 * o) * ncb + c)),
                  pl.BlockSpec((hid, ct), lambda o, c: (0, (2 * o + 1) * ncb + c)),
                  const((2, half, 1)),
                  pl.BlockSpec((1, ct), lambda o, c: (0, c)),
                  const((half, 1)), const((half, 1)), const((half, 1)), const((half, 1)),
                  const((half, half)), const((half, half))],
        out_specs=[pl.BlockSpec((1, 6, half, ct), lambda o, c: (o, 0, 0, c)),
                   pl.BlockSpec((1, 2, ct), lambda o, c: (o, 0, c))],
        out_shape=[jax.ShapeDtypeStruct((orders, 6, half, ch), F32),
                   jax.ShapeDtypeStruct((orders, 2, ch), F32)],
        compiler_params=_cparams(("arbitrary", "arbitrary")),
        name="hyena_filter",
    )(jnp.asarray(zt), pos_w1, pos_b1, freq1, pos_w2, pos_b2, freq2, pos_w3, pos_w3, jnp.asarray(t_col),
      jnp.asarray(deltas), jnp.asarray(sgn), jnp.asarray(wk), jnp.asarray(np.cos(ang).astype(np.float32)),
      jnp.asarray(np.sin(ang).astype(np.float32)), dft_c, dft_s)


def _split_rows(x_ref):
    half = x_ref.shape[1] // 2
    return x_ref[0, pl.ds(0, half, stride=2), :], x_ref[0, pl.ds(1, half, stride=2), :]


def _short_conv_split(p_ref, w_ref, b_ref):
    pe, po = _split_rows(p_ref)
    half = pe.shape[0]
    pos = lax.broadcasted_iota(jnp.int32, pe.shape, 0) % (GRID_W // 2)
    before_even = jnp.where(pos == 0, 0.0, pltpu.roll(po, 1, axis=0))
    after_odd = jnp.where(pos == GRID_W // 2 - 1, 0.0, pltpu.roll(pe, half - 1, axis=0))
    w0, w1, w2 = w_ref[0:1, :], w_ref[1:2, :], w_ref[2:3, :]
    return (b_ref[...] + before_even * w0 + pe * w1 + po * w2,
            b_ref[...] + pe * w0 + po * w1 + after_odd * w2)


def _hyena_kernel(u_ref, g_ref, cwu_ref, cbu_ref, cwg_ref, cbg_ref, fwd_ref, inv_ref, p_ref, hm_ref,
                  bias_ref, sgn_ref, o_ref, spec_ref, *, conv_u, freq_tile):
    half = inv_ref.shape[0]
    ct = o_ref.shape[2]
    ue, uo = _short_conv_split(u_ref, cwu_ref, cbu_ref) if conv_u else _split_rows(u_ref)
    ub = jnp.concatenate([ue, uo], axis=1).astype(BF16)
    for ft in range(half // freq_tile):
        lo, hi = ft * freq_tile, (ft + 1) * freq_tile
        dc = _dot(fwd_ref[lo:hi, :], ub)
        ds = _dot(fwd_ref[half + lo:half + hi, :], ub)
        er, orr, es, os_ = dc[:, :ct], dc[:, ct:], ds[:, :ct], ds[:, ct:]
        p1r, p1i, p2r, p2i, p3r, p3i = (p_ref[0, i, lo:hi, :] for i in range(6))
        ar = er * p1r + es * p1i + orr * p2r + os_ * p2i
        br = er * p3r + es * p3i + orr * p1r + os_ * p1i
        ai = es * p1r - er * p1i + os_ * p2r - orr * p2i
        bi = es * p3r - er * p3i + os_ * p1r - orr * p1i
        spec_ref[lo:hi, :] = jnp.concatenate([ar, br], axis=1).astype(BF16)
        spec_ref[half + lo:half + hi, :] = jnp.concatenate([ai, bi], axis=1).astype(BF16)
    y = _dot(inv_ref[...], spec_ref[...])
    sgn = sgn_ref[...]
    en = jnp.sum(ue * sgn, axis=0, keepdims=True)
    on = jnp.sum(uo * sgn, axis=0, keepdims=True)
    hm_r, hm_i = hm_ref[0, 0:1, :], hm_ref[0, 1:2, :]
    bias = bias_ref[0]
    ge, go = _short_conv_split(g_ref, cwg_ref, cbg_ref)
    o_ref[0, pl.ds(0, half, stride=2), :] = ge * (y[:, :ct] + ue * bias + sgn * (en * hm_r + on * hm_i))
    o_ref[0, pl.ds(1, half, stride=2), :] = go * (y[:, ct:] + uo * bias + sgn * (on * hm_r - en * hm_i))


def _hyena_conv(u, u_col_blk, p_hy, gate_col_blk, conv_w, conv_b, dft_c, dft_s, ptab, hm, bias, order,
                conv_u, ct):
    bsz, seq, _ = p_hy.shape
    half = seq // 2
    ch = ptab.shape[3]
    ncb = ch // ct
    sgn = jnp.asarray(np.where(np.arange(half) % 2 == 0, 1.0, -1.0).astype(np.float32)[:, None])
    taps = conv_w.shape[0]
    cbias = conv_b.reshape(1, -1)
    const = lambda shape: pl.BlockSpec(shape, lambda c, b: (0,) * len(shape))
    return pl.pallas_call(
        functools.partial(_hyena_kernel, conv_u=conv_u, freq_tile=half // 4),
        grid=(ncb, bsz),
        in_specs=[pl.BlockSpec((1, seq, ct), lambda c, b: (b, 0, u_col_blk * ncb + c)),
                  pl.BlockSpec((1, seq, ct), lambda c, b: (b, 0, gate_col_blk * ncb + c)),
                  pl.BlockSpec((taps, ct), lambda c, b: (0, u_col_blk * ncb + c)),
                  pl.BlockSpec((1, ct), lambda c, b: (0, u_col_blk * ncb + c)),
                  pl.BlockSpec((taps, ct), lambda c, b: (0, gate_col_blk * ncb + c)),
                  pl.BlockSpec((1, ct), lambda c, b: (0, gate_col_blk * ncb + c)),
                  const((2 * half, half)), const((half, 2 * half)),
                  pl.BlockSpec((1, 6, half, ct), lambda c, b: (order, 0, 0, c)),
                  pl.BlockSpec((1, 2, ct), lambda c, b: (order, 0, c)),
                  pl.BlockSpec((1, 1, ct), lambda c, b: (order, 0, c)),
                  const((half, 1))],
        out_specs=pl.BlockSpec((1, seq, ct), lambda c, b: (b, 0, c)),
        out_shape=jax.ShapeDtypeStruct((bsz, seq, ch), F32),
        scratch_shapes=[pltpu.VMEM((2 * half, 2 * ct), BF16)],
        compiler_params=_cparams(("arbitrary", "arbitrary")),
        name=f"hyena_conv{order}",
    )(u, p_hy, conv_w, cbias, conv_w, cbias, jnp.concatenate([dft_c, dft_s], axis=0),
      jnp.concatenate([dft_c, dft_s], axis=1), ptab, hm, bias.reshape(bias.shape[0], 1, ch), sgn)


def _forget_gate(z, lb):
    f = lb + (1.0 - lb) * jax.nn.sigmoid(z)
    return 1.0 - f, jnp.log(f)


def _scan_consts():
    c, blk = SCAN_CHUNK, SCAN_BLOCK
    r = lax.broadcasted_iota(jnp.int32, (c, c), 0)
    s = lax.broadcasted_iota(jnp.int32, (c, c), 1)
    row = lax.broadcasted_iota(jnp.int32, (c, LANES), 0)
    fb = FAST_BLOCK

    def pair_mask(size, rev):
        same = (r // (2 * size)) == (s // (2 * size))
        hi_r, hi_s = r % (2 * size) >= size, s % (2 * size) >= size
        return same & ((~hi_r & hi_s) if rev else (hi_r & ~hi_s))

    return dict(
        cum_f=(s <= r).astype(F32), cum_b=(s >= r).astype(F32),
        m32_f=(r >= c // 2) & (s < c // 2), m32_b=(r < c // 2) & (s >= c // 2),
        m16_f=pair_mask(blk, False), m16_b=pair_mask(blk, True),
        m8_f=pair_mask(fb, False), m8_b=pair_mask(fb, True),
        md_f=(r // fb == s // fb) & (s <= r), md_b=(r // fb == s // fb) & (s >= r),
        sub=lax.broadcasted_iota(jnp.int32, (fb, LANES), 0),
        row=row, ones=jnp.ones((LANES, LANES), BF16))


def _state_update(st_ref, k, v, g, g_edge):
    kd = (k * jnp.exp(g_edge - g)).astype(BF16)
    st_ref[...] = st_ref[...] * jnp.exp(g_edge) + _dot_tn(v.astype(BF16), kd)


def _chunk_cumsum(logf, cst, reverse):
    return jnp.dot(cst["cum_b" if reverse else "cum_f"], logf, preferred_element_type=F32, precision=HIGHEST)


def _scan_chunk(st_ref, q, k, v, logf, cst, reverse):
    c, blk = SCAN_CHUNK, SCAN_BLOCK
    row = cst["row"]
    g = _chunk_cumsum(logf, cst, reverse)
    o = _dot_nt((q * jnp.exp(g)).astype(BF16), st_ref[...].astype(BF16))
    half = c // 2
    if not reverse:
        ref32 = g[half - 1:half, :]
        ref16 = jnp.where(row < half, g[blk - 1:blk, :], g[half + blk - 1:half + blk, :])
        t32, s32 = row >= half, row < half
        t16, s16 = row % (2 * blk) >= blk, row % (2 * blk) < blk
        g_edge = g[c - 1:c, :]
    else:
        ref32 = g[half:half + 1, :]
        ref16 = jnp.where(row < half, g[blk:blk + 1, :], g[half + blk:half + blk + 1, :])
        t32, s32 = row < half, row >= half
        t16, s16 = row % (2 * blk) < blk, row % (2 * blk) >= blk
        g_edge = g[0:1, :]
    a32 = jnp.where(t32, q * jnp.exp(jnp.minimum(g - ref32, 0.0)), 0.0).astype(BF16)
    b32 = jnp.where(s32, k * jnp.exp(jnp.minimum(ref32 - g, 0.0)), 0.0).astype(BF16)
    a16 = jnp.where(t16, q * jnp.exp(jnp.minimum(g - ref16, 0.0)), 0.0).astype(BF16)
    b16 = jnp.where(s16, k * jnp.exp(jnp.minimum(ref16 - g, 0.0)), 0.0).astype(BF16)
    att = (jnp.where(cst["m32_b" if reverse else "m32_f"], _dot_nt(a32, b32), 0.0)
           + jnp.where(cst["m16_b" if reverse else "m16_f"], _dot_nt(a16, b16), 0.0))
    o = o + _dot(att.astype(BF16), v.astype(BF16))
    pos = row % blk
    for d in range(blk):
        shift = d if not reverse else (c - d) % c
        if d == 0:
            ks, gs, vs = k, g, v
        else:
            ks = pltpu.roll(k, shift, axis=0)
            gs = pltpu.roll(g, shift, axis=0)
            vs = pltpu.roll(v, shift, axis=0)
        valid = (pos >= d) if not reverse else (pos + d < blk)
        w = jnp.where(valid, q * ks * jnp.exp(jnp.minimum(g - gs, 0.0)), 0.0)
        o = o + _dot(w.astype(BF16), cst["ones"]) * vs
    _state_update(st_ref, k, v, g, g_edge)
    return o


def _chunk_prepare(q, k, v, logf, cst, reverse):
    c, fb = SCAN_CHUNK, FAST_BLOCK
    nb = c // fb
    d = "b" if reverse else "f"
    sub = cst["sub"]
    blocks = lambda x: [x[fb * j:fb * (j + 1), :] for j in range(nb)]
    qs, ks = blocks(q), blocks(k)
    ps = []
    for p in blocks(logf):
        for s in (1, 2, 4):
            if reverse:
                p = p + jnp.where(sub + s < fb, pltpu.roll(p, fb - s, axis=0), 0.0)
            else:
                p = p + jnp.where(sub >= s, pltpu.roll(p, s, axis=0), 0.0)
        ps.append(p)
    last = 0 if reverse else fb - 1
    tot = [jnp.broadcast_to(p[last:last + 1, :], p.shape) for p in ps]
    near, far = [None] * nb, [None] * nb
    run = jnp.zeros_like(tot[0])
    for j in (range(nb - 1, -1, -1) if reverse else range(nb)):
        near[j] = run
        run = run + tot[j]
        far[j] = run
    edge = run
    qe = [qs[j] * jnp.exp(ps[j]) for j in range(nb)]
    kb = [ks[j] * jnp.exp(tot[j] - ps[j]) for j in range(nb)]
    zero = jnp.zeros_like(run)
    cat = lambda parts: jnp.concatenate(parts, axis=0).astype(BF16)

    def level(n):
        a, b = [], []
        for j in range(nb):
            sb, later = j // (2 * n), (j % (2 * n)) >= n
            ref = far[sb * 2 * n + (n if reverse else n - 1)]
            if later != reverse:
                a.append(qe[j] * jnp.exp(near[j] - ref))
                b.append(zero)
            else:
                a.append(zero)
                b.append(kb[j] * jnp.exp(ref - far[j]))
        return _dot_nt(cat(a), cat(b))

    bd = cat([ks[j] * jnp.exp(-ps[j]) for j in range(nb)])
    att = (level(nb // 2)
           + jnp.where(cst["m16_" + d], level(nb // 4), 0.0)
           + jnp.where(cst["m8_" + d], level(nb // 8), 0.0)
           + jnp.where(cst["md_" + d], _dot_nt(cat(qe), bd), 0.0))
    qd = cat([qe[j] * jnp.exp(near[j]) for j in range(nb)])
    kd = cat([kb[j] * jnp.exp(edge - far[j]) for j in range(nb)])
    vb = v.astype(BF16)
    return att.astype(BF16), qd, vb, _dot_tn(vb, kd), jnp.exp(edge[0:1, :])


def _chunk_finish(st_ref, prepared):
    att, qd, vb, kv, decay = prepared
    o = _dot_nt(qd, st_ref[...].astype(BF16)) + _dot(att, vb)
    st_ref[...] = st_ref[...] * decay + kv
    return o


def _hgrn_kernel(q_ref, i_ref, zf_ref, zb_ref, g_ref, ic_ref, zfc_ref, zbc_ref, lb_ref, ng_ref, o_ref,
                 of_ref, ob_ref, st_ref):
    seq = q_ref.shape[1]
    ctx_len = ic_ref.shape[1]
    hd = ng_ref.shape[1]
    heads = q_ref.shape[2] // hd
    c = SCAN_CHUNK
    n_chunks, n_ctx = seq // c, ctx_len // c
    cst = _scan_consts()
    lbr = lb_ref[...]
    e = jnp.exp(lbr - jnp.max(lbr, axis=0, keepdims=True))
    lb = e[0] / jnp.sum(e, axis=0)
    qscale = hd ** -0.5
    lanes = [slice(h * hd, (h + 1) * hd) for h in range(heads)]
    st_f = [st_ref.at[2 * h] for h in range(heads)]
    st_b = [st_ref.at[2 * h + 1] for h in range(heads)]

    st_ref[...] = jnp.zeros_like(st_ref)

    def ctx_body(j, carry):
        rf = pl.multiple_of(j * c, c)
        rb = pl.multiple_of((n_ctx - 1 - j) * c, c)
        for h, ln in enumerate(lanes):
            for st, z_ref, r, lb_row, rev in ((st_f[h], zfc_ref, rf, lb[0:1, ln], False),
                                              (st_b[h], zbc_ref, rb, lb[1:2, ln], True)):
                k, logf = _forget_gate(z_ref[0, pl.ds(r, c), ln], lb_row)
                *_, kv, decay = _chunk_prepare(jnp.zeros_like(k), k, ic_ref[0, pl.ds(r, c), ln], logf, cst, rev)
                st[...] = st[...] * decay + kv
        return carry

    lax.fori_loop(0, n_ctx, ctx_body, 0)

    def rows(j):
        if isinstance(j, int):
            return j * c, (n_chunks - 1 - j) * c
        return pl.multiple_of(j * c, c), pl.multiple_of((n_chunks - 1 - j) * c, c)

    def operands(r, ln, z_ref, lb_row):
        k, logf = _forget_gate(z_ref[0, pl.ds(r, c), ln], lb_row)
        return _silu(q_ref[0, pl.ds(r, c), ln]) * qscale, k, i_ref[0, pl.ds(r, c), ln], logf

    def direct_body(j, carry):
        rf, rb = rows(j)
        for h, ln in enumerate(lanes):
            of_ref[pl.ds(rf, c), ln] = _scan_chunk(st_f[h], *operands(rf, ln, zf_ref, lb[0:1, ln]), cst, False)
            ob_ref[pl.ds(rb, c), ln] = _scan_chunk(st_b[h], *operands(rb, ln, zb_ref, lb[1:2, ln]), cst, True)
        return carry

    def prepare(j):
        rf, rb = rows(j)
        out = []
        for ln in lanes:
            out.append(_chunk_prepare(*operands(rf, ln, zf_ref, lb[0:1, ln]), cst, False))
            out.append(_chunk_prepare(*operands(rb, ln, zb_ref, lb[1:2, ln]), cst, True))
        return tuple(out)

    def finish(j, prepared):
        rf, rb = rows(j)
        for h, ln in enumerate(lanes):
            of_ref[pl.ds(rf, c), ln] = _chunk_finish(st_f[h], prepared[2 * h])
            ob_ref[pl.ds(rb, c), ln] = _chunk_finish(st_b[h], prepared[2 * h + 1])

    def factored_body(j, prepared):
        upcoming = prepare(j + 1)
        finish(j, prepared)
        return upcoming

    bounded = jnp.min(lb) >= math.exp(-MAX_FAST_EXPONENT / FAST_BLOCK)

    @pl.when(bounded)
    def _():
        finish(n_chunks - 1, lax.fori_loop(0, n_chunks - 1, factored_body, prepare(0), unroll=2))

    @pl.when(jnp.logical_not(bounded))
    def _():
        lax.fori_loop(0, n_chunks, direct_body, 0)

    for ln in lanes:
        o = of_ref[:, ln] + ob_ref[:, ln]
        y = o * lax.rsqrt(jnp.mean(o * o, axis=-1, keepdims=True) + EPS) * ng_ref[...]
        o_ref[0, :, ln] = y * _silu(g_ref[0, :, ln])


def _hgrn(p, col0, pc, hg_lb, norm_g, heads_per_step):
    bsz, seq, _ = p.shape
    ctx_len = pc.shape[1]
    hd = norm_g.shape[0]
    w = heads_per_step * hd
    steps = HG_HEADS // heads_per_step
    lat = lambda grp: pl.BlockSpec((1, seq, w), lambda b, h: (b, 0, (col0 + grp * HG_HEADS) // heads_per_step + h))
    ctx = lambda grp: pl.BlockSpec((1, ctx_len, w), lambda b, h: (b, 0, grp * steps + h))
    return pl.pallas_call(
        _hgrn_kernel,
        grid=(bsz, steps),
        in_specs=[lat(0), lat(1), lat(2), lat(3), lat(4), ctx(0), ctx(1), ctx(2),
                  pl.BlockSpec((hg_lb.shape[0], 2, w), lambda b, h: (0, 0, h)),
                  pl.BlockSpec((1, hd), lambda b, h: (0, 0))],
        out_specs=pl.BlockSpec((1, seq, w), lambda b, h: (b, 0, h)),
        out_shape=jax.ShapeDtypeStruct((bsz, seq, HG_HEADS * hd), F32),
        scratch_shapes=[pltpu.VMEM((seq, w), F32), pltpu.VMEM((seq, w), F32),
                        pltpu.VMEM((2 * heads_per_step, hd, hd), F32)],
        compiler_params=_cparams(("arbitrary", "arbitrary")),
        name="hgrn_scan",
    )(p, p, p, p, p, pc, pc, pc, hg_lb, norm_g.reshape(1, hd))


def _outproj_kernel(yhy_ref, yhg_ref, x_ref, g1_ref, sh2_ref, sc2_ref, hyg_ref, n2g_ref, wt_ref, wb_ref, rw_ref,
                    x1_ref, h2_ref, aff_ref):
    yhy = yhy_ref[0]
    yn = yhy * lax.rsqrt(jnp.mean(yhy * yhy, axis=-1, keepdims=True) + EPS) * hyg_ref[...]
    mix = _dot(yn.astype(BF16), wt_ref[...]) + _dot(yhg_ref[0].astype(BF16), wb_ref[...])
    x1 = x_ref[0] + g1_ref[0] * mix
    x1_ref[0] = x1
    h2 = (x1 * lax.rsqrt(jnp.mean(x1 * x1, axis=-1, keepdims=True) + EPS) * n2g_ref[...]
          * (1.0 + sc2_ref[0]) + sh2_ref[0])
    h2_ref[0] = h2.astype(BF16)
    logits = lax.dot_general(rw_ref[...], h2, (((1,), (1,)), ((), ())), preferred_element_type=F32,
                             precision=HIGHEST)
    ex = jnp.exp(logits - jnp.max(logits, axis=0, keepdims=True))
    aff_ref[0] = ex / jnp.sum(ex, axis=0, keepdims=True)


def _outproj(y_hy, y_hg, x, mods3, hy_norm_g, norm2_g, w_top, w_bot, router_wt, tm):
    bsz, seq, d = x.shape
    wh = y_hy.shape[2]
    ne = router_wt.shape[0]
    mod = lambda k: pl.BlockSpec((1, 1, d), lambda b, i: (b, 0, k))
    const = lambda shape: pl.BlockSpec(shape, lambda b, i: (0,) * len(shape))
    return pl.pallas_call(
        _outproj_kernel,
        grid=(bsz, seq // tm),
        in_specs=[pl.BlockSpec((1, tm, wh), lambda b, i: (b, i, 0)),
                  pl.BlockSpec((1, tm, wh), lambda b, i: (b, i, 0)),
                  pl.BlockSpec((1, tm, d), lambda b, i: (b, i, 0)),
                  mod(2), mod(3), mod(4),
                  const((1, wh)), const((1, d)), const((wh, d)), const((wh, d)), const((ne, d))],
        out_specs=[pl.BlockSpec((1, tm, d), lambda b, i: (b, i, 0)),
                   pl.BlockSpec((1, tm, d), lambda b, i: (b, i, 0)),
                   pl.BlockSpec((1, ne, tm), lambda b, i: (b, 0, i))],
        out_shape=[jax.ShapeDtypeStruct((bsz, seq, d), F32),
                   jax.ShapeDtypeStruct((bsz, seq, d), BF16),
                   jax.ShapeDtypeStruct((bsz, ne, seq), F32)],
        compiler_params=_cparams(("arbitrary", "arbitrary")),
        name="outproj_router",
    )(y_hy, y_hg, x, mods3, mods3, mods3, hy_norm_g.reshape(1, wh), norm2_g.reshape(1, d), w_top, w_bot, router_wt)


def _select_kernel(a_ref, slot_ref, *, cap, rows_per_expert):
    nr = a_ref.shape[1]
    affs = [a_ref[i] for i in range(a_ref.shape[0])]
    r = lax.broadcasted_iota(jnp.int32, (nr, nr), 0)
    s = lax.broadcasted_iota(jnp.int32, (nr, nr), 1)
    same = (r // rows_per_expert) == (s // rows_per_expert)
    grp = same.astype(BF16)
    before = (same & (s < r)).astype(BF16)
    li = lax.broadcasted_iota(jnp.int32, (LANES, LANES), 0)
    lj = lax.broadcasted_iota(jnp.int32, (LANES, LANES), 1)
    ones = jnp.ones((LANES, LANES), BF16)
    strict = (li < lj).astype(BF16)

    def row_total(ind):
        return _dot(ind, ones).astype(BF16)

    def expert_total(ind):
        return _dot(grp, row_total(ind))

    def prefix(ind):
        return _dot(ind, strict) + _dot(before, row_total(ind))

    def search(i, thrs):
        bit = jnp.left_shift(jnp.int32(1), 30 - i)
        out = []
        for a, t in zip(affs, thrs):
            cand = t | bit
            cnt = expert_total(jnp.where(a >= pltpu.bitcast(cand, F32), 1.0, 0.0).astype(BF16))
            out.append(jnp.where(cnt >= cap, cand, t))
        return tuple(out)

    thrs = lax.fori_loop(0, 31, search, tuple(jnp.zeros((nr, LANES), jnp.int32) for _ in affs))
    for i, (a, thr) in enumerate(zip(affs, thrs)):
        thr_f = pltpu.bitcast(thr, F32)
        gt = a > thr_f
        eq = a == thr_f
        need = cap - expert_total(jnp.where(gt, 1.0, 0.0).astype(BF16))
        eq_rank = prefix(jnp.where(eq, 1.0, 0.0).astype(BF16))
        sel = gt | (eq & (eq_rank < need))
        rank = prefix(jnp.where(sel, 1.0, 0.0).astype(BF16))
        slot_ref[i] = jnp.where(sel, rank, -1.0)


def _select(aff_t, cap, bt):
    bsz, ne, n = aff_t.shape
    rpe = n // LANES
    a2 = aff_t.reshape(bsz, ne * rpe, LANES)
    slot = pl.pallas_call(
        functools.partial(_select_kernel, cap=cap, rows_per_expert=rpe),
        grid=(bsz // bt,),
        in_specs=[pl.BlockSpec((bt, ne * rpe, LANES), lambda b: (b, 0, 0))],
        out_specs=pl.BlockSpec((bt, ne * rpe, LANES), lambda b: (b, 0, 0)),
        out_shape=jax.ShapeDtypeStruct((bsz, ne * rpe, LANES), F32),
        compiler_params=_cparams(("arbitrary",)),
        name="ec_select",
    )(a2)
    return slot.reshape(bsz, ne, n)


def _gather_kernel(slot_ref, h_ref, o_ref, *, cap):
    slot = slot_ref[0, 0]
    j = lax.broadcasted_iota(jnp.int32, (cap, slot.shape[1]), 0).astype(F32)
    onehot = jnp.where(slot == j, 1.0, 0.0).astype(BF16)
    o_ref[0, 0] = _dot(onehot, h_ref[0]).astype(BF16)


def _gather(slot4, h2, cap):
    bsz, ne, _, n = slot4.shape
    d = h2.shape[2]
    return pl.pallas_call(
        functools.partial(_gather_kernel, cap=cap),
        grid=(bsz, ne),
        in_specs=[pl.BlockSpec((1, 1, 1, n), lambda b, e: (b, e, 0, 0)),
                  pl.BlockSpec((1, n, d), lambda b, e: (b, 0, 0))],
        out_specs=pl.BlockSpec((1, 1, cap, d), lambda b, e: (b, e, 0, 0)),
        out_shape=jax.ShapeDtypeStruct((bsz, ne, cap, d), BF16),
        compiler_params=_cparams(("arbitrary", "arbitrary")),
        name="ec_gather",
    )(slot4, h2)


def _expert_kernel(x_ref, w1_ref, w3_ref, w2_ref, o_ref, acc_ref):
    f = pl.program_id(2)
    bt, _, cap, d = x_ref.shape
    x = x_ref[...].reshape(bt * cap, d)

    @pl.when(f == 0)
    def _():
        acc_ref[...] = jnp.zeros_like(acc_ref)

    a = _dot(x, w1_ref[0].astype(BF16))
    b = _dot(x, w3_ref[0].astype(BF16))
    acc_ref[...] += _dot((_silu(a) * b).astype(BF16), w2_ref[0].astype(BF16))

    @pl.when(f == pl.num_programs(2) - 1)
    def _():
        o_ref[...] = acc_ref[...].astype(BF16).reshape(bt, 1, cap, d)


def _experts(xg, w1, w3, w2, bt, ft):
    bsz, ne, cap, d = xg.shape
    ff = w1.shape[2]
    return pl.pallas_call(
        _expert_kernel,
        grid=(ne, bsz // bt, ff // ft),
        in_specs=[pl.BlockSpec((bt, 1, cap, d), lambda e, m, f: (m, e, 0, 0)),
                  pl.BlockSpec((1, d, ft), lambda e, m, f: (e, 0, f)),
                  pl.BlockSpec((1, d, ft), lambda e, m, f: (e, 0, f)),
                  pl.BlockSpec((1, ft, d), lambda e, m, f: (e, f, 0))],
        out_specs=pl.BlockSpec((bt, 1, cap, d), lambda e, m, f: (m, e, 0, 0)),
        out_shape=jax.ShapeDtypeStruct((bsz, ne, cap, d), BF16),
        scratch_shapes=[pltpu.VMEM((bt * cap, d), F32)],
        compiler_params=_cparams(("arbitrary", "arbitrary", "arbitrary")),
        name="ec_experts",
    )(xg, w1, w3, w2)


def _combine_kernel(y_ref, slot_ref, aff_ref, x1_ref, g2_ref, fg_ref, o_ref, *, cap):
    _, ne, _, d = y_ref.shape
    slot = slot_ref[0]
    aff = aff_ref[0]
    j = lax.broadcasted_iota(jnp.int32, (slot.shape[0], cap), 1).astype(F32)
    gated = jnp.concatenate([jnp.where(slot[:, e:e + 1] == j, aff[:, e:e + 1], 0.0).astype(BF16)
                             for e in range(ne)], axis=1)
    moe = _dot(gated, y_ref[0].reshape(ne * cap, d))
    x2 = x1_ref[0] + g2_ref[0] * moe
    o_ref[0] = x2 * lax.rsqrt(jnp.mean(x2 * x2, axis=-1, keepdims=True) + EPS) * fg_ref[...]


def _combine(y, slot_nt, aff_nt, x1, mods3, final_g, cap, tn):
    bsz, ne, _, d = y.shape
    n = x1.shape[1]
    return pl.pallas_call(
        functools.partial(_combine_kernel, cap=cap),
        grid=(bsz, n // tn),
        in_specs=[pl.BlockSpec((1, ne, cap, d), lambda b, t: (b, 0, 0, 0)),
                  pl.BlockSpec((1, tn, ne), lambda b, t: (b, t, 0)),
                  pl.BlockSpec((1, tn, ne), lambda b, t: (b, t, 0)),
                  pl.BlockSpec((1, tn, d), lambda b, t: (b, t, 0)),
                  pl.BlockSpec((1, 1, d), lambda b, t: (b, 0, 5)),
                  pl.BlockSpec((1, d), lambda b, t: (0, 0))],
        out_specs=pl.BlockSpec((1, tn, d), lambda b, t: (b, t, 0)),
        out_shape=jax.ShapeDtypeStruct((bsz, n, d), F32),
        compiler_params=_cparams(("arbitrary", "arbitrary")),
        name="ec_combine",
    )(y, slot_nt, aff_nt, x1, mods3, final_g.reshape(1, d))


def kernel(x, c, ctx, c_ctx, ada_w, ada_b, norm1_g, w_in, hy_conv_w, hy_conv_b, hy_pos_w1, hy_pos_b1, hy_freq1,
           hy_pos_w2, hy_pos_b2, hy_freq2, hy_pos_w3, hy_bias, hy_norm_g, hg_lb, hg_norm_g, w_out, norm2_g,
           router_w, exp_w1, exp_w3, exp_w2, final_g):
    bsz, seq, d = x.shape
    hyw = hy_norm_g.shape[1]
    hgw = w_in.shape[2] - 3 * hyw
    hg = hgw // 5
    cap = CAPACITY_FACTOR * seq // N_EXPERTS

    rows = -(-(bsz + 1) // 8) * 8
    s_rows = jnp.zeros((rows, d), F32).at[:bsz].set(c).at[bsz].set(c_ctx)
    mods3 = _ada(s_rows, ada_w[0], ada_b[0]).reshape(rows, 1, 6 * d)

    w_bf = w_in[0].astype(BF16)
    tn = 1024
    p = _inproj(x, mods3, lambda b: b, norm1_g[0], w_bf, 0, w_in.shape[2] // tn, seq, tn)
    w_ctx = w_bf[:, 3 * hyw + hg:3 * hyw + 4 * hg]
    pc = _inproj(ctx, mods3, lambda b: bsz, norm1_g[0], w_ctx, 0, 1, ctx.shape[1], 3 * hg)

    dft_c, dft_s = _dft_tables(seq)
    ct = LANES
    ptab, hm = _hyena_filters(seq, hy_pos_w1[0], hy_pos_b1[0], hy_freq1[0], hy_pos_w2[0], hy_pos_b2[0],
                              hy_freq2[0], hy_pos_w3[0], dft_c, dft_s, 2 * LANES)
    z = _hyena_conv(p, 0, p, 1, hy_conv_w[0], hy_conv_b[0], dft_c, dft_s, ptab, hm, hy_bias[0], 0, True, ct)
    y_hy = _hyena_conv(z, 0, p, 2, hy_conv_w[0], hy_conv_b[0], dft_c, dft_s, ptab, hm, hy_bias[0], 1, False, ct)

    y_hg = _hgrn(p, 3 * hyw // (hg // HG_HEADS), pc, hg_lb, hg_norm_g[0], 2)

    w_o = w_out[0].astype(BF16)
    x1, h2, aff_t = _outproj(y_hy, y_hg, x, mods3, hy_norm_g[0], norm2_g[0], w_o[:hyw], w_o[hyw:],
                             router_w[0].T, 1024)

    slot = _select(aff_t, cap, 8)
    xg = _gather(slot.reshape(bsz, N_EXPERTS, 1, seq), h2, cap)
    y = _experts(xg, exp_w1[0], exp_w3[0], exp_w2[0], 8, 512)
    return _combine(y, jnp.swapaxes(slot, 1, 2), jnp.swapaxes(aff_t, 1, 2), x1, mods3, final_g, cap, 512)
```

```python
import functools
import math

import numpy as np
import jax
import jax.numpy as jnp
from jax import lax
from jax.experimental import pallas as pl
from jax.experimental.pallas import tpu as pltpu

F32 = jnp.float32
BF16 = jnp.bfloat16
HIGHEST = lax.Precision.HIGHEST

GRID_W = 64
HG_HEADS = 4
HY_SHORT = 3
HY_BANDS = 16
HY_DECAY_TARGET = 1e-2
HY_FAST_DECAY = 0.3
HY_SLOW_DECAY = 1.5
HY_WINDOW_SHIFT = 0.05
N_EXPERTS = 16
CAPACITY_FACTOR = 2
EPS = 1e-6

SCAN_CHUNK = 64
SCAN_BLOCK = 16
FAST_BLOCK = 8
MAX_FAST_EXPONENT = 80.0
EXPERT_ROW_TILES = 2
LANES = 128
VMEM_LIMIT = 56 * 1024 * 1024


def _cparams(sem):
    return pltpu.CompilerParams(dimension_semantics=sem, vmem_limit_bytes=VMEM_LIMIT)


def _resident(shape, grid_rank):
    zeros = (0,) * len(shape)
    return pl.BlockSpec(shape, {1: lambda a: zeros, 2: lambda a, b: zeros, 3: lambda a, b, c: zeros}[grid_rank],
                        pipeline_mode=pl.Buffered(1))


def _dot(a, b):
    return jnp.dot(a, b, preferred_element_type=F32)


def _dot_nt(a, b):
    return lax.dot_general(a, b, (((1,), (1,)), ((), ())), preferred_element_type=F32)


def _dot_tn(a, b):
    return lax.dot_general(a, b, (((0,), (0,)), ((), ())), preferred_element_type=F32)


def _silu(x):
    return x * jax.nn.sigmoid(x)


def _ada_kernel(s_ref, w_ref, b_ref, o_ref):
    s = _silu(s_ref[...])
    o_ref[...] = jnp.dot(s, w_ref[...], preferred_element_type=F32, precision=HIGHEST) + b_ref[...]


def _ada(s_rows, ada_w, ada_b):
    rows, d = s_rows.shape
    n = ada_w.shape[1]
    tn = 1024
    return pl.pallas_call(
        _ada_kernel,
        grid=(n // tn,),
        in_specs=[pl.BlockSpec((rows, d), lambda j: (0, 0)),
                  pl.BlockSpec((d, tn), lambda j: (0, j)),
                  pl.BlockSpec((1, tn), lambda j: (0, j))],
        out_specs=pl.BlockSpec((rows, tn), lambda j: (0, j)),
        out_shape=jax.ShapeDtypeStruct((rows, n), F32),
        compiler_params=_cparams(("arbitrary",)),
        name="ada",
    )(s_rows, ada_w, ada_b.reshape(1, n))


def _inproj_kernel(x_ref, sh_ref, sc_ref, g_ref, w_ref, o_ref, h_scr):
    @pl.when(pl.program_id(2) == 0)
    def _():
        x = x_ref[0]
        y = x * lax.rsqrt(jnp.mean(x * x, axis=-1, keepdims=True) + EPS) * g_ref[...]
        h_scr[...] = (y * (1.0 + sc_ref[0]) + sh_ref[0]).astype(BF16)

    o_ref[0] = _dot(h_scr[...], w_ref[...])


def _inproj(x, mods3, mod_row, norm_g, w_bf, col_blk0, n_col_blk, tm, tn):
    bsz, seq, d = x.shape
    return pl.pallas_call(
        _inproj_kernel,
        grid=(bsz, seq // tm, n_col_blk),
        in_specs=[pl.BlockSpec((1, tm, d), lambda b, i, j: (b, i, 0)),
                  pl.BlockSpec((1, 1, d), lambda b, i, j: (mod_row(b), 0, 0)),
                  pl.BlockSpec((1, 1, d), lambda b, i, j: (mod_row(b), 0, 1)),
                  pl.BlockSpec((1, d), lambda b, i, j: (0, 0)),
                  pl.BlockSpec((d, tn), lambda b, i, j: (0, col_blk0 + j))],
        out_specs=pl.BlockSpec((1, tm, tn), lambda b, i, j: (b, i, j)),
        out_shape=jax.ShapeDtypeStruct((bsz, seq, n_col_blk * tn), F32),
        scratch_shapes=[pltpu.VMEM((tm, d), BF16)],
        compiler_params=_cparams(("arbitrary", "arbitrary", "arbitrary")),
        name="inproj",
    )(x, mods3, mods3, norm_g.reshape(1, d), w_bf)


def _dft_tables(seq):
    half = seq // 2
    k = np.arange(half, dtype=np.int64)
    idx = (k[:, None] * k[None, :]) % seq
    ang = 2.0 * np.pi * np.arange(seq, dtype=np.float64) / seq
    cos_t = jnp.asarray(np.cos(ang)[idx].astype(np.float32))
    sin_t = jnp.asarray(np.sin(ang)[idx].astype(np.float32))
    return cos_t.astype(BF16), sin_t.astype(BF16)


def _pos_features(seq):
    t = np.linspace(0.0, 1.0, seq, dtype=np.float32)[:, None]
    w = (2.0 * math.pi * np.arange(seq, dtype=np.float32)[:, None] / seq).astype(np.float32)
    f = np.linspace(1e-4, HY_BANDS - 1, HY_BANDS, dtype=np.float32)[None, :]
    fw = (f * w).astype(np.float32)
    return np.concatenate([t, np.cos(fw), -np.sin(fw)], axis=-1).astype(np.float32)


def _filter_kernel(z_ref, w1_ref, b1_ref, f1_ref, w2_ref, b2_ref, f2_ref, w3f_ref, w3b_ref, t_ref, dl_ref,
                   sgn_ref, wk_ref, c2_ref, s2_ref, c_ref, s_ref, p_ref, hm_ref):
    half = c_ref.shape[0]
    n = 4 * half
    taps = {}
    for par in range(2):
        h = jnp.sin(f1_ref[...] * (jnp.dot(z_ref[par], w1_ref[...], preferred_element_type=F32, precision=HIGHEST)
                                   + b1_ref[...]))
        h = jnp.sin(f2_ref[...] * (jnp.dot(h, w2_ref[...], preferred_element_type=F32, precision=HIGHEST)
                                   + b2_ref[...]))
        window = jnp.exp(-t_ref[par] * dl_ref[...]) + HY_WINDOW_SHIFT
        taps["f", par] = jnp.dot(h, w3f_ref[...], preferred_element_type=F32, precision=HIGHEST) * window
        taps["b", par] = jnp.dot(h, w3b_ref[...], preferred_element_type=F32, precision=HIGHEST) * window
    row = lax.broadcasted_iota(jnp.int32, taps["b", 0].shape, 0)
    taps["b", 0] = jnp.where(row == 0, 0.0, taps["b", 0])
    inv = 1.0 / (sum(jnp.sum(jnp.abs(x), axis=0, keepdims=True) for x in taps.values()) + EPS)
    sgn = sgn_ref[...]
    spec = {}
    for key, x in taps.items():
        x = x * inv
        xb = x.astype(BF16)
        spec[key] = (_dot(c_ref[...], xb), _dot(s_ref[...], xb), jnp.sum(x * sgn, axis=0, keepdims=True))
    (fec, fes, fen), (foc, fos, fon) = spec["f", 0], spec["f", 1]
    (bec, bes, ben), (boc, bos, bon) = spec["b", 0], spec["b", 1]
    c2, s2 = c2_ref[...], s2_ref[...]
    wk = 2.0 * wk_ref[...]
    p_ref[0, 0] = (fec + bec) * wk
    p_ref[0, 1] = (bes - fes) * wk
    p_ref[0, 2] = (c2 * foc - s2 * fos + boc) * wk
    p_ref[0, 3] = (bos - c2 * fos - s2 * foc) * wk
    p_ref[0, 4] = (foc + c2 * boc - s2 * bos) * wk
    p_ref[0, 5] = (c2 * bos + s2 * boc - fos) * wk
    hm_ref[0] = jnp.concatenate([fen + ben, bon - fon], axis=0) * (2.0 / n)


def _hyena_filters(seq, pos_w1, pos_b1, freq1, pos_w2, pos_b2, freq2, pos_w3, dft_c, dft_s, ct):
    orders = 2
    half = seq // 2
    ch = pos_w3.shape[1] // (orders * 2)
    zt = _pos_features(seq)
    pdim = -(-zt.shape[1] // LANES) * LANES
    hid = -(-pos_w1.shape[1] // LANES) * LANES
    zt = np.pad(zt, ((0, 0), (0, pdim - zt.shape[1])))
    zt = np.stack([zt[0::2], zt[1::2]])
    pad2 = lambda a, r, c_: jnp.pad(a, ((0, r - a.shape[0]), (0, c_ - a.shape[1])))
    pos_w1 = pad2(pos_w1, pdim, hid)
    pos_w2 = pad2(pos_w2, hid, hid)
    pos_w3 = pad2(pos_w3, hid, pos_w3.shape[1])
    pos_b1, freq1, pos_b2, freq2 = (pad2(a.reshape(1, -1), 1, hid) for a in (pos_b1, freq1, pos_b2, freq2))
    t_col = np.linspace(0.0, 1.0, seq, dtype=np.float32)[:, None]
    t_col = np.stack([t_col[0::2], t_col[1::2]])
    min_decay = math.log(HY_DECAY_TARGET) / HY_SLOW_DECAY
    max_decay = math.log(HY_DECAY_TARGET) / HY_FAST_DECAY
    deltas = np.abs(np.linspace(min_decay, max_decay, ch, dtype=np.float32))[None, :]
    sgn = np.where(np.arange(half) % 2 == 0, 1.0, -1.0).astype(np.float32)[:, None]
    wk = np.full((half, 1), 2.0 / (2 * seq), np.float32)
    wk[0, 0] = 1.0 / (2 * seq)
    ang = 2.0 * np.pi * np.arange(half, dtype=np.float64)[:, None] / seq
    ncb = ch // ct
    const = lambda shape: pl.BlockSpec(shape, lambda o, c: (0,) * len(shape))
    return pl.pallas_call(
        _filter_kernel,
        grid=(orders, ncb),
        in_specs=[const((2, half, pdim)), const((pdim, hid)), const((1, hid)), const((1, hid)),
                  const((hid, hid)), const((1, hid)), const((1, hid)),
                  pl.BlockSpec((hid, ct), lambda o, c: (0, (2 * o) * ncb + c)),
                  pl.BlockSpec((hid, ct), lambda o, c: (0, (2 * o + 1) * ncb + c)),
                  const((2, half, 1)),
                  pl.BlockSpec((1, ct), lambda o, c: (0, c)),
                  const((half, 1)), const((half, 1)), const((half, 1)), const((half, 1)),
                  const((half, half)), const((half, half))],
        out_specs=[pl.BlockSpec((1, 6, half, ct), lambda o, c: (o, 0, 0, c)),
                   pl.BlockSpec((1, 2, ct), lambda o, c: (o, 0, c))],
        out_shape=[jax.ShapeDtypeStruct((orders, 6, half, ch), F32),
                   jax.ShapeDtypeStruct((orders, 2, ch), F32)],
        compiler_params=_cparams(("arbitrary", "arbitrary")),
        name="hyena_filter",
    )(jnp.asarray(zt), pos_w1, pos_b1, freq1, pos_w2, pos_b2, freq2, pos_w3, pos_w3, jnp.asarray(t_col),
      jnp.asarray(deltas), jnp.asarray(sgn), jnp.asarray(wk), jnp.asarray(np.cos(ang).astype(np.float32)),
      jnp.asarray(np.sin(ang).astype(np.float32)), dft_c, dft_s)


def _split_rows(x_ref):
    half = x_ref.shape[1] // 2
    return x_ref[0, pl.ds(0, half, stride=2), :], x_ref[0, pl.ds(1, half, stride=2), :]


def _short_conv_split(p_ref, w_ref, b_ref):
    pe, po = _split_rows(p_ref)
    half = pe.shape[0]
    pos = lax.broadcasted_iota(jnp.int32, pe.shape, 0) % (GRID_W // 2)
    before_even = jnp.where(pos == 0, 0.0, pltpu.roll(po, 1, axis=0))
    after_odd = jnp.where(pos == GRID_W // 2 - 1, 0.0, pltpu.roll(pe, half - 1, axis=0))
    w0, w1, w2 = w_ref[0:1, :], w_ref[1:2, :], w_ref[2:3, :]
    return (b_ref[...] + before_even * w0 + pe * w1 + po * w2,
            b_ref[...] + pe * w0 + po * w1 + after_odd * w2)


def _hyena_kernel(u_ref, g_ref, cwu_ref, cbu_ref, cwg_ref, cbg_ref, fwd_ref, inv_ref, p_ref, hm_ref,
                  bias_ref, sgn_ref, o_ref, spec_ref, *, conv_u, freq_tile):
    half = inv_ref.shape[0]
    ct = o_ref.shape[2]
    ue, uo = _short_conv_split(u_ref, cwu_ref, cbu_ref) if conv_u else _split_rows(u_ref)
    ub = jnp.concatenate([ue, uo], axis=1).astype(BF16)
    for ft in range(half // freq_tile):
        lo, hi = ft * freq_tile, (ft + 1) * freq_tile
        dc = _dot(fwd_ref[lo:hi, :], ub)
        ds = _dot(fwd_ref[half + lo:half + hi, :], ub)
        er, orr, es, os_ = dc[:, :ct], dc[:, ct:], ds[:, :ct], ds[:, ct:]
        p1r, p1i, p2r, p2i, p3r, p3i = (p_ref[0, i, lo:hi, :] for i in range(6))
        ar = er * p1r + es * p1i + orr * p2r + os_ * p2i
        br = er * p3r + es * p3i + orr * p1r + os_ * p1i
        ai = es * p1r - er * p1i + os_ * p2r - orr * p2i
        bi = es * p3r - er * p3i + os_ * p1r - orr * p1i
        spec_ref[lo:hi, :] = jnp.concatenate([ar, br], axis=1).astype(BF16)
        spec_ref[half + lo:half + hi, :] = jnp.concatenate([ai, bi], axis=1).astype(BF16)
    y = _dot(inv_ref[...], spec_ref[...])
    sgn = sgn_ref[...]
    en = jnp.sum(ue * sgn, axis=0, keepdims=True)
    on = jnp.sum(uo * sgn, axis=0, keepdims=True)
    hm_r, hm_i = hm_ref[0, 0:1, :], hm_ref[0, 1:2, :]
    bias = bias_ref[0]
    ge, go = _short_conv_split(g_ref, cwg_ref, cbg_ref)
    o_ref[0, pl.ds(0, half, stride=2), :] = ge * (y[:, :ct] + ue * bias + sgn * (en * hm_r + on * hm_i))
    o_ref[0, pl.ds(1, half, stride=2), :] = go * (y[:, ct:] + uo * bias + sgn * (on * hm_r - en * hm_i))


def _hyena_conv(u, u_col_blk, p_hy, gate_col_blk, conv_w, conv_b, dft_c, dft_s, ptab, hm, bias, order,
                conv_u, ct):
    bsz, seq, _ = p_hy.shape
    half = seq // 2
    ch = ptab.shape[3]
    ncb = ch // ct
    sgn = jnp.asarray(np.where(np.arange(half) % 2 == 0, 1.0, -1.0).astype(np.float32)[:, None])
    taps = conv_w.shape[0]
    cbias = conv_b.reshape(1, -1)
    const = lambda shape: pl.BlockSpec(shape, lambda c, b: (0,) * len(shape))
    return pl.pallas_call(
        functools.partial(_hyena_kernel, conv_u=conv_u, freq_tile=half // 4),
        grid=(ncb, bsz),
        in_specs=[pl.BlockSpec((1, seq, ct), lambda c, b: (b, 0, u_col_blk * ncb + c)),
                  pl.BlockSpec((1, seq, ct), lambda c, b: (b, 0, gate_col_blk * ncb + c)),
                  pl.BlockSpec((taps, ct), lambda c, b: (0, u_col_blk * ncb + c)),
                  pl.BlockSpec((1, ct), lambda c, b: (0, u_col_blk * ncb + c)),
                  pl.BlockSpec((taps, ct), lambda c, b: (0, gate_col_blk * ncb + c)),
                  pl.BlockSpec((1, ct), lambda c, b: (0, gate_col_blk * ncb + c)),
                  const((2 * half, half)), const((half, 2 * half)),
                  pl.BlockSpec((1, 6, half, ct), lambda c, b: (order, 0, 0, c)),
                  pl.BlockSpec((1, 2, ct), lambda c, b: (order, 0, c)),
                  pl.BlockSpec((1, 1, ct), lambda c, b: (order, 0, c)),
                  const((half, 1))],
        out_specs=pl.BlockSpec((1, seq, ct), lambda c, b: (b, 0, c)),
        out_shape=jax.ShapeDtypeStruct((bsz, seq, ch), F32),
        scratch_shapes=[pltpu.VMEM((2 * half, 2 * ct), BF16)],
        compiler_params=_cparams(("arbitrary", "arbitrary")),
        name=f"hyena_conv{order}",
    )(u, p_hy, conv_w, cbias, conv_w, cbias, jnp.concatenate([dft_c, dft_s], axis=0),
      jnp.concatenate([dft_c, dft_s], axis=1), ptab, hm, bias.reshape(bias.shape[0], 1, ch), sgn)


def _forget_gate(z, lb):
    f = lb + (1.0 - lb) * jax.nn.sigmoid(z)
    return 1.0 - f, jnp.log(f)


def _scan_consts():
    c, blk = SCAN_CHUNK, SCAN_BLOCK
    r = lax.broadcasted_iota(jnp.int32, (c, c), 0)
    s = lax.broadcasted_iota(jnp.int32, (c, c), 1)
    row = lax.broadcasted_iota(jnp.int32, (c, LANES), 0)
    fb = FAST_BLOCK

    def pair_mask(size, rev):
        same = (r // (2 * size)) == (s // (2 * size))
        hi_r, hi_s = r % (2 * size) >= size, s % (2 * size) >= size
        return same & ((~hi_r & hi_s) if rev else (hi_r & ~hi_s))

    return dict(
        cum_f=(s <= r).astype(F32), cum_b=(s >= r).astype(F32),
        m32_f=(r >= c // 2) & (s < c // 2), m32_b=(r < c // 2) & (s >= c // 2),
        m16_f=pair_mask(blk, False), m16_b=pair_mask(blk, True),
        m8_f=pair_mask(fb, False), m8_b=pair_mask(fb, True),
        md_f=(r // fb == s // fb) & (s <= r), md_b=(r // fb == s // fb) & (s >= r),
        sub=lax.broadcasted_iota(jnp.int32, (fb, LANES), 0),
        row=row, ones=jnp.ones((LANES, LANES), BF16))


def _state_update(st_ref, k, v, g, g_edge):
    kd = (k * jnp.exp(g_edge - g)).astype(BF16)
    st_ref[...] = st_ref[...] * jnp.exp(g_edge) + _dot_tn(v.astype(BF16), kd)


def _chunk_cumsum(logf, cst, reverse):
    return jnp.dot(cst["cum_b" if reverse else "cum_f"], logf, preferred_element_type=F32, precision=HIGHEST)


def _scan_chunk(st_ref, q, k, v, logf, cst, reverse):
    c, blk = SCAN_CHUNK, SCAN_BLOCK
    row = cst["row"]
    g = _chunk_cumsum(logf, cst, reverse)
    o = _dot_nt((q * jnp.exp(g)).astype(BF16), st_ref[...].astype(BF16))
    half = c // 2
    if not reverse:
        ref32 = g[half - 1:half, :]
        ref16 = jnp.where(row < half, g[blk - 1:blk, :], g[half + blk - 1:half + blk, :])
        t32, s32 = row >= half, row < half
        t16, s16 = row % (2 * blk) >= blk, row % (2 * blk) < blk
        g_edge = g[c - 1:c, :]
    else:
        ref32 = g[half:half + 1, :]
        ref16 = jnp.where(row < half, g[blk:blk + 1, :], g[half + blk:half + blk + 1, :])
        t32, s32 = row < half, row >= half
        t16, s16 = row % (2 * blk) < blk, row % (2 * blk) >= blk
        g_edge = g[0:1, :]
    a32 = jnp.where(t32, q * jnp.exp(jnp.minimum(g - ref32, 0.0)), 0.0).astype(BF16)
    b32 = jnp.where(s32, k * jnp.exp(jnp.minimum(ref32 - g, 0.0)), 0.0).astype(BF16)
    a16 = jnp.where(t16, q * jnp.exp(jnp.minimum(g - ref16, 0.0)), 0.0).astype(BF16)
    b16 = jnp.where(s16, k * jnp.exp(jnp.minimum(ref16 - g, 0.0)), 0.0).astype(BF16)
    att = (jnp.where(cst["m32_b" if reverse else "m32_f"], _dot_nt(a32, b32), 0.0)
           + jnp.where(cst["m16_b" if reverse else "m16_f"], _dot_nt(a16, b16), 0.0))
    o = o + _dot(att.astype(BF16), v.astype(BF16))
    pos = row % blk
    for d in range(blk):
        shift = d if not reverse else (c - d) % c
        if d == 0:
            ks, gs, vs = k, g, v
        else:
            ks = pltpu.roll(k, shift, axis=0)
            gs = pltpu.roll(g, shift, axis=0)
            vs = pltpu.roll(v, shift, axis=0)
        valid = (pos >= d) if not reverse else (pos + d < blk)
        w = jnp.where(valid, q * ks * jnp.exp(jnp.minimum(g - gs, 0.0)), 0.0)
        o = o + _dot(w.astype(BF16), cst["ones"]) * vs
    _state_update(st_ref, k, v, g, g_edge)
    return o


def _chunk_prepare(q, k, v, logf, cst, reverse):
    c, fb = SCAN_CHUNK, FAST_BLOCK
    nb = c // fb
    d = "b" if reverse else "f"
    sub = cst["sub"]
    blocks = lambda x: [x[fb * j:fb * (j + 1), :] for j in range(nb)]
    qs, ks = blocks(q), blocks(k)
    ps = []
    for p in blocks(logf):
        for s in (1, 2, 4):
            if reverse:
                p = p + jnp.where(sub + s < fb, pltpu.roll(p, fb - s, axis=0), 0.0)
            else:
                p = p + jnp.where(sub >= s, pltpu.roll(p, s, axis=0), 0.0)
        ps.append(p)
    last = 0 if reverse else fb - 1
    tot = [jnp.broadcast_to(p[last:last + 1, :], p.shape) for p in ps]
    decay = [jnp.exp(t) for t in tot]
    memo = {}

    def span(lo, hi):
        if hi <= lo:
            return None
        if (lo, hi) not in memo:
            if hi == nb and lo > 0:
                inner, last = span(lo + 1, hi), decay[lo]
            else:
                inner, last = span(lo, hi - 1), decay[hi - 1]
            memo[lo, hi] = last if inner is None else inner * last
        return memo[lo, hi]

    scale = lambda x, f: x if f is None else x * f
    qe = [qs[j] * jnp.exp(ps[j]) for j in range(nb)]
    kb = [ks[j] * jnp.exp(tot[j] - ps[j]) for j in range(nb)]
    zero = jnp.zeros_like(tot[0])
    cat = lambda parts: jnp.concatenate(parts, axis=0).astype(BF16)

    def level(n):
        a, b = [], []
        for j in range(nb):
            mid = (j // (2 * n)) * 2 * n + n
            later = j >= mid
            if later != reverse:
                a.append(scale(qe[j], span(j + 1, mid) if reverse else span(mid, j)))
                b.append(zero)
            else:
                a.append(zero)
                b.append(scale(kb[j], span(mid, j) if reverse else span(j + 1, mid)))
        return _dot_nt(cat(a), cat(b))

    bd = cat([ks[j] * jnp.exp(-ps[j]) for j in range(nb)])
    att = (level(nb // 2)
           + jnp.where(cst["m16_" + d], level(nb // 4), 0.0)
           + jnp.where(cst["m8_" + d], level(nb // 8), 0.0)
           + jnp.where(cst["md_" + d], _dot_nt(cat(qe), bd), 0.0))
    before = lambda j: span(j + 1, nb) if reverse else span(0, j)
    after = lambda j: span(0, j) if reverse else span(j + 1, nb)
    qd = cat([scale(qe[j], before(j)) for j in range(nb)])
    kd = cat([scale(kb[j], after(j)) for j in range(nb)])
    vb = v.astype(BF16)
    return att.astype(BF16), qd, vb, _dot_tn(vb, kd), span(0, nb)[0:1, :]


def _chunk_finish(st_ref, prepared):
    att, qd, vb, kv, decay = prepared
    o = _dot_nt(qd, st_ref[...].astype(BF16)) + _dot(att, vb)
    st_ref[...] = st_ref[...] * decay + kv
    return o


def _hgrn_kernel(q_ref, i_ref, zf_ref, zb_ref, g_ref, ic_ref, zfc_ref, zbc_ref, lb_ref, ng_ref, o_ref,
                 of_ref, ob_ref, st_ref):
    seq = q_ref.shape[1]
    ctx_len = ic_ref.shape[1]
    hd = ng_ref.shape[1]
    heads = q_ref.shape[2] // hd
    c = SCAN_CHUNK
    n_chunks, n_ctx = seq // c, ctx_len // c
    cst = _scan_consts()
    lbr = lb_ref[...]
    e = jnp.exp(lbr - jnp.max(lbr, axis=0, keepdims=True))
    lb = e[0] / jnp.sum(e, axis=0)
    qscale = hd ** -0.5
    lanes = [slice(h * hd, (h + 1) * hd) for h in range(heads)]
    st_f = [st_ref.at[2 * h] for h in range(heads)]
    st_b = [st_ref.at[2 * h + 1] for h in range(heads)]

    st_ref[...] = jnp.zeros_like(st_ref)

    def ctx_body(j, carry):
        rf = pl.multiple_of(j * c, c)
        rb = pl.multiple_of((n_ctx - 1 - j) * c, c)
        for h, ln in enumerate(lanes):
            for st, z_ref, r, lb_row, rev in ((st_f[h], zfc_ref, rf, lb[0:1, ln], False),
                                              (st_b[h], zbc_ref, rb, lb[1:2, ln], True)):
                k, logf = _forget_gate(z_ref[0, pl.ds(r, c), ln], lb_row)
                *_, kv, decay = _chunk_prepare(jnp.zeros_like(k), k, ic_ref[0, pl.ds(r, c), ln], logf, cst, rev)
                st[...] = st[...] * decay + kv
        return carry

    lax.fori_loop(0, n_ctx, ctx_body, 0)

    def rows(j):
        if isinstance(j, int):
            return j * c, (n_chunks - 1 - j) * c
        return pl.multiple_of(j * c, c), pl.multiple_of((n_chunks - 1 - j) * c, c)

    def operands(r, ln, z_ref, lb_row):
        k, logf = _forget_gate(z_ref[0, pl.ds(r, c), ln], lb_row)
        return _silu(q_ref[0, pl.ds(r, c), ln]) * qscale, k, i_ref[0, pl.ds(r, c), ln], logf

    def direct_body(j, carry):
        rf, rb = rows(j)
        for h, ln in enumerate(lanes):
            of_ref[pl.ds(rf, c), ln] = _scan_chunk(st_f[h], *operands(rf, ln, zf_ref, lb[0:1, ln]), cst, False)
            ob_ref[pl.ds(rb, c), ln] = _scan_chunk(st_b[h], *operands(rb, ln, zb_ref, lb[1:2, ln]), cst, True)
        return carry

    def prepare(j):
        rf, rb = rows(j)
        out = []
        for ln in lanes:
            out.append(_chunk_prepare(*operands(rf, ln, zf_ref, lb[0:1, ln]), cst, False))
            out.append(_chunk_prepare(*operands(rb, ln, zb_ref, lb[1:2, ln]), cst, True))
        return tuple(out)

    def finish(j, prepared):
        rf, rb = rows(j)
        for h, ln in enumerate(lanes):
            of_ref[pl.ds(rf, c), ln] = _chunk_finish(st_f[h], prepared[2 * h])
            ob_ref[pl.ds(rb, c), ln] = _chunk_finish(st_b[h], prepared[2 * h + 1])

    def factored_body(j, prepared):
        upcoming = prepare(j + 1)
        finish(j, prepared)
        return upcoming

    bounded = jnp.min(lb) >= math.exp(-MAX_FAST_EXPONENT / FAST_BLOCK)

    @pl.when(bounded)
    def _():
        finish(n_chunks - 1, lax.fori_loop(0, n_chunks - 1, factored_body, prepare(0), unroll=2))

    @pl.when(jnp.logical_not(bounded))
    def _():
        lax.fori_loop(0, n_chunks, direct_body, 0)

    for ln in lanes:
        o = of_ref[:, ln] + ob_ref[:, ln]
        y = o * lax.rsqrt(jnp.mean(o * o, axis=-1, keepdims=True) + EPS) * ng_ref[...]
        o_ref[0, :, ln] = y * _silu(g_ref[0, :, ln])


def _hgrn(p, col0, pc, hg_lb, norm_g, heads_per_step):
    bsz, seq, _ = p.shape
    ctx_len = pc.shape[1]
    hd = norm_g.shape[0]
    w = heads_per_step * hd
    steps = HG_HEADS // heads_per_step
    lat = lambda grp: pl.BlockSpec((1, seq, w), lambda b, h: (b, 0, (col0 + grp * HG_HEADS) // heads_per_step + h))
    ctx = lambda grp: pl.BlockSpec((1, ctx_len, w), lambda b, h: (b, 0, grp * steps + h))
    return pl.pallas_call(
        _hgrn_kernel,
        grid=(bsz, steps),
        in_specs=[lat(0), lat(1), lat(2), lat(3), lat(4), ctx(0), ctx(1), ctx(2),
                  pl.BlockSpec((hg_lb.shape[0], 2, w), lambda b, h: (0, 0, h)),
                  pl.BlockSpec((1, hd), lambda b, h: (0, 0))],
        out_specs=pl.BlockSpec((1, seq, w), lambda b, h: (b, 0, h)),
        out_shape=jax.ShapeDtypeStruct((bsz, seq, HG_HEADS * hd), F32),
        scratch_shapes=[pltpu.VMEM((seq, w), F32), pltpu.VMEM((seq, w), F32),
                        pltpu.VMEM((2 * heads_per_step, hd, hd), F32)],
        compiler_params=_cparams(("arbitrary", "arbitrary")),
        name="hgrn_scan",
    )(p, p, p, p, p, pc, pc, pc, hg_lb, norm_g.reshape(1, hd))


def _outproj_kernel(yhy_ref, yhg_ref, x_ref, g1_ref, sh2_ref, sc2_ref, hyg_ref, n2g_ref, wt_ref, wb_ref, rw_ref,
                    x1_ref, h2_ref, aff_ref):
    yhy = yhy_ref[0]
    yn = yhy * lax.rsqrt(jnp.mean(yhy * yhy, axis=-1, keepdims=True) + EPS) * hyg_ref[...]
    mix = _dot(yn.astype(BF16), wt_ref[...]) + _dot(yhg_ref[0].astype(BF16), wb_ref[...])
    x1 = x_ref[0] + g1_ref[0] * mix
    x1_ref[0] = x1
    h2 = (x1 * lax.rsqrt(jnp.mean(x1 * x1, axis=-1, keepdims=True) + EPS) * n2g_ref[...]
          * (1.0 + sc2_ref[0]) + sh2_ref[0])
    h_hi = h2.astype(BF16)
    h2_ref[0] = h_hi
    h_lo = (h2 - h_hi.astype(F32)).astype(BF16)
    rw = rw_ref[...]
    r_hi = rw.astype(BF16)
    r_lo = (rw - r_hi.astype(F32)).astype(BF16)
    ne = rw.shape[0]
    both = _dot_nt(jnp.concatenate([r_hi, r_lo], axis=0), h_hi)
    logits = both[:ne] + both[ne:] + _dot_nt(r_hi, h_lo)
    ex = jnp.exp(logits - jnp.max(logits, axis=0, keepdims=True))
    aff_ref[0] = ex / jnp.sum(ex, axis=0, keepdims=True)


def _outproj(y_hy, y_hg, x, mods3, hy_norm_g, norm2_g, w_top, w_bot, router_wt, tm):
    bsz, seq, d = x.shape
    wh = y_hy.shape[2]
    ne = router_wt.shape[0]
    mod = lambda k: pl.BlockSpec((1, 1, d), lambda b, i: (b, 0, k))
    const = lambda shape: pl.BlockSpec(shape, lambda b, i: (0,) * len(shape))
    return pl.pallas_call(
        _outproj_kernel,
        grid=(bsz, seq // tm),
        in_specs=[pl.BlockSpec((1, tm, wh), lambda b, i: (b, i, 0)),
                  pl.BlockSpec((1, tm, wh), lambda b, i: (b, i, 0)),
                  pl.BlockSpec((1, tm, d), lambda b, i: (b, i, 0)),
                  mod(2), mod(3), mod(4),
                  const((1, wh)), const((1, d)), const((wh, d)), const((wh, d)), const((ne, d))],
        out_specs=[pl.BlockSpec((1, tm, d), lambda b, i: (b, i, 0)),
                   pl.BlockSpec((1, tm, d), lambda b, i: (b, i, 0)),
                   pl.BlockSpec((1, ne, tm), lambda b, i: (b, 0, i))],
        out_shape=[jax.ShapeDtypeStruct((bsz, seq, d), F32),
                   jax.ShapeDtypeStruct((bsz, seq, d), BF16),
                   jax.ShapeDtypeStruct((bsz, ne, seq), F32)],
        compiler_params=_cparams(("arbitrary", "arbitrary")),
        name="outproj_router",
    )(y_hy, y_hg, x, mods3, mods3, mods3, hy_norm_g.reshape(1, wh), norm2_g.reshape(1, d), w_top, w_bot, router_wt)


def _select_kernel(a_ref, slot_ref, *, cap, rows_per_expert):
    nr = a_ref.shape[1]
    affs = [a_ref[i] for i in range(a_ref.shape[0])]
    r = lax.broadcasted_iota(jnp.int32, (nr, nr), 0)
    s = lax.broadcasted_iota(jnp.int32, (nr, nr), 1)
    same = (r // rows_per_expert) == (s // rows_per_expert)
    grp = same.astype(BF16)
    before = (same & (s < r)).astype(BF16)
    li = lax.broadcasted_iota(jnp.int32, (LANES, LANES), 0)
    lj = lax.broadcasted_iota(jnp.int32, (LANES, LANES), 1)
    ones = jnp.ones((LANES, LANES), BF16)
    strict = (li < lj).astype(BF16)

    def row_total(ind):
        return _dot(ind, ones).astype(BF16)

    def expert_total(ind):
        return _dot(grp, row_total(ind))

    def prefix(ind):
        return _dot(ind, strict) + _dot(before, row_total(ind))

    def search(i, thrs):
        bit = jnp.left_shift(jnp.int32(1), 30 - i)
        out = []
        for a, t in zip(affs, thrs):
            cand = t | bit
            cnt = expert_total(jnp.where(a >= pltpu.bitcast(cand, F32), 1.0, 0.0).astype(BF16))
            out.append(jnp.where(cnt >= cap, cand, t))
        return tuple(out)

    thrs = lax.fori_loop(0, 31, search, tuple(jnp.zeros((nr, LANES), jnp.int32) for _ in affs))
    for i, (a, thr) in enumerate(zip(affs, thrs)):
        thr_f = pltpu.bitcast(thr, F32)
        gt = a > thr_f
        eq = a == thr_f
        need = cap - expert_total(jnp.where(gt, 1.0, 0.0).astype(BF16))
        eq_rank = prefix(jnp.where(eq, 1.0, 0.0).astype(BF16))
        sel = gt | (eq & (eq_rank < need))
        rank = prefix(jnp.where(sel, 1.0, 0.0).astype(BF16))
        slot_ref[i] = jnp.where(sel, rank, -1.0)


def _select(aff_t, cap, bt):
    bsz, ne, n = aff_t.shape
    rpe = n // LANES
    a2 = aff_t.reshape(bsz, ne * rpe, LANES)
    slot = pl.pallas_call(
        functools.partial(_select_kernel, cap=cap, rows_per_expert=rpe),
        grid=(bsz // bt,),
        in_specs=[pl.BlockSpec((bt, ne * rpe, LANES), lambda b: (b, 0, 0))],
        out_specs=pl.BlockSpec((bt, ne * rpe, LANES), lambda b: (b, 0, 0)),
        out_shape=jax.ShapeDtypeStruct((bsz, ne * rpe, LANES), F32),
        compiler_params=_cparams(("arbitrary",)),
        name="ec_select",
    )(a2)
    return slot.reshape(bsz, ne, n)


def _gather_kernel(slot_ref, h_ref, o_ref, *, cap):
    slot = slot_ref[0, 0]
    j = lax.broadcasted_iota(jnp.int32, (cap, slot.shape[1]), 0).astype(F32)
    onehot = jnp.where(slot == j, 1.0, 0.0).astype(BF16)
    o_ref[0, 0] = _dot(onehot, h_ref[0]).astype(BF16)


def _gather(slot4, h2, cap):
    bsz, ne, _, n = slot4.shape
    d = h2.shape[2]
    return pl.pallas_call(
        functools.partial(_gather_kernel, cap=cap),
        grid=(bsz, ne),
        in_specs=[pl.BlockSpec((1, 1, 1, n), lambda b, e: (b, e, 0, 0)),
                  pl.BlockSpec((1, n, d), lambda b, e: (b, 0, 0))],
        out_specs=pl.BlockSpec((1, 1, cap, d), lambda b, e: (b, e, 0, 0)),
        out_shape=jax.ShapeDtypeStruct((bsz, ne, cap, d), BF16),
        compiler_params=_cparams(("arbitrary", "arbitrary")),
        name="ec_gather",
    )(slot4, h2)


def _expert_kernel(x_ref, w1_ref, w3_ref, w2_ref, o_ref, acc_ref):
    f = pl.program_id(2)
    bt, _, cap, d = x_ref.shape
    x = x_ref[...].reshape(bt * cap, d)

    @pl.when(f == 0)
    def _():
        acc_ref[...] = jnp.zeros_like(acc_ref)

    w1, w3, w2 = w1_ref[0].astype(BF16), w3_ref[0].astype(BF16), w2_ref[0].astype(BF16)
    rows = bt * cap // EXPERT_ROW_TILES
    for r in range(EXPERT_ROW_TILES):
        sl = slice(r * rows, (r + 1) * rows)
        a = _dot(x[sl], w1)
        b = _dot(x[sl], w3)
        acc_ref[sl, :] += _dot((_silu(a) * b).astype(BF16), w2)

    @pl.when(f == pl.num_programs(2) - 1)
    def _():
        o_ref[...] = acc_ref[...].astype(BF16).reshape(bt, 1, cap, d)


def _experts(xg, w1, w3, w2, bt, ft):
    bsz, ne, cap, d = xg.shape
    ff = w1.shape[2]
    return pl.pallas_call(
        _expert_kernel,
        grid=(ne, bsz // bt, ff // ft),
        in_specs=[pl.BlockSpec((bt, 1, cap, d), lambda e, m, f: (m, e, 0, 0)),
                  pl.BlockSpec((1, d, ft), lambda e, m, f: (e, 0, f)),
                  pl.BlockSpec((1, d, ft), lambda e, m, f: (e, 0, f)),
                  pl.BlockSpec((1, ft, d), lambda e, m, f: (e, f, 0))],
        out_specs=pl.BlockSpec((bt, 1, cap, d), lambda e, m, f: (m, e, 0, 0)),
        out_shape=jax.ShapeDtypeStruct((bsz, ne, cap, d), BF16),
        scratch_shapes=[pltpu.VMEM((bt * cap, d), F32)],
        compiler_params=_cparams(("arbitrary", "arbitrary", "arbitrary")),
        name="ec_experts",
    )(xg, w1, w3, w2)


def _combine_kernel(y_ref, slot_ref, aff_ref, x1_ref, g2_ref, fg_ref, o_ref, *, cap):
    _, ne, _, d = y_ref.shape
    slot = slot_ref[0]
    aff = aff_ref[0]
    j = lax.broadcasted_iota(jnp.int32, (slot.shape[0], cap), 1).astype(F32)
    gated = jnp.concatenate([jnp.where(slot[:, e:e + 1] == j, aff[:, e:e + 1], 0.0).astype(BF16)
                             for e in range(ne)], axis=1)
    moe = _dot(gated, y_ref[0].reshape(ne * cap, d))
    x2 = x1_ref[0] + g2_ref[0] * moe
    o_ref[0] = x2 * lax.rsqrt(jnp.mean(x2 * x2, axis=-1, keepdims=True) + EPS) * fg_ref[...]


def _combine(y, slot_nt, aff_nt, x1, mods3, final_g, cap, tn):
    bsz, ne, _, d = y.shape
    n = x1.shape[1]
    return pl.pallas_call(
        functools.partial(_combine_kernel, cap=cap),
        grid=(bsz, n // tn),
        in_specs=[pl.BlockSpec((1, ne, cap, d), lambda b, t: (b, 0, 0, 0)),
                  pl.BlockSpec((1, tn, ne), lambda b, t: (b, t, 0)),
                  pl.BlockSpec((1, tn, ne), lambda b, t: (b, t, 0)),
                  pl.BlockSpec((1, tn, d), lambda b, t: (b, t, 0)),
                  pl.BlockSpec((1, 1, d), lambda b, t: (b, 0, 5)),
                  pl.BlockSpec((1, d), lambda b, t: (0, 0))],
        out_specs=pl.BlockSpec((1, tn, d), lambda b, t: (b, t, 0)),
        out_shape=jax.ShapeDtypeStruct((bsz, n, d), F32),
        compiler_params=_cparams(("arbitrary", "arbitrary")),
        name="ec_combine",
    )(y, slot_nt, aff_nt, x1, mods3, final_g.reshape(1, d))


def kernel(x, c, ctx, c_ctx, ada_w, ada_b, norm1_g, w_in, hy_conv_w, hy_conv_b, hy_pos_w1, hy_pos_b1, hy_freq1,
           hy_pos_w2, hy_pos_b2, hy_freq2, hy_pos_w3, hy_bias, hy_norm_g, hg_lb, hg_norm_g, w_out, norm2_g,
           router_w, exp_w1, exp_w3, exp_w2, final_g):
    bsz, seq, d = x.shape
    hyw = hy_norm_g.shape[1]
    hgw = w_in.shape[2] - 3 * hyw
    hg = hgw // 5
    cap = CAPACITY_FACTOR * seq // N_EXPERTS

    rows = -(-(bsz + 1) // 8) * 8
    s_rows = jnp.zeros((rows, d), F32).at[:bsz].set(c).at[bsz].set(c_ctx)
    mods3 = _ada(s_rows, ada_w[0], ada_b[0]).reshape(rows, 1, 6 * d)

    w_bf = w_in[0].astype(BF16)
    tn = 1024
    p = _inproj(x, mods3, lambda b: b, norm1_g[0], w_bf, 0, w_in.shape[2] // tn, seq, tn)
    w_ctx = w_bf[:, 3 * hyw + hg:3 * hyw + 4 * hg]
    pc = _inproj(ctx, mods3, lambda b: bsz, norm1_g[0], w_ctx, 0, 1, ctx.shape[1], 3 * hg)

    dft_c, dft_s = _dft_tables(seq)
    ct = LANES
    ptab, hm = _hyena_filters(seq, hy_pos_w1[0], hy_pos_b1[0], hy_freq1[0], hy_pos_w2[0], hy_pos_b2[0],
                              hy_freq2[0], hy_pos_w3[0], dft_c, dft_s, 2 * LANES)
    z = _hyena_conv(p, 0, p, 1, hy_conv_w[0], hy_conv_b[0], dft_c, dft_s, ptab, hm, hy_bias[0], 0, True, ct)
    y_hy = _hyena_conv(z, 0, p, 2, hy_conv_w[0], hy_conv_b[0], dft_c, dft_s, ptab, hm, hy_bias[0], 1, False, ct)

    y_hg = _hgrn(p, 3 * hyw // (hg // HG_HEADS), pc, hg_lb, hg_norm_g[0], 2)

    w_o = w_out[0].astype(BF16)
    x1, h2, aff_t = _outproj(y_hy, y_hg, x, mods3, hy_norm_g[0], norm2_g[0], w_o[:hyw], w_o[hyw:],
                             router_w[0].T, 1024)

    slot = _select(aff_t, cap, 8)
    xg = _gather(slot.reshape(bsz, N_EXPERTS, 1, seq), h2, cap)
    y = _experts(xg, exp_w1[0], exp_w3[0], exp_w2[0], 8, 512)
    return _combine(y, jnp.swapaxes(slot, 1, 2), jnp.swapaxes(aff_t, 1, 2), x1, mods3, final_g, cap, 512)
```

```python
import functools
import math

import numpy as np
import jax
import jax.numpy as jnp
from jax import lax
from jax.experimental import pallas as pl
from jax.experimental.pallas import tpu as pltpu

F32 = jnp.float32
BF16 = jnp.bfloat16
HIGHEST = lax.Precision.HIGHEST

GRID_W = 64
HG_HEADS = 4
HY_SHORT = 3
HY_BANDS = 16
HY_DECAY_TARGET = 1e-2
HY_FAST_DECAY = 0.3
HY_SLOW_DECAY = 1.5
HY_WINDOW_SHIFT = 0.05
N_EXPERTS = 16
CAPACITY_FACTOR = 2
EPS = 1e-6

SCAN_CHUNK = 64
SCAN_BLOCK = 16
FAST_BLOCK = 8
MAX_FAST_EXPONENT = 80.0
EXPERT_ROW_TILES = 2
SCAN_UNROLL = 2
LANES = 128
VMEM_LIMIT = 56 * 1024 * 1024


def _cparams(sem):
    return pltpu.CompilerParams(dimension_semantics=sem, vmem_limit_bytes=VMEM_LIMIT)


def _resident(shape, grid_rank):
    zeros = (0,) * len(shape)
    return pl.BlockSpec(shape, {1: lambda a: zeros, 2: lambda a, b: zeros, 3: lambda a, b, c: zeros}[grid_rank],
                        pipeline_mode=pl.Buffered(1))


def _dot(a, b):
    return jnp.dot(a, b, preferred_element_type=F32)


def _dot_nt(a, b):
    return lax.dot_general(a, b, (((1,), (1,)), ((), ())), preferred_element_type=F32)


def _dot_tn(a, b):
    return lax.dot_general(a, b, (((0,), (0,)), ((), ())), preferred_element_type=F32)


def _silu(x):
    return x * jax.nn.sigmoid(x)


def _ada_kernel(s_ref, w_ref, b_ref, o_ref):
    s = _silu(s_ref[...])
    o_ref[...] = jnp.dot(s, w_ref[...], preferred_element_type=F32, precision=HIGHEST) + b_ref[...]


def _ada(s_rows, ada_w, ada_b):
    rows, d = s_rows.shape
    n = ada_w.shape[1]
    tn = 1024
    return pl.pallas_call(
        _ada_kernel,
        grid=(n // tn,),
        in_specs=[pl.BlockSpec((rows, d), lambda j: (0, 0)),
                  pl.BlockSpec((d, tn), lambda j: (0, j)),
                  pl.BlockSpec((1, tn), lambda j: (0, j))],
        out_specs=pl.BlockSpec((rows, tn), lambda j: (0, j)),
        out_shape=jax.ShapeDtypeStruct((rows, n), F32),
        compiler_params=_cparams(("arbitrary",)),
        name="ada",
    )(s_rows, ada_w, ada_b.reshape(1, n))


def _inproj_kernel(x_ref, sh_ref, sc_ref, g_ref, w_ref, o_ref, h_scr):
    @pl.when(pl.program_id(2) == 0)
    def _():
        x = x_ref[0]
        y = x * lax.rsqrt(jnp.mean(x * x, axis=-1, keepdims=True) + EPS) * g_ref[...]
        h_scr[...] = (y * (1.0 + sc_ref[0]) + sh_ref[0]).astype(BF16)

    o_ref[0] = _dot(h_scr[...], w_ref[...])


def _inproj(x, mods3, mod_row, norm_g, w_bf, col_blk0, n_col_blk, tm, tn):
    bsz, seq, d = x.shape
    return pl.pallas_call(
        _inproj_kernel,
        grid=(bsz, seq // tm, n_col_blk),
        in_specs=[pl.BlockSpec((1, tm, d), lambda b, i, j: (b, i, 0)),
                  pl.BlockSpec((1, 1, d), lambda b, i, j: (mod_row(b), 0, 0)),
                  pl.BlockSpec((1, 1, d), lambda b, i, j: (mod_row(b), 0, 1)),
                  pl.BlockSpec((1, d), lambda b, i, j: (0, 0)),
                  pl.BlockSpec((d, tn), lambda b, i, j: (0, col_blk0 + j))],
        out_specs=pl.BlockSpec((1, tm, tn), lambda b, i, j: (b, i, j)),
        out_shape=jax.ShapeDtypeStruct((bsz, seq, n_col_blk * tn), F32),
        scratch_shapes=[pltpu.VMEM((tm, d), BF16)],
        compiler_params=_cparams(("arbitrary", "arbitrary", "arbitrary")),
        name="inproj",
    )(x, mods3, mods3, norm_g.reshape(1, d), w_bf)


def _gate_floor(lb_raw):
    e = jnp.exp(lb_raw - jnp.max(lb_raw, axis=0, keepdims=True))
    return e[0] / jnp.sum(e, axis=0)


def _forget_gate(z, lb):
    f = lb + (1.0 - lb) * jax.nn.sigmoid(z)
    return 1.0 - f, jnp.log(f)


def _inproj_mix_kernel(x_ref, sh_ref, sc_ref, g_ref, w_ref, lb_ref, o_ref, lf_ref, h_scr, *, n_col_blk, gw,
                       q_group, zf_group, zb_group, q_scale):
    j = pl.program_id(2)

    @pl.when(j == 0)
    def _():
        x = x_ref[0]
        y = x * lax.rsqrt(jnp.mean(x * x, axis=-1, keepdims=True) + EPS) * g_ref[...]
        h_scr[...] = (y * (1.0 + sc_ref[0]) + sh_ref[0]).astype(BF16)

    lb = _gate_floor(lb_ref[...])
    per_tile = o_ref.shape[2] // gw
    for jj in range(n_col_blk):
        @pl.when(j == jj)
        def _(jj=jj):
            for loc in range(per_tile):
                grp = jj * per_tile + loc
                part = _dot(h_scr[...], w_ref[:, loc * gw:(loc + 1) * gw])
                if grp == q_group:
                    part = _silu(part) * q_scale
                elif grp in (zf_group, zb_group):
                    d = 0 if grp == zf_group else 1
                    part, lf_ref[0] = _forget_gate(part, lb[d:d + 1, :])
                o_ref[0, :, loc * gw:(loc + 1) * gw] = part


def _inproj_mix(x, mods3, norm_g, w_bf, hg_lb, tn, gw, q_group, zf_group, zb_group, q_scale):
    bsz, seq, d = x.shape
    n_col_blk = w_bf.shape[1] // tn
    z_tile0 = zf_group * gw // tn
    assert zb_group * gw // tn == z_tile0 + 1 and zf_group * gw // tn != q_group * gw // tn
    return pl.pallas_call(
        functools.partial(_inproj_mix_kernel, n_col_blk=n_col_blk, gw=gw, q_group=q_group, zf_group=zf_group,
                          zb_group=zb_group, q_scale=q_scale),
        grid=(bsz, 1, n_col_blk),
        in_specs=[pl.BlockSpec((1, seq, d), lambda b, i, j: (b, 0, 0)),
                  pl.BlockSpec((1, 1, d), lambda b, i, j: (b, 0, 0)),
                  pl.BlockSpec((1, 1, d), lambda b, i, j: (b, 0, 1)),
                  pl.BlockSpec((1, d), lambda b, i, j: (0, 0)),
                  pl.BlockSpec((d, tn), lambda b, i, j: (0, j)),
                  pl.BlockSpec((hg_lb.shape[0], 2, gw), lambda b, i, j: (0, 0, 0))],
        out_specs=[pl.BlockSpec((1, seq, tn), lambda b, i, j: (b, 0, j)),
                   pl.BlockSpec((1, seq, gw), lambda b, i, j: (b, 0, jnp.maximum(j - z_tile0, 0)))],
        out_shape=[jax.ShapeDtypeStruct((bsz, seq, w_bf.shape[1]), F32),
                   jax.ShapeDtypeStruct((bsz, seq, 2 * gw), F32)],
        scratch_shapes=[pltpu.VMEM((seq, d), BF16)],
        compiler_params=_cparams(("arbitrary", "arbitrary", "arbitrary")),
        name="inproj_mix",
    )(x, mods3, mods3, norm_g.reshape(1, d), w_bf, hg_lb)


def _dft_tables(seq):
    half = seq // 2
    k = np.arange(half, dtype=np.int64)
    idx = (k[:, None] * k[None, :]) % seq
    ang = 2.0 * np.pi * np.arange(seq, dtype=np.float64) / seq
    cos_t = jnp.asarray(np.cos(ang)[idx].astype(np.float32))
    sin_t = jnp.asarray(np.sin(ang)[idx].astype(np.float32))
    return cos_t.astype(BF16), sin_t.astype(BF16)


def _pos_features(seq):
    t = np.linspace(0.0, 1.0, seq, dtype=np.float32)[:, None]
    w = (2.0 * math.pi * np.arange(seq, dtype=np.float32)[:, None] / seq).astype(np.float32)
    f = np.linspace(1e-4, HY_BANDS - 1, HY_BANDS, dtype=np.float32)[None, :]
    fw = (f * w).astype(np.float32)
    return np.concatenate([t, np.cos(fw), -np.sin(fw)], axis=-1).astype(np.float32)


def _filter_kernel(z_ref, w1_ref, b1_ref, f1_ref, w2_ref, b2_ref, f2_ref, w3f_ref, w3b_ref, t_ref, dl_ref,
                   sgn_ref, wk_ref, c2_ref, s2_ref, c_ref, s_ref, p_ref, hm_ref, hid_ref):
    half = c_ref.shape[0]
    n = 4 * half

    @pl.when((pl.program_id(0) == 0) & (pl.program_id(1) == 0))
    def _():
        for par in range(2):
            h = jnp.sin(f1_ref[...] * (jnp.dot(z_ref[par], w1_ref[...], preferred_element_type=F32,
                                               precision=HIGHEST) + b1_ref[...]))
            hid_ref[par] = jnp.sin(f2_ref[...] * (jnp.dot(h, w2_ref[...], preferred_element_type=F32,
                                                          precision=HIGHEST) + b2_ref[...]))

    taps = {}
    for par in range(2):
        h = hid_ref[par]
        window = jnp.exp(-t_ref[par] * dl_ref[...]) + HY_WINDOW_SHIFT
        taps["f", par] = jnp.dot(h, w3f_ref[...], preferred_element_type=F32, precision=HIGHEST) * window
        taps["b", par] = jnp.dot(h, w3b_ref[...], preferred_element_type=F32, precision=HIGHEST) * window
    row = lax.broadcasted_iota(jnp.int32, taps["b", 0].shape, 0)
    taps["b", 0] = jnp.where(row == 0, 0.0, taps["b", 0])
    inv = 1.0 / (sum(jnp.sum(jnp.abs(x), axis=0, keepdims=True) for x in taps.values()) + EPS)
    sgn = sgn_ref[...]
    spec = {}
    for key, x in taps.items():
        x = x * inv
        xb = x.astype(BF16)
        spec[key] = (_dot(c_ref[...], xb), _dot(s_ref[...], xb), jnp.sum(x * sgn, axis=0, keepdims=True))
    (fec, fes, fen), (foc, fos, fon) = spec["f", 0], spec["f", 1]
    (bec, bes, ben), (boc, bos, bon) = spec["b", 0], spec["b", 1]
    c2, s2 = c2_ref[...], s2_ref[...]
    wk = 2.0 * wk_ref[...]
    p_ref[0, 0] = (fec + bec) * wk
    p_ref[0, 1] = (bes - fes) * wk
    p_ref[0, 2] = (c2 * foc - s2 * fos + boc) * wk
    p_ref[0, 3] = (bos - c2 * fos - s2 * foc) * wk
    p_ref[0, 4] = (foc + c2 * boc - s2 * bos) * wk
    p_ref[0, 5] = (c2 * bos + s2 * boc - fos) * wk
    hm_ref[0] = jnp.concatenate([fen + ben, bon - fon], axis=0) * (2.0 / n)


def _hyena_filters(seq, pos_w1, pos_b1, freq1, pos_w2, pos_b2, freq2, pos_w3, dft_c, dft_s, ct):
    orders = 2
    half = seq // 2
    ch = pos_w3.shape[1] // (orders * 2)
    zt = _pos_features(seq)
    pdim = -(-zt.shape[1] // LANES) * LANES
    hid = -(-pos_w1.shape[1] // LANES) * LANES
    zt = np.pad(zt, ((0, 0), (0, pdim - zt.shape[1])))
    zt = np.stack([zt[0::2], zt[1::2]])
    pad2 = lambda a, r, c_: jnp.pad(a, ((0, r - a.shape[0]), (0, c_ - a.shape[1])))
    pos_w1 = pad2(pos_w1, pdim, hid)
    pos_w2 = pad2(pos_w2, hid, hid)
    pos_w3 = pad2(pos_w3, hid, pos_w3.shape[1])
    pos_b1, freq1, pos_b2, freq2 = (pad2(a.reshape(1, -1), 1, hid) for a in (pos_b1, freq1, pos_b2, freq2))
    t_col = np.linspace(0.0, 1.0, seq, dtype=np.float32)[:, None]
    t_col = np.stack([t_col[0::2], t_col[1::2]])
    min_decay = math.log(HY_DECAY_TARGET) / HY_SLOW_DECAY
    max_decay = math.log(HY_DECAY_TARGET) / HY_FAST_DECAY
    deltas = np.abs(np.linspace(min_decay, max_decay, ch, dtype=np.float32))[None, :]
    sgn = np.where(np.arange(half) % 2 == 0, 1.0, -1.0).astype(np.float32)[:, None]
    wk = np.full((half, 1), 2.0 / (2 * seq), np.float32)
    wk[0, 0] = 1.0 / (2 * seq)
    ang = 2.0 * np.pi * np.arange(half, dtype=np.float64)[:, None] / seq
    ncb = ch // ct
    const = lambda shape: pl.BlockSpec(shape, lambda o, c: (0,) * len(shape))
    return pl.pallas_call(
        _filter_kernel,
        grid=(orders, ncb),
        in_specs=[const((2, half, pdim)), const((pdim, hid)), const((1, hid)), const((1, hid)),
                  const((hid, hid)), const((1, hid)), const((1, hid)),
                  pl.BlockSpec((hid, ct), lambda o, c: (0, (2 * o) * ncb + c)),
                  pl.BlockSpec((hid, ct), lambda o, c: (0, (2 * o + 1) * ncb + c)),
                  const((2, half, 1)),
                  pl.BlockSpec((1, ct), lambda o, c: (0, c)),
                  const((half, 1)), const((half, 1)), const((half, 1)), const((half, 1)),
                  const((half, half)), const((half, half))],
        out_specs=[pl.BlockSpec((1, 6, half, ct), lambda o, c: (o, 0, 0, c)),
                   pl.BlockSpec((1, 2, ct), lambda o, c: (o, 0, c))],
        out_shape=[jax.ShapeDtypeStruct((orders, 6, half, ch), F32),
                   jax.ShapeDtypeStruct((orders, 2, ch), F32)],
        scratch_shapes=[pltpu.VMEM((2, half, hid), F32)],
        compiler_params=_cparams(("arbitrary", "arbitrary")),
        name="hyena_filter",
    )(jnp.asarray(zt), pos_w1, pos_b1, freq1, pos_w2, pos_b2, freq2, pos_w3, pos_w3, jnp.asarray(t_col),
      jnp.asarray(deltas), jnp.asarray(sgn), jnp.asarray(wk), jnp.asarray(np.cos(ang).astype(np.float32)),
      jnp.asarray(np.sin(ang).astype(np.float32)), dft_c, dft_s)


def _split_rows(x_ref):
    half = x_ref.shape[1] // 2
    return x_ref[0, pl.ds(0, half, stride=2), :], x_ref[0, pl.ds(1, half, stride=2), :]


def _short_conv_split(p_ref, w_ref, b_ref):
    pe, po = _split_rows(p_ref)
    half = pe.shape[0]
    pos = lax.broadcasted_iota(jnp.int32, pe.shape, 0) % (GRID_W // 2)
    before_even = jnp.where(pos == 0, 0.0, pltpu.roll(po, 1, axis=0))
    after_odd = jnp.where(pos == GRID_W // 2 - 1, 0.0, pltpu.roll(pe, half - 1, axis=0))
    w0, w1, w2 = w_ref[0:1, :], w_ref[1:2, :], w_ref[2:3, :]
    return (b_ref[...] + before_even * w0 + pe * w1 + po * w2,
            b_ref[...] + pe * w0 + po * w1 + after_odd * w2)


def _hyena_kernel(u_ref, g_ref, cwu_ref, cbu_ref, cwg_ref, cbg_ref, fwd_ref, inv_ref, p_ref, hm_ref,
                  bias_ref, sgn_ref, o_ref, spec_ref, *, conv_u, freq_tile):
    half = inv_ref.shape[0]
    ct = o_ref.shape[2]
    ue, uo = _short_conv_split(u_ref, cwu_ref, cbu_ref) if conv_u else _split_rows(u_ref)
    ub = jnp.concatenate([ue, uo], axis=1).astype(BF16)
    for ft in range(half // freq_tile):
        lo, hi = ft * freq_tile, (ft + 1) * freq_tile
        dc = _dot(fwd_ref[lo:hi, :], ub)
        ds = _dot(fwd_ref[half + lo:half + hi, :], ub)
        er, orr, es, os_ = dc[:, :ct], dc[:, ct:], ds[:, :ct], ds[:, ct:]
        p1r, p1i, p2r, p2i, p3r, p3i = (p_ref[0, i, lo:hi, :] for i in range(6))
        ar = er * p1r + es * p1i + orr * p2r + os_ * p2i
        br = er * p3r + es * p3i + orr * p1r + os_ * p1i
        ai = es * p1r - er * p1i + os_ * p2r - orr * p2i
        bi = es * p3r - er * p3i + os_ * p1r - orr * p1i
        spec_ref[lo:hi, :] = jnp.concatenate([ar, br], axis=1).astype(BF16)
        spec_ref[half + lo:half + hi, :] = jnp.concatenate([ai, bi], axis=1).astype(BF16)
    y = _dot(inv_ref[...], spec_ref[...])
    sgn = sgn_ref[...]
    en = jnp.sum(ue * sgn, axis=0, keepdims=True)
    on = jnp.sum(uo * sgn, axis=0, keepdims=True)
    hm_r, hm_i = hm_ref[0, 0:1, :], hm_ref[0, 1:2, :]
    bias = bias_ref[0]
    ge, go = _short_conv_split(g_ref, cwg_ref, cbg_ref)
    o_ref[0, pl.ds(0, half, stride=2), :] = ge * (y[:, :ct] + ue * bias + sgn * (en * hm_r + on * hm_i))
    o_ref[0, pl.ds(1, half, stride=2), :] = go * (y[:, ct:] + uo * bias + sgn * (on * hm_r - en * hm_i))


def _hyena_conv(u, u_col_blk, p_hy, gate_col_blk, conv_w, conv_b, dft_c, dft_s, ptab, hm, bias, order,
                conv_u, ct):
    bsz, seq, _ = p_hy.shape
    half = seq // 2
    ch = ptab.shape[3]
    ncb = ch // ct
    sgn = jnp.asarray(np.where(np.arange(half) % 2 == 0, 1.0, -1.0).astype(np.float32)[:, None])
    taps = conv_w.shape[0]
    cbias = conv_b.reshape(1, -1)
    const = lambda shape: pl.BlockSpec(shape, lambda c, b: (0,) * len(shape))
    return pl.pallas_call(
        functools.partial(_hyena_kernel, conv_u=conv_u, freq_tile=half // 4),
        grid=(ncb, bsz),
        in_specs=[pl.BlockSpec((1, seq, ct), lambda c, b: (b, 0, u_col_blk * ncb + c)),
                  pl.BlockSpec((1, seq, ct), lambda c, b: (b, 0, gate_col_blk * ncb + c)),
                  pl.BlockSpec((taps, ct), lambda c, b: (0, u_col_blk * ncb + c)),
                  pl.BlockSpec((1, ct), lambda c, b: (0, u_col_blk * ncb + c)),
                  pl.BlockSpec((taps, ct), lambda c, b: (0, gate_col_blk * ncb + c)),
                  pl.BlockSpec((1, ct), lambda c, b: (0, gate_col_blk * ncb + c)),
                  const((2 * half, half)), const((half, 2 * half)),
                  pl.BlockSpec((1, 6, half, ct), lambda c, b: (order, 0, 0, c)),
                  pl.BlockSpec((1, 2, ct), lambda c, b: (order, 0, c)),
                  pl.BlockSpec((1, 1, ct), lambda c, b: (order, 0, c)),
                  const((half, 1))],
        out_specs=pl.BlockSpec((1, seq, ct), lambda c, b: (b, 0, c)),
        out_shape=jax.ShapeDtypeStruct((bsz, seq, ch), F32),
        scratch_shapes=[pltpu.VMEM((2 * half, 2 * ct), BF16)],
        compiler_params=_cparams(("arbitrary", "arbitrary")),
        name=f"hyena_conv{order}",
    )(u, p_hy, conv_w, cbias, conv_w, cbias, jnp.concatenate([dft_c, dft_s], axis=0),
      jnp.concatenate([dft_c, dft_s], axis=1), ptab, hm, bias.reshape(bias.shape[0], 1, ch), sgn)


def _scan_consts():
    c, blk = SCAN_CHUNK, SCAN_BLOCK
    r = lax.broadcasted_iota(jnp.int32, (c, c), 0)
    s = lax.broadcasted_iota(jnp.int32, (c, c), 1)
    row = lax.broadcasted_iota(jnp.int32, (c, LANES), 0)
    fb = FAST_BLOCK

    def pair_mask(size, rev):
        same = (r // (2 * size)) == (s // (2 * size))
        hi_r, hi_s = r % (2 * size) >= size, s % (2 * size) >= size
        return same & ((~hi_r & hi_s) if rev else (hi_r & ~hi_s))

    return dict(
        cum_f=(s <= r).astype(F32), cum_b=(s >= r).astype(F32),
        m32_f=(r >= c // 2) & (s < c // 2), m32_b=(r < c // 2) & (s >= c // 2),
        m16_f=pair_mask(blk, False), m16_b=pair_mask(blk, True),
        m8_f=pair_mask(fb, False), m8_b=pair_mask(fb, True),
        md_f=(r // fb == s // fb) & (s <= r), md_b=(r // fb == s // fb) & (s >= r),
        sub=lax.broadcasted_iota(jnp.int32, (fb, LANES), 0),
        row=row, ones=jnp.ones((LANES, LANES), BF16))


def _state_update(st_ref, k, v, g, g_edge):
    kd = (k * jnp.exp(g_edge - g)).astype(BF16)
    st_ref[...] = st_ref[...] * jnp.exp(g_edge) + _dot_tn(v.astype(BF16), kd)


def _chunk_cumsum(logf, cst, reverse):
    return jnp.dot(cst["cum_b" if reverse else "cum_f"], logf, preferred_element_type=F32, precision=HIGHEST)


def _scan_chunk(st_ref, q, k, v, logf, cst, reverse):
    c, blk = SCAN_CHUNK, SCAN_BLOCK
    row = cst["row"]
    g = _chunk_cumsum(logf, cst, reverse)
    o = _dot_nt((q * jnp.exp(g)).astype(BF16), st_ref[...].astype(BF16))
    half = c // 2
    if not reverse:
        ref32 = g[half - 1:half, :]
        ref16 = jnp.where(row < half, g[blk - 1:blk, :], g[half + blk - 1:half + blk, :])
        t32, s32 = row >= half, row < half
        t16, s16 = row % (2 * blk) >= blk, row % (2 * blk) < blk
        g_edge = g[c - 1:c, :]
    else:
        ref32 = g[half:half + 1, :]
        ref16 = jnp.where(row < half, g[blk:blk + 1, :], g[half + blk:half + blk + 1, :])
        t32, s32 = row < half, row >= half
        t16, s16 = row % (2 * blk) < blk, row % (2 * blk) >= blk
        g_edge = g[0:1, :]
    a32 = jnp.where(t32, q * jnp.exp(jnp.minimum(g - ref32, 0.0)), 0.0).astype(BF16)
    b32 = jnp.where(s32, k * jnp.exp(jnp.minimum(ref32 - g, 0.0)), 0.0).astype(BF16)
    a16 = jnp.where(t16, q * jnp.exp(jnp.minimum(g - ref16, 0.0)), 0.0).astype(BF16)
    b16 = jnp.where(s16, k * jnp.exp(jnp.minimum(ref16 - g, 0.0)), 0.0).astype(BF16)
    att = (jnp.where(cst["m32_b" if reverse else "m32_f"], _dot_nt(a32, b32), 0.0)
           + jnp.where(cst["m16_b" if reverse else "m16_f"], _dot_nt(a16, b16), 0.0))
    o = o + _dot(att.astype(BF16), v.astype(BF16))
    pos = row % blk
    for d in range(blk):
        shift = d if not reverse else (c - d) % c
        if d == 0:
            ks, gs, vs = k, g, v
        else:
            ks = pltpu.roll(k, shift, axis=0)
            gs = pltpu.roll(g, shift, axis=0)
            vs = pltpu.roll(v, shift, axis=0)
        valid = (pos >= d) if not reverse else (pos + d < blk)
        w = jnp.where(valid, q * ks * jnp.exp(jnp.minimum(g - gs, 0.0)), 0.0)
        o = o + _dot(w.astype(BF16), cst["ones"]) * vs
    _state_update(st_ref, k, v, g, g_edge)
    return o


def _chunk_prepare(q, k, v, logf, cst, reverse):
    c, fb = SCAN_CHUNK, FAST_BLOCK
    nb = c // fb
    d = "b" if reverse else "f"
    sub = cst["sub"]
    blocks = lambda x: [x[fb * j:fb * (j + 1), :] for j in range(nb)]
    qs, ks = blocks(q), blocks(k)
    ps = []
    for p in blocks(logf):
        for s in (1, 2, 4):
            if reverse:
                p = p + jnp.where(sub + s < fb, pltpu.roll(p, fb - s, axis=0), 0.0)
            else:
                p = p + jnp.where(sub >= s, pltpu.roll(p, s, axis=0), 0.0)
        ps.append(p)
    last = 0 if reverse else fb - 1
    tot = [jnp.broadcast_to(p[last:last + 1, :], p.shape) for p in ps]
    decay = [jnp.exp(t) for t in tot]
    memo = {}

    def span(lo, hi):
        if hi <= lo:
            return None
        if (lo, hi) not in memo:
            if hi == nb and lo > 0:
                inner, last = span(lo + 1, hi), decay[lo]
            else:
                inner, last = span(lo, hi - 1), decay[hi - 1]
            memo[lo, hi] = last if inner is None else inner * last
        return memo[lo, hi]

    scale = lambda x, f: x if f is None else x * f
    qe = [qs[j] * jnp.exp(ps[j]) for j in range(nb)]
    kb = [ks[j] * jnp.exp(tot[j] - ps[j]) for j in range(nb)]
    zero = jnp.zeros_like(tot[0])
    cat = lambda parts: jnp.concatenate(parts, axis=0).astype(BF16)

    def level(n):
        a, b = [], []
        for j in range(nb):
            mid = (j // (2 * n)) * 2 * n + n
            later = j >= mid
            if later != reverse:
                a.append(scale(qe[j], span(j + 1, mid) if reverse else span(mid, j)))
                b.append(zero)
            else:
                a.append(zero)
                b.append(scale(kb[j], span(mid, j) if reverse else span(j + 1, mid)))
        return _dot_nt(cat(a), cat(b))

    bd = cat([ks[j] * jnp.exp(-ps[j]) for j in range(nb)])
    att = (level(nb // 2)
           + jnp.where(cst["m16_" + d], level(nb // 4), 0.0)
           + jnp.where(cst["m8_" + d], level(nb // 8), 0.0)
           + jnp.where(cst["md_" + d], _dot_nt(cat(qe), bd), 0.0))
    before = lambda j: span(j + 1, nb) if reverse else span(0, j)
    after = lambda j: span(0, j) if reverse else span(j + 1, nb)
    qd = cat([scale(qe[j], before(j)) for j in range(nb)])
    kd = cat([scale(kb[j], after(j)) for j in range(nb)])
    vb = v.astype(BF16)
    return att.astype(BF16), qd, vb, _dot_tn(vb, kd), span(0, nb)[0:1, :]


def _chunk_finish(st_ref, prepared):
    att, qd, vb, kv, decay = prepared
    o = _dot_nt(qd, st_ref[...].astype(BF16)) + _dot(att, vb)
    st_ref[...] = st_ref[...] * decay + kv
    return o


def _hgrn_kernel(q_ref, i_ref, kf_ref, kb_ref, g_ref, lf_ref, lbw_ref, ic_ref, zfc_ref, zbc_ref, lb_ref, ng_ref,
                 o_ref, of_ref, ob_ref, st_ref):
    seq = q_ref.shape[1]
    ctx_len = ic_ref.shape[1]
    hd = ng_ref.shape[1]
    heads = q_ref.shape[2] // hd
    c = SCAN_CHUNK
    n_chunks, n_ctx = seq // c, ctx_len // c
    cst = _scan_consts()
    lb = _gate_floor(lb_ref[...])
    lanes = [slice(h * hd, (h + 1) * hd) for h in range(heads)]
    st_f = [st_ref.at[2 * h] for h in range(heads)]
    st_b = [st_ref.at[2 * h + 1] for h in range(heads)]

    st_ref[...] = jnp.zeros_like(st_ref)

    def ctx_body(j, carry):
        rf = pl.multiple_of(j * c, c)
        rb = pl.multiple_of((n_ctx - 1 - j) * c, c)
        for h, ln in enumerate(lanes):
            for st, z_ref, r, lb_row, rev in ((st_f[h], zfc_ref, rf, lb[0:1, ln], False),
                                              (st_b[h], zbc_ref, rb, lb[1:2, ln], True)):
                k, logf = _forget_gate(z_ref[0, pl.ds(r, c), ln], lb_row)
                *_, kv, decay = _chunk_prepare(jnp.zeros_like(k), k, ic_ref[0, pl.ds(r, c), ln], logf, cst, rev)
                st[...] = st[...] * decay + kv
        return carry

    lax.fori_loop(0, n_ctx, ctx_body, 0)

    def rows(j):
        if isinstance(j, int):
            return j * c, (n_chunks - 1 - j) * c
        return pl.multiple_of(j * c, c), pl.multiple_of((n_chunks - 1 - j) * c, c)

    def operands(r, ln, k_ref, logf_ref):
        sl = (0, pl.ds(r, c), ln)
        return q_ref[sl], k_ref[sl], i_ref[sl], logf_ref[sl]

    def direct_body(j, carry):
        rf, rb = rows(j)
        for h, ln in enumerate(lanes):
            of_ref[pl.ds(rf, c), ln] = _scan_chunk(st_f[h], *operands(rf, ln, kf_ref, lf_ref), cst, False)
            ob_ref[pl.ds(rb, c), ln] = _scan_chunk(st_b[h], *operands(rb, ln, kb_ref, lbw_ref), cst, True)
        return carry

    def prepare(j):
        rf, rb = rows(j)
        out = []
        for ln in lanes:
            out.append(_chunk_prepare(*operands(rf, ln, kf_ref, lf_ref), cst, False))
            out.append(_chunk_prepare(*operands(rb, ln, kb_ref, lbw_ref), cst, True))
        return tuple(out)

    def finish(j, prepared):
        rf, rb = rows(j)
        for h, ln in enumerate(lanes):
            of_ref[pl.ds(rf, c), ln] = _chunk_finish(st_f[h], prepared[2 * h])
            ob_ref[pl.ds(rb, c), ln] = _chunk_finish(st_b[h], prepared[2 * h + 1])

    def factored_body(j, prepared):
        upcoming = prepare(j + 1)
        finish(j, prepared)
        return upcoming

    bounded = jnp.min(lb) >= math.exp(-MAX_FAST_EXPONENT / FAST_BLOCK)

    @pl.when(bounded)
    def _():
        looped = (n_chunks - 1) // SCAN_UNROLL * SCAN_UNROLL
        prepared = lax.fori_loop(0, looped, factored_body, prepare(0), unroll=SCAN_UNROLL)
        for j in range(looped, n_chunks - 1):
            prepared = factored_body(j, prepared)
        finish(n_chunks - 1, prepared)

    @pl.when(jnp.logical_not(bounded))
    def _():
        lax.fori_loop(0, n_chunks, direct_body, 0)

    for ln in lanes:
        o = of_ref[:, ln] + ob_ref[:, ln]
        y = o * lax.rsqrt(jnp.mean(o * o, axis=-1, keepdims=True) + EPS) * ng_ref[...]
        o_ref[0, :, ln] = y * _silu(g_ref[0, :, ln])


def _hgrn(p, col0, logf, pc, hg_lb, norm_g, heads_per_step):
    bsz, seq, _ = p.shape
    ctx_len = pc.shape[1]
    hd = norm_g.shape[0]
    w = heads_per_step * hd
    steps = HG_HEADS // heads_per_step
    lat = lambda grp: pl.BlockSpec((1, seq, w), lambda b, h: (b, 0, (col0 + grp * HG_HEADS) // heads_per_step + h))
    lgf = lambda d: pl.BlockSpec((1, seq, w), lambda b, h: (b, 0, d * steps + h))
    ctx = lambda grp: pl.BlockSpec((1, ctx_len, w), lambda b, h: (b, 0, grp * steps + h))
    return pl.pallas_call(
        _hgrn_kernel,
        grid=(bsz, steps),
        in_specs=[lat(0), lat(1), lat(2), lat(3), lat(4), lgf(0), lgf(1), ctx(0), ctx(1), ctx(2),
                  pl.BlockSpec((hg_lb.shape[0], 2, w), lambda b, h: (0, 0, h)),
                  pl.BlockSpec((1, hd), lambda b, h: (0, 0))],
        out_specs=pl.BlockSpec((1, seq, w), lambda b, h: (b, 0, h)),
        out_shape=jax.ShapeDtypeStruct((bsz, seq, HG_HEADS * hd), F32),
        scratch_shapes=[pltpu.VMEM((seq, w), F32), pltpu.VMEM((seq, w), F32),
                        pltpu.VMEM((2 * heads_per_step, hd, hd), F32)],
        compiler_params=_cparams(("arbitrary", "arbitrary")),
        name="hgrn_scan",
    )(p, p, p, p, p, logf, logf, pc, pc, pc, hg_lb, norm_g.reshape(1, hd))


def _outproj_kernel(yhy_ref, yhg_ref, x_ref, g1_ref, sh2_ref, sc2_ref, hyg_ref, n2g_ref, wt_ref, wb_ref, rw_ref,
                    x1_ref, h2_ref, aff_ref):
    yhy = yhy_ref[0]
    yn = yhy * lax.rsqrt(jnp.mean(yhy * yhy, axis=-1, keepdims=True) + EPS) * hyg_ref[...]
    mix = _dot(yn.astype(BF16), wt_ref[...]) + _dot(yhg_ref[0].astype(BF16), wb_ref[...])
    x1 = x_ref[0] + g1_ref[0] * mix
    x1_ref[0] = x1
    h2 = (x1 * lax.rsqrt(jnp.mean(x1 * x1, axis=-1, keepdims=True) + EPS) * n2g_ref[...]
          * (1.0 + sc2_ref[0]) + sh2_ref[0])
    h_hi = h2.astype(BF16)
    h2_ref[0] = h_hi
    h_lo = (h2 - h_hi.astype(F32)).astype(BF16)
    rw = rw_ref[...]
    r_hi = rw.astype(BF16)
    r_lo = (rw - r_hi.astype(F32)).astype(BF16)
    ne = rw.shape[0]
    both = _dot_nt(jnp.concatenate([r_hi, r_lo], axis=0), h_hi)
    logits = both[:ne] + both[ne:] + _dot_nt(r_hi, h_lo)
    ex = jnp.exp(logits - jnp.max(logits, axis=0, keepdims=True))
    aff_ref[0] = ex / jnp.sum(ex, axis=0, keepdims=True)


def _outproj(y_hy, y_hg, x, mods3, hy_norm_g, norm2_g, w_top, w_bot, router_wt, tm):
    bsz, seq, d = x.shape
    wh = y_hy.shape[2]
    ne = router_wt.shape[0]
    mod = lambda k: pl.BlockSpec((1, 1, d), lambda b, i: (b, 0, k))
    const = lambda shape: pl.BlockSpec(shape, lambda b, i: (0,) * len(shape))
    return pl.pallas_call(
        _outproj_kernel,
        grid=(bsz, seq // tm),
        in_specs=[pl.BlockSpec((1, tm, wh), lambda b, i: (b, i, 0)),
                  pl.BlockSpec((1, tm, wh), lambda b, i: (b, i, 0)),
                  pl.BlockSpec((1, tm, d), lambda b, i: (b, i, 0)),
                  mod(2), mod(3), mod(4),
                  const((1, wh)), const((1, d)), const((wh, d)), const((wh, d)), const((ne, d))],
        out_specs=[pl.BlockSpec((1, tm, d), lambda b, i: (b, i, 0)),
                   pl.BlockSpec((1, tm, d), lambda b, i: (b, i, 0)),
                   pl.BlockSpec((1, ne, tm), lambda b, i: (b, 0, i))],
        out_shape=[jax.ShapeDtypeStruct((bsz, seq, d), F32),
                   jax.ShapeDtypeStruct((bsz, seq, d), BF16),
                   jax.ShapeDtypeStruct((bsz, ne, seq), F32)],
        compiler_params=_cparams(("arbitrary", "arbitrary")),
        name="outproj_router",
    )(y_hy, y_hg, x, mods3, mods3, mods3, hy_norm_g.reshape(1, wh), norm2_g.reshape(1, d), w_top, w_bot, router_wt)


def _select_kernel(a_ref, slot_ref, *, cap, rows_per_expert):
    nr = a_ref.shape[1]
    affs = [a_ref[i] for i in range(a_ref.shape[0])]
    r = lax.broadcasted_iota(jnp.int32, (nr, nr), 0)
    s = lax.broadcasted_iota(jnp.int32, (nr, nr), 1)
    same = (r // rows_per_expert) == (s // rows_per_expert)
    grp = same.astype(BF16)
    before = (same & (s < r)).astype(BF16)
    li = lax.broadcasted_iota(jnp.int32, (LANES, LANES), 0)
    lj = lax.broadcasted_iota(jnp.int32, (LANES, LANES), 1)
    ones = jnp.ones((LANES, LANES), BF16)
    strict = (li < lj).astype(BF16)

    def row_total(ind):
        return _dot(ind, ones).astype(BF16)

    def expert_total(ind):
        return _dot(grp, row_total(ind))

    def prefix(ind):
        return _dot(ind, strict) + _dot(before, row_total(ind))

    def search(i, thrs):
        bit = jnp.left_shift(jnp.int32(1), 30 - i)
        out = []
        for a, t in zip(affs, thrs):
            cand = t | bit
            cnt = expert_total(jnp.where(a >= pltpu.bitcast(cand, F32), 1.0, 0.0).astype(BF16))
            out.append(jnp.where(cnt >= cap, cand, t))
        return tuple(out)

    thrs = lax.fori_loop(0, 31, search, tuple(jnp.zeros((nr, LANES), jnp.int32) for _ in affs))
    for i, (a, thr) in enumerate(zip(affs, thrs)):
        thr_f = pltpu.bitcast(thr, F32)
        gt = a > thr_f
        eq = a == thr_f
        need = cap - expert_total(jnp.where(gt, 1.0, 0.0).astype(BF16))
        eq_rank = prefix(jnp.where(eq, 1.0, 0.0).astype(BF16))
        sel = gt | (eq & (eq_rank < need))
        rank = prefix(jnp.where(sel, 1.0, 0.0).astype(BF16))
        slot_ref[i] = jnp.where(sel, rank, -1.0)


def _select(aff_t, cap, bt):
    bsz, ne, n = aff_t.shape
    rpe = n // LANES
    a2 = aff_t.reshape(bsz, ne * rpe, LANES)
    slot = pl.pallas_call(
        functools.partial(_select_kernel, cap=cap, rows_per_expert=rpe),
        grid=(bsz // bt,),
        in_specs=[pl.BlockSpec((bt, ne * rpe, LANES), lambda b: (b, 0, 0))],
        out_specs=pl.BlockSpec((bt, ne * rpe, LANES), lambda b: (b, 0, 0)),
        out_shape=jax.ShapeDtypeStruct((bsz, ne * rpe, LANES), F32),
        compiler_params=_cparams(("arbitrary",)),
        name="ec_select",
    )(a2)
    return slot.reshape(bsz, ne, n)


def _gather_kernel(slot_ref, h_ref, o_ref, *, cap):
    slot = slot_ref[0, 0]
    j = lax.broadcasted_iota(jnp.int32, (cap, slot.shape[1]), 0).astype(F32)
    onehot = jnp.where(slot == j, 1.0, 0.0).astype(BF16)
    o_ref[0, 0] = _dot(onehot, h_ref[0]).astype(BF16)


def _gather(slot4, h2, cap):
    bsz, ne, _, n = slot4.shape
    d = h2.shape[2]
    return pl.pallas_call(
        functools.partial(_gather_kernel, cap=cap),
        grid=(bsz, ne),
        in_specs=[pl.BlockSpec((1, 1, 1, n), lambda b, e: (b, e, 0, 0)),
                  pl.BlockSpec((1, n, d), lambda b, e: (b, 0, 0))],
        out_specs=pl.BlockSpec((1, 1, cap, d), lambda b, e: (b, e, 0, 0)),
        out_shape=jax.ShapeDtypeStruct((bsz, ne, cap, d), BF16),
        compiler_params=_cparams(("arbitrary", "arbitrary")),
        name="ec_gather",
    )(slot4, h2)


def _expert_kernel(x_ref, w1_ref, w3_ref, w2_ref, o_ref, acc_ref):
    f = pl.program_id(2)
    bt, _, cap, d = x_ref.shape
    x = x_ref[...].reshape(bt * cap, d)

    @pl.when(f == 0)
    def _():
        acc_ref[...] = jnp.zeros_like(acc_ref)

    w1, w3, w2 = w1_ref[0].astype(BF16), w3_ref[0].astype(BF16), w2_ref[0].astype(BF16)
    rows = bt * cap // EXPERT_ROW_TILES
    for r in range(EXPERT_ROW_TILES):
        sl = slice(r * rows, (r + 1) * rows)
        a = _dot(x[sl], w1)
        b = _dot(x[sl], w3)
        acc_ref[sl, :] += _dot((_silu(a) * b).astype(BF16), w2)

    @pl.when(f == pl.num_programs(2) - 1)
    def _():
        o_ref[...] = acc_ref[...].astype(BF16).reshape(bt, 1, cap, d)


def _experts(xg, w1, w3, w2, bt, ft):
    bsz, ne, cap, d = xg.shape
    ff = w1.shape[2]
    return pl.pallas_call(
        _expert_kernel,
        grid=(ne, bsz // bt, ff // ft),
        in_specs=[pl.BlockSpec((bt, 1, cap, d), lambda e, m, f: (m, e, 0, 0)),
                  pl.BlockSpec((1, d, ft), lambda e, m, f: (e, 0, f)),
                  pl.BlockSpec((1, d, ft), lambda e, m, f: (e, 0, f)),
                  pl.BlockSpec((1, ft, d), lambda e, m, f: (e, f, 0))],
        out_specs=pl.BlockSpec((bt, 1, cap, d), lambda e, m, f: (m, e, 0, 0)),
        out_shape=jax.ShapeDtypeStruct((bsz, ne, cap, d), BF16),
        scratch_shapes=[pltpu.VMEM((bt * cap, d), F32)],
        compiler_params=_cparams(("arbitrary", "arbitrary", "arbitrary")),
        name="ec_experts",
    )(xg, w1, w3, w2)


def _combine_kernel(y_ref, slot_ref, aff_ref, x1_ref, g2_ref, fg_ref, o_ref, *, cap):
    _, ne, _, d = y_ref.shape
    slot = slot_ref[0]
    aff = aff_ref[0]
    j = lax.broadcasted_iota(jnp.int32, (slot.shape[0], cap), 1).astype(F32)
    gated = jnp.concatenate([jnp.where(slot[:, e:e + 1] == j, aff[:, e:e + 1], 0.0).astype(BF16)
                             for e in range(ne)], axis=1)
    moe = _dot(gated, y_ref[0].reshape(ne * cap, d))
    x2 = x1_ref[0] + g2_ref[0] * moe
    o_ref[0] = x2 * lax.rsqrt(jnp.mean(x2 * x2, axis=-1, keepdims=True) + EPS) * fg_ref[...]


def _combine(y, slot_nt, aff_nt, x1, mods3, final_g, cap, tn):
    bsz, ne, _, d = y.shape
    n = x1.shape[1]
    return pl.pallas_call(
        functools.partial(_combine_kernel, cap=cap),
        grid=(bsz, n // tn),
        in_specs=[pl.BlockSpec((1, ne, cap, d), lambda b, t: (b, 0, 0, 0)),
                  pl.BlockSpec((1, tn, ne), lambda b, t: (b, t, 0)),
                  pl.BlockSpec((1, tn, ne), lambda b, t: (b, t, 0)),
                  pl.BlockSpec((1, tn, d), lambda b, t: (b, t, 0)),
                  pl.BlockSpec((1, 1, d), lambda b, t: (b, 0, 5)),
                  pl.BlockSpec((1, d), lambda b, t: (0, 0))],
        out_specs=pl.BlockSpec((1, tn, d), lambda b, t: (b, t, 0)),
        out_shape=jax.ShapeDtypeStruct((bsz, n, d), F32),
        compiler_params=_cparams(("arbitrary", "arbitrary")),
        name="ec_combine",
    )(y, slot_nt, aff_nt, x1, mods3, final_g.reshape(1, d))


def kernel(x, c, ctx, c_ctx, ada_w, ada_b, norm1_g, w_in, hy_conv_w, hy_conv_b, hy_pos_w1, hy_pos_b1, hy_freq1,
           hy_pos_w2, hy_pos_b2, hy_freq2, hy_pos_w3, hy_bias, hy_norm_g, hg_lb, hg_norm_g, w_out, norm2_g,
           router_w, exp_w1, exp_w3, exp_w2, final_g):
    bsz, seq, d = x.shape
    hyw = hy_norm_g.shape[1]
    hgw = w_in.shape[2] - 3 * hyw
    hg = hgw // 5
    assert hyw == hg, "column groups of the input projection are addressed with one width"
    cap = CAPACITY_FACTOR * seq // N_EXPERTS

    rows = -(-(bsz + 1) // 8) * 8
    s_rows = jnp.zeros((rows, d), F32).at[:bsz].set(c).at[bsz].set(c_ctx)
    mods3 = _ada(s_rows, ada_w[0], ada_b[0]).reshape(rows, 1, 6 * d)

    w_bf = w_in[0].astype(BF16)
    p, logf = _inproj_mix(x, mods3, norm1_g[0], w_bf, hg_lb, 1024, hg, 3, 5, 6, (hg // HG_HEADS) ** -0.5)
    w_ctx = w_bf[:, 3 * hyw + hg:3 * hyw + 4 * hg]
    pc = _inproj(ctx, mods3, lambda b: bsz, norm1_g[0], w_ctx, 0, 1, ctx.shape[1], 3 * hg)

    dft_c, dft_s = _dft_tables(seq)
    ct = LANES
    ptab, hm = _hyena_filters(seq, hy_pos_w1[0], hy_pos_b1[0], hy_freq1[0], hy_pos_w2[0], hy_pos_b2[0],
                              hy_freq2[0], hy_pos_w3[0], dft_c, dft_s, 2 * LANES)
    z = _hyena_conv(p, 0, p, 1, hy_conv_w[0], hy_conv_b[0], dft_c, dft_s, ptab, hm, hy_bias[0], 0, True, ct)
    y_hy = _hyena_conv(z, 0, p, 2, hy_conv_w[0], hy_conv_b[0], dft_c, dft_s, ptab, hm, hy_bias[0], 1, False, ct)

    y_hg = _hgrn(p, 3 * hyw // (hg // HG_HEADS), logf, pc, hg_lb, hg_norm_g[0], 2)

    w_o = w_out[0].astype(BF16)
    x1, h2, aff_t = _outproj(y_hy, y_hg, x, mods3, hy_norm_g[0], norm2_g[0], w_o[:hyw], w_o[hyw:],
                             router_w[0].T, 1024)

    slot = _select(aff_t, cap, 8)
    xg = _gather(slot.reshape(bsz, N_EXPERTS, 1, seq), h2, cap)
    y = _experts(xg, exp_w1[0], exp_w3[0], exp_w2[0], 8, 512)
    return _combine(y, jnp.swapaxes(slot, 1, 2), jnp.swapaxes(aff_t, 1, 2), x1, mods3, final_g, cap, 512)
```

```python
import functools
import math

import numpy as np
import jax
import jax.numpy as jnp
from jax import lax
from jax.experimental import pallas as pl
from jax.experimental.pallas import tpu as pltpu

F32 = jnp.float32
BF16 = jnp.bfloat16
HIGHEST = lax.Precision.HIGHEST

GRID_W = 64
HG_HEADS = 4
HY_BANDS = 16
HY_DECAY_TARGET = 1e-2
HY_FAST_DECAY = 0.3
HY_SLOW_DECAY = 1.5
HY_WINDOW_SHIFT = 0.05
N_EXPERTS = 16
CAPACITY_FACTOR = 2
EPS = 1e-6

SCAN_CHUNK = 64
SCAN_BLOCK = 16
FAST_BLOCK = 8
MAX_FAST_EXPONENT = 80.0
SCAN_UNROLL = 2
SCAN_HEADS_PER_STEP = 2
LANES = 128
VMEM_LIMIT = 56 * 1024 * 1024

INPROJ_COL_TILE = 1024
INPROJ_ROW_TILES = 4
FILTER_CH_TILE = 256
HYENA_CH_TILE = LANES
OUTPROJ_ROW_TILE = 1024
SELECT_SAMPLES = 8
EXPERT_SAMPLES = 8
EXPERT_FF_TILE = 512
EXPERT_ROW_TILES = 2
COMBINE_ROW_TILE = 512


def _cparams(sem):
    return pltpu.CompilerParams(dimension_semantics=sem, vmem_limit_bytes=VMEM_LIMIT)


def _dot(a, b):
    return jnp.dot(a, b, preferred_element_type=F32)


def _dot_nt(a, b):
    return lax.dot_general(a, b, (((1,), (1,)), ((), ())), preferred_element_type=F32)


def _dot_tn(a, b):
    return lax.dot_general(a, b, (((0,), (0,)), ((), ())), preferred_element_type=F32)


def _silu(x):
    return x * jax.nn.sigmoid(x)


def _ada_kernel(s_ref, w_ref, b_ref, o_ref):
    s = _silu(s_ref[...])
    o_ref[...] = jnp.dot(s, w_ref[...], preferred_element_type=F32, precision=HIGHEST) + b_ref[...]


def _ada(s_rows, ada_w, ada_b):
    rows, d = s_rows.shape
    n = ada_w.shape[1]
    tn = 1024
    return pl.pallas_call(
        _ada_kernel,
        grid=(n // tn,),
        in_specs=[pl.BlockSpec((rows, d), lambda j: (0, 0)),
                  pl.BlockSpec((d, tn), lambda j: (0, j)),
                  pl.BlockSpec((1, tn), lambda j: (0, j))],
        out_specs=pl.BlockSpec((rows, tn), lambda j: (0, j)),
        out_shape=jax.ShapeDtypeStruct((rows, n), F32),
        compiler_params=_cparams(("arbitrary",)),
        name="ada",
    )(s_rows, ada_w, ada_b.reshape(1, n))


def _inproj_kernel(x_ref, sh_ref, sc_ref, g_ref, w_ref, o_ref, h_scr):
    @pl.when(pl.program_id(2) == 0)
    def _():
        x = x_ref[0]
        y = x * lax.rsqrt(jnp.mean(x * x, axis=-1, keepdims=True) + EPS) * g_ref[...]
        h_scr[...] = (y * (1.0 + sc_ref[0]) + sh_ref[0]).astype(BF16)

    o_ref[0] = _dot(h_scr[...], w_ref[...])


def _inproj(x, mods3, mod_row, norm_g, w_bf, col_blk0, n_col_blk, tm, tn):
    bsz, seq, d = x.shape
    return pl.pallas_call(
        _inproj_kernel,
        grid=(bsz, seq // tm, n_col_blk),
        in_specs=[pl.BlockSpec((1, tm, d), lambda b, i, j: (b, i, 0)),
                  pl.BlockSpec((1, 1, d), lambda b, i, j: (mod_row(b), 0, 0)),
                  pl.BlockSpec((1, 1, d), lambda b, i, j: (mod_row(b), 0, 1)),
                  pl.BlockSpec((1, d), lambda b, i, j: (0, 0)),
                  pl.BlockSpec((d, tn), lambda b, i, j: (0, col_blk0 + j))],
        out_specs=pl.BlockSpec((1, tm, tn), lambda b, i, j: (b, i, j)),
        out_shape=jax.ShapeDtypeStruct((bsz, seq, n_col_blk * tn), F32),
        scratch_shapes=[pltpu.VMEM((tm, d), BF16)],
        compiler_params=_cparams(("arbitrary", "arbitrary", "arbitrary")),
        name="inproj",
    )(x, mods3, mods3, norm_g.reshape(1, d), w_bf)


def _gate_floor(lb_raw):
    e = jnp.exp(lb_raw - jnp.max(lb_raw, axis=0, keepdims=True))
    return e[0] / jnp.sum(e, axis=0)


def _forget_gate(z, lb):
    f = lb + (1.0 - lb) * jax.nn.sigmoid(z)
    return 1.0 - f, jnp.log(f)


def _inproj_mix_kernel(x_ref, sh_ref, sc_ref, g_ref, w_ref, lb_ref, o_ref, lf_ref, h_scr, *, n_col_blk, gw,
                       q_group, zf_group, zb_group, q_scale):
    j = pl.program_id(2)

    @pl.when(j == 0)
    def _():
        x = x_ref[0]
        y = x * lax.rsqrt(jnp.mean(x * x, axis=-1, keepdims=True) + EPS) * g_ref[...]
        h_scr[...] = (y * (1.0 + sc_ref[0]) + sh_ref[0]).astype(BF16)

    lb = _gate_floor(lb_ref[...])
    per_tile = o_ref.shape[2] // gw
    for jj in range(n_col_blk):
        @pl.when(j == jj)
        def _(jj=jj):
            rt = h_scr.shape[0] // INPROJ_ROW_TILES
            for loc in range(per_tile):
                grp = jj * per_tile + loc
                cols = slice(loc * gw, (loc + 1) * gw)
                for r in range(INPROJ_ROW_TILES):
                    rows = slice(r * rt, (r + 1) * rt)
                    part = _dot(h_scr[rows, :], w_ref[:, cols])
                    if grp == q_group:
                        part = _silu(part) * q_scale
                    elif grp in (zf_group, zb_group):
                        d = 0 if grp == zf_group else 1
                        part, lf_ref[0, rows, :] = _forget_gate(part, lb[d:d + 1, :])
                    o_ref[0, rows, cols] = part


def _inproj_mix(x, mods3, norm_g, w_bf, hg_lb, tn, gw, q_group, zf_group, zb_group, q_scale):
    bsz, seq, d = x.shape
    n_col_blk = w_bf.shape[1] // tn
    z_tile0 = zf_group * gw // tn
    assert zb_group * gw // tn == z_tile0 + 1 and zf_group * gw // tn != q_group * gw // tn
    return pl.pallas_call(
        functools.partial(_inproj_mix_kernel, n_col_blk=n_col_blk, gw=gw, q_group=q_group, zf_group=zf_group,
                          zb_group=zb_group, q_scale=q_scale),
        grid=(bsz, 1, n_col_blk),
        in_specs=[pl.BlockSpec((1, seq, d), lambda b, i, j: (b, 0, 0)),
                  pl.BlockSpec((1, 1, d), lambda b, i, j: (b, 0, 0)),
                  pl.BlockSpec((1, 1, d), lambda b, i, j: (b, 0, 1)),
                  pl.BlockSpec((1, d), lambda b, i, j: (0, 0)),
                  pl.BlockSpec((d, tn), lambda b, i, j: (0, j)),
                  pl.BlockSpec((hg_lb.shape[0], 2, gw), lambda b, i, j: (0, 0, 0))],
        out_specs=[pl.BlockSpec((1, seq, tn), lambda b, i, j: (b, 0, j)),
                   pl.BlockSpec((1, seq, gw), lambda b, i, j: (b, 0, jnp.maximum(j - z_tile0, 0)))],
        out_shape=[jax.ShapeDtypeStruct((bsz, seq, w_bf.shape[1]), F32),
                   jax.ShapeDtypeStruct((bsz, seq, 2 * gw), F32)],
        scratch_shapes=[pltpu.VMEM((seq, d), BF16)],
        compiler_params=_cparams(("arbitrary", "arbitrary", "arbitrary")),
        name="inproj_mix",
    )(x, mods3, mods3, norm_g.reshape(1, d), w_bf, hg_lb)


def _dft_tables(seq):
    half = seq // 2
    k = np.arange(half, dtype=np.int64)
    idx = (k[:, None] * k[None, :]) % seq
    ang = 2.0 * np.pi * np.arange(seq, dtype=np.float64) / seq
    cos_t = jnp.asarray(np.cos(ang)[idx].astype(np.float32))
    sin_t = jnp.asarray(np.sin(ang)[idx].astype(np.float32))
    return cos_t.astype(BF16), sin_t.astype(BF16)


def _pos_features(seq):
    t = np.linspace(0.0, 1.0, seq, dtype=np.float32)[:, None]
    w = (2.0 * math.pi * np.arange(seq, dtype=np.float32)[:, None] / seq).astype(np.float32)
    f = np.linspace(1e-4, HY_BANDS - 1, HY_BANDS, dtype=np.float32)[None, :]
    fw = (f * w).astype(np.float32)
    return np.concatenate([t, np.cos(fw), -np.sin(fw)], axis=-1).astype(np.float32)


def _filter_kernel(z_ref, w1_ref, b1_ref, f1_ref, w2_ref, b2_ref, f2_ref, w3f_ref, w3b_ref, t_ref, dl_ref,
                   sgn_ref, wk_ref, c2_ref, s2_ref, c_ref, s_ref, p_ref, hm_ref, hid_ref):
    half = c_ref.shape[0]
    n = 4 * half

    @pl.when((pl.program_id(0) == 0) & (pl.program_id(1) == 0))
    def _():
        for par in range(2):
            h = jnp.sin(f1_ref[...] * (jnp.dot(z_ref[par], w1_ref[...], preferred_element_type=F32,
                                               precision=HIGHEST) + b1_ref[...]))
            hid_ref[par] = jnp.sin(f2_ref[...] * (jnp.dot(h, w2_ref[...], preferred_element_type=F32,
                                                          precision=HIGHEST) + b2_ref[...]))

    taps = {}
    for par in range(2):
        h = hid_ref[par]
        window = jnp.exp(-t_ref[par] * dl_ref[...]) + HY_WINDOW_SHIFT
        taps["f", par] = jnp.dot(h, w3f_ref[...], preferred_element_type=F32, precision=HIGHEST) * window
        taps["b", par] = jnp.dot(h, w3b_ref[...], preferred_element_type=F32, precision=HIGHEST) * window
    row = lax.broadcasted_iota(jnp.int32, taps["b", 0].shape, 0)
    taps["b", 0] = jnp.where(row == 0, 0.0, taps["b", 0])
    inv = 1.0 / (sum(jnp.sum(jnp.abs(x), axis=0, keepdims=True) for x in taps.values()) + EPS)
    sgn = sgn_ref[...]
    spec = {}
    for key, x in taps.items():
        x = x * inv
        xb = x.astype(BF16)
        spec[key] = (_dot(c_ref[...], xb), _dot(s_ref[...], xb), jnp.sum(x * sgn, axis=0, keepdims=True))
    (fec, fes, fen), (foc, fos, fon) = spec["f", 0], spec["f", 1]
    (bec, bes, ben), (boc, bos, bon) = spec["b", 0], spec["b", 1]
    c2, s2 = c2_ref[...], s2_ref[...]
    wk = 2.0 * wk_ref[...]
    p_ref[0, 0] = (fec + bec) * wk
    p_ref[0, 1] = (bes - fes) * wk
    p_ref[0, 2] = (c2 * foc - s2 * fos + boc) * wk
    p_ref[0, 3] = (bos - c2 * fos - s2 * foc) * wk
    p_ref[0, 4] = (foc + c2 * boc - s2 * bos) * wk
    p_ref[0, 5] = (c2 * bos + s2 * boc - fos) * wk
    hm_ref[0] = jnp.concatenate([fen + ben, bon - fon], axis=0) * (2.0 / n)


def _hyena_filters(seq, pos_w1, pos_b1, freq1, pos_w2, pos_b2, freq2, pos_w3, dft_c, dft_s, ct):
    orders = 2
    half = seq // 2
    ch = pos_w3.shape[1] // (orders * 2)
    zt = _pos_features(seq)
    pdim = -(-zt.shape[1] // LANES) * LANES
    hid = -(-pos_w1.shape[1] // LANES) * LANES
    zt = np.pad(zt, ((0, 0), (0, pdim - zt.shape[1])))
    zt = np.stack([zt[0::2], zt[1::2]])
    pad2 = lambda a, r, c_: jnp.pad(a, ((0, r - a.shape[0]), (0, c_ - a.shape[1])))
    pos_w1 = pad2(pos_w1, pdim, hid)
    pos_w2 = pad2(pos_w2, hid, hid)
    pos_w3 = pad2(pos_w3, hid, pos_w3.shape[1])
    pos_b1, freq1, pos_b2, freq2 = (pad2(a.reshape(1, -1), 1, hid) for a in (pos_b1, freq1, pos_b2, freq2))
    t_col = np.linspace(0.0, 1.0, seq, dtype=np.float32)[:, None]
    t_col = np.stack([t_col[0::2], t_col[1::2]])
    min_decay = math.log(HY_DECAY_TARGET) / HY_SLOW_DECAY
    max_decay = math.log(HY_DECAY_TARGET) / HY_FAST_DECAY
    deltas = np.abs(np.linspace(min_decay, max_decay, ch, dtype=np.float32))[None, :]
    sgn = np.where(np.arange(half) % 2 == 0, 1.0, -1.0).astype(np.float32)[:, None]
    wk = np.full((half, 1), 2.0 / (2 * seq), np.float32)
    wk[0, 0] = 1.0 / (2 * seq)
    ang = 2.0 * np.pi * np.arange(half, dtype=np.float64)[:, None] / seq
    ncb = ch // ct
    const = lambda shape: pl.BlockSpec(shape, lambda o, c: (0,) * len(shape))
    return pl.pallas_call(
        _filter_kernel,
        grid=(orders, ncb),
        in_specs=[const((2, half, pdim)), const((pdim, hid)), const((1, hid)), const((1, hid)),
                  const((hid, hid)), const((1, hid)), const((1, hid)),
                  pl.BlockSpec((hid, ct), lambda o, c: (0, (2 * o) * ncb + c)),
                  pl.BlockSpec((hid, ct), lambda o, c: (0, (2 * o + 1) * ncb + c)),
                  const((2, half, 1)),
                  pl.BlockSpec((1, ct), lambda o, c: (0, c)),
                  const((half, 1)), const((half, 1)), const((half, 1)), const((half, 1)),
                  const((half, half)), const((half, half))],
        out_specs=[pl.BlockSpec((1, 6, half, ct), lambda o, c: (o, 0, 0, c)),
                   pl.BlockSpec((1, 2, ct), lambda o, c: (o, 0, c))],
        out_shape=[jax.ShapeDtypeStruct((orders, 6, half, ch), F32),
                   jax.ShapeDtypeStruct((orders, 2, ch), F32)],
        scratch_shapes=[pltpu.VMEM((2, half, hid), F32)],
        compiler_params=_cparams(("arbitrary", "arbitrary")),
        name="hyena_filter",
    )(jnp.asarray(zt), pos_w1, pos_b1, freq1, pos_w2, pos_b2, freq2, pos_w3, pos_w3, jnp.asarray(t_col),
      jnp.asarray(deltas), jnp.asarray(sgn), jnp.asarray(wk), jnp.asarray(np.cos(ang).astype(np.float32)),
      jnp.asarray(np.sin(ang).astype(np.float32)), dft_c, dft_s)


def _split_rows(x_ref):
    half = x_ref.shape[1] // 2
    return x_ref[0, pl.ds(0, half, stride=2), :], x_ref[0, pl.ds(1, half, stride=2), :]


def _short_conv_split(p_ref, w_ref, b_ref):
    pe, po = _split_rows(p_ref)
    half = pe.shape[0]
    pos = lax.broadcasted_iota(jnp.int32, pe.shape, 0) % (GRID_W // 2)
    before_even = jnp.where(pos == 0, 0.0, pltpu.roll(po, 1, axis=0))
    after_odd = jnp.where(pos == GRID_W // 2 - 1, 0.0, pltpu.roll(pe, half - 1, axis=0))
    w0, w1, w2 = w_ref[0:1, :], w_ref[1:2, :], w_ref[2:3, :]
    return (b_ref[...] + before_even * w0 + pe * w1 + po * w2,
            b_ref[...] + pe * w0 + po * w1 + after_odd * w2)


def _hyena_kernel(u_ref, g_ref, cwu_ref, cbu_ref, cwg_ref, cbg_ref, fwd_ref, inv_ref, p_ref, hm_ref,
                  bias_ref, sgn_ref, o_ref, spec_ref, *, conv_u, freq_tile):
    half = inv_ref.shape[0]
    ct = o_ref.shape[2]
    ue, uo = _short_conv_split(u_ref, cwu_ref, cbu_ref) if conv_u else _split_rows(u_ref)
    ub = jnp.concatenate([ue, uo], axis=1).astype(BF16)
    for ft in range(half // freq_tile):
        lo, hi = ft * freq_tile, (ft + 1) * freq_tile
        dc = _dot(fwd_ref[lo:hi, :], ub)
        ds = _dot(fwd_ref[half + lo:half + hi, :], ub)
        er, orr, es, os_ = dc[:, :ct], dc[:, ct:], ds[:, :ct], ds[:, ct:]
        p1r, p1i, p2r, p2i, p3r, p3i = (p_ref[0, i, lo:hi, :] for i in range(6))
        ar = er * p1r + es * p1i + orr * p2r + os_ * p2i
        br = er * p3r + es * p3i + orr * p1r + os_ * p1i
        ai = es * p1r - er * p1i + os_ * p2r - orr * p2i
        bi = es * p3r - er * p3i + os_ * p1r - orr * p1i
        spec_ref[lo:hi, :] = jnp.concatenate([ar, br], axis=1).astype(BF16)
        spec_ref[half + lo:half + hi, :] = jnp.concatenate([ai, bi], axis=1).astype(BF16)
    y = _dot(inv_ref[...], spec_ref[...])
    sgn = sgn_ref[...]
    en = jnp.sum(ue * sgn, axis=0, keepdims=True)
    on = jnp.sum(uo * sgn, axis=0, keepdims=True)
    hm_r, hm_i = hm_ref[0, 0:1, :], hm_ref[0, 1:2, :]
    bias = bias_ref[0]
    ge, go = _short_conv_split(g_ref, cwg_ref, cbg_ref)
    o_ref[0, pl.ds(0, half, stride=2), :] = ge * (y[:, :ct] + ue * bias + sgn * (en * hm_r + on * hm_i))
    o_ref[0, pl.ds(1, half, stride=2), :] = go * (y[:, ct:] + uo * bias + sgn * (on * hm_r - en * hm_i))


def _hyena_conv(u, u_col_blk, p_hy, gate_col_blk, conv_w, conv_b, dft_c, dft_s, ptab, hm, bias, order,
                conv_u, ct):
    bsz, seq, _ = p_hy.shape
    half = seq // 2
    ch = ptab.shape[3]
    ncb = ch // ct
    sgn = jnp.asarray(np.where(np.arange(half) % 2 == 0, 1.0, -1.0).astype(np.float32)[:, None])
    taps = conv_w.shape[0]
    cbias = conv_b.reshape(1, -1)
    const = lambda shape: pl.BlockSpec(shape, lambda c, b: (0,) * len(shape))
    return pl.pallas_call(
        functools.partial(_hyena_kernel, conv_u=conv_u, freq_tile=half // 4),
        grid=(ncb, bsz),
        in_specs=[pl.BlockSpec((1, seq, ct), lambda c, b: (b, 0, u_col_blk * ncb + c)),
                  pl.BlockSpec((1, seq, ct), lambda c, b: (b, 0, gate_col_blk * ncb + c)),
                  pl.BlockSpec((taps, ct), lambda c, b: (0, u_col_blk * ncb + c)),
                  pl.BlockSpec((1, ct), lambda c, b: (0, u_col_blk * ncb + c)),
                  pl.BlockSpec((taps, ct), lambda c, b: (0, gate_col_blk * ncb + c)),
                  pl.BlockSpec((1, ct), lambda c, b: (0, gate_col_blk * ncb + c)),
                  const((2 * half, half)), const((half, 2 * half)),
                  pl.BlockSpec((1, 6, half, ct), lambda c, b: (order, 0, 0, c)),
                  pl.BlockSpec((1, 2, ct), lambda c, b: (order, 0, c)),
                  pl.BlockSpec((1, 1, ct), lambda c, b: (order, 0, c)),
                  const((half, 1))],
        out_specs=pl.BlockSpec((1, seq, ct), lambda c, b: (b, 0, c)),
        out_shape=jax.ShapeDtypeStruct((bsz, seq, ch), F32),
        scratch_shapes=[pltpu.VMEM((2 * half, 2 * ct), BF16)],
        compiler_params=_cparams(("arbitrary", "arbitrary")),
        name=f"hyena_conv{order}",
    )(u, p_hy, conv_w, cbias, conv_w, cbias, jnp.concatenate([dft_c, dft_s], axis=0),
      jnp.concatenate([dft_c, dft_s], axis=1), ptab, hm, bias.reshape(bias.shape[0], 1, ch), sgn)


def _scan_consts():
    c, blk = SCAN_CHUNK, SCAN_BLOCK
    r = lax.broadcasted_iota(jnp.int32, (c, c), 0)
    s = lax.broadcasted_iota(jnp.int32, (c, c), 1)
    row = lax.broadcasted_iota(jnp.int32, (c, LANES), 0)
    fb = FAST_BLOCK

    def pair_mask(size, rev):
        same = (r // (2 * size)) == (s // (2 * size))
        hi_r, hi_s = r % (2 * size) >= size, s % (2 * size) >= size
        return same & ((~hi_r & hi_s) if rev else (hi_r & ~hi_s))

    return dict(
        cum_f=(s <= r).astype(F32), cum_b=(s >= r).astype(F32),
        m32_f=(r >= c // 2) & (s < c // 2), m32_b=(r < c // 2) & (s >= c // 2),
        m16_f=pair_mask(blk, False), m16_b=pair_mask(blk, True),
        m8_f=pair_mask(fb, False), m8_b=pair_mask(fb, True),
        md_f=(r // fb == s // fb) & (s <= r), md_b=(r // fb == s // fb) & (s >= r),
        sub=lax.broadcasted_iota(jnp.int32, (fb, LANES), 0),
        row=row, ones=jnp.ones((LANES, LANES), BF16))


def _state_update(st_ref, k, v, g, g_edge):
    kd = (k * jnp.exp(g_edge - g)).astype(BF16)
    st_ref[...] = st_ref[...] * jnp.exp(g_edge) + _dot_tn(v.astype(BF16), kd)


def _chunk_cumsum(logf, cst, reverse):
    return jnp.dot(cst["cum_b" if reverse else "cum_f"], logf, preferred_element_type=F32, precision=HIGHEST)


def _scan_chunk(st_ref, q, k, v, logf, cst, reverse):
    c, blk = SCAN_CHUNK, SCAN_BLOCK
    row = cst["row"]
    g = _chunk_cumsum(logf, cst, reverse)
    o = _dot_nt((q * jnp.exp(g)).astype(BF16), st_ref[...].astype(BF16))
    half = c // 2
    if not reverse:
        ref32 = g[half - 1:half, :]
        ref16 = jnp.where(row < half, g[blk - 1:blk, :], g[half + blk - 1:half + blk, :])
        t32, s32 = row >= half, row < half
        t16, s16 = row % (2 * blk) >= blk, row % (2 * blk) < blk
        g_edge = g[c - 1:c, :]
    else:
        ref32 = g[half:half + 1, :]
        ref16 = jnp.where(row < half, g[blk:blk + 1, :], g[half + blk:half + blk + 1, :])
        t32, s32 = row < half, row >= half
        t16, s16 = row % (2 * blk) < blk, row % (2 * blk) >= blk
        g_edge = g[0:1, :]
    a32 = jnp.where(t32, q * jnp.exp(jnp.minimum(g - ref32, 0.0)), 0.0).astype(BF16)
    b32 = jnp.where(s32, k * jnp.exp(jnp.minimum(ref32 - g, 0.0)), 0.0).astype(BF16)
    a16 = jnp.where(t16, q * jnp.exp(jnp.minimum(g - ref16, 0.0)), 0.0).astype(BF16)
    b16 = jnp.where(s16, k * jnp.exp(jnp.minimum(ref16 - g, 0.0)), 0.0).astype(BF16)
    att = (jnp.where(cst["m32_b" if reverse else "m32_f"], _dot_nt(a32, b32), 0.0)
           + jnp.where(cst["m16_b" if reverse else "m16_f"], _dot_nt(a16, b16), 0.0))
    o = o + _dot(att.astype(BF16), v.astype(BF16))
    pos = row % blk
    for d in range(blk):
        shift = d if not reverse else (c - d) % c
        if d == 0:
            ks, gs, vs = k, g, v
        else:
            ks = pltpu.roll(k, shift, axis=0)
            gs = pltpu.roll(g, shift, axis=0)
            vs = pltpu.roll(v, shift, axis=0)
        valid = (pos >= d) if not reverse else (pos + d < blk)
        w = jnp.where(valid, q * ks * jnp.exp(jnp.minimum(g - gs, 0.0)), 0.0)
        o = o + _dot(w.astype(BF16), cst["ones"]) * vs
    _state_update(st_ref, k, v, g, g_edge)
    return o


def _chunk_prepare(q, k, v, logf, cst, reverse):
    c, fb = SCAN_CHUNK, FAST_BLOCK
    nb = c // fb
    d = "b" if reverse else "f"
    sub = cst["sub"]
    blocks = lambda x: [x[fb * j:fb * (j + 1), :] for j in range(nb)]
    qs, ks = blocks(q), blocks(k)
    ps = []
    for p in blocks(logf):
        for s in (1, 2, 4):
            if reverse:
                p = p + jnp.where(sub + s < fb, pltpu.roll(p, fb - s, axis=0), 0.0)
            else:
                p = p + jnp.where(sub >= s, pltpu.roll(p, s, axis=0), 0.0)
        ps.append(p)
    last = 0 if reverse else fb - 1
    tot = [jnp.broadcast_to(p[last:last + 1, :], p.shape) for p in ps]
    decay = [jnp.exp(t) for t in tot]
    memo = {}

    def span(lo, hi):
        if hi <= lo:
            return None
        if (lo, hi) not in memo:
            if hi == nb and lo > 0:
                inner, last = span(lo + 1, hi), decay[lo]
            else:
                inner, last = span(lo, hi - 1), decay[hi - 1]
            memo[lo, hi] = last if inner is None else inner * last
        return memo[lo, hi]

    scale = lambda x, f: x if f is None else x * f
    qe = [qs[j] * jnp.exp(ps[j]) for j in range(nb)]
    kb = [ks[j] * jnp.exp(tot[j] - ps[j]) for j in range(nb)]
    zero = jnp.zeros_like(tot[0])
    cat = lambda parts: jnp.concatenate(parts, axis=0).astype(BF16)

    def level(n):
        a, b = [], []
        for j in range(nb):
            mid = (j // (2 * n)) * 2 * n + n
            later = j >= mid
            if later != reverse:
                a.append(scale(qe[j], span(j + 1, mid) if reverse else span(mid, j)))
                b.append(zero)
            else:
                a.append(zero)
                b.append(scale(kb[j], span(mid, j) if reverse else span(j + 1, mid)))
        return _dot_nt(cat(a), cat(b))

    bd = cat([ks[j] * jnp.exp(-ps[j]) for j in range(nb)])
    att = (level(nb // 2)
           + jnp.where(cst["m16_" + d], level(nb // 4), 0.0)
           + jnp.where(cst["m8_" + d], level(nb // 8), 0.0)
           + jnp.where(cst["md_" + d], _dot_nt(cat(qe), bd), 0.0))
    before = lambda j: span(j + 1, nb) if reverse else span(0, j)
    after = lambda j: span(0, j) if reverse else span(j + 1, nb)
    qd = cat([scale(qe[j], before(j)) for j in range(nb)])
    kd = cat([scale(kb[j], after(j)) for j in range(nb)])
    vb = v.astype(BF16)
    return att.astype(BF16), qd, vb, _dot_tn(vb, kd), span(0, nb)[0:1, :]


def _chunk_finish(st_ref, prepared):
    att, qd, vb, kv, decay = prepared
    o = _dot_nt(qd, st_ref[...].astype(BF16)) + _dot(att, vb)
    st_ref[...] = st_ref[...] * decay + kv
    return o


def _hgrn_kernel(q_ref, i_ref, kf_ref, kb_ref, g_ref, lf_ref, lbw_ref, ic_ref, zfc_ref, zbc_ref, lb_ref, ng_ref,
                 o_ref, of_ref, ob_ref, st_ref):
    seq = q_ref.shape[1]
    ctx_len = ic_ref.shape[1]
    hd = ng_ref.shape[1]
    heads = q_ref.shape[2] // hd
    c = SCAN_CHUNK
    n_chunks, n_ctx = seq // c, ctx_len // c
    cst = _scan_consts()
    lb = _gate_floor(lb_ref[...])
    lanes = [slice(h * hd, (h + 1) * hd) for h in range(heads)]
    st_f = [st_ref.at[2 * h] for h in range(heads)]
    st_b = [st_ref.at[2 * h + 1] for h in range(heads)]

    st_ref[...] = jnp.zeros_like(st_ref)

    def ctx_body(j, carry):
        rf = pl.multiple_of(j * c, c)
        rb = pl.multiple_of((n_ctx - 1 - j) * c, c)
        for h, ln in enumerate(lanes):
            for st, z_ref, r, lb_row, rev in ((st_f[h], zfc_ref, rf, lb[0:1, ln], False),
                                              (st_b[h], zbc_ref, rb, lb[1:2, ln], True)):
                k, logf = _forget_gate(z_ref[0, pl.ds(r, c), ln], lb_row)
                *_, kv, decay = _chunk_prepare(jnp.zeros_like(k), k, ic_ref[0, pl.ds(r, c), ln], logf, cst, rev)
                st[...] = st[...] * decay + kv
        return carry

    lax.fori_loop(0, n_ctx, ctx_body, 0)

    def rows(j):
        if isinstance(j, int):
            return j * c, (n_chunks - 1 - j) * c
        return pl.multiple_of(j * c, c), pl.multiple_of((n_chunks - 1 - j) * c, c)

    def operands(r, ln, k_ref, logf_ref):
        sl = (0, pl.ds(r, c), ln)
        return q_ref[sl], k_ref[sl], i_ref[sl], logf_ref[sl]

    def direct_body(j, carry):
        rf, rb = rows(j)
        for h, ln in enumerate(lanes):
            of_ref[pl.ds(rf, c), ln] = _scan_chunk(st_f[h], *operands(rf, ln, kf_ref, lf_ref), cst, False)
            ob_ref[pl.ds(rb, c), ln] = _scan_chunk(st_b[h], *operands(rb, ln, kb_ref, lbw_ref), cst, True)
        return carry

    def prepare(j):
        rf, rb = rows(j)
        out = []
        for ln in lanes:
            out.append(_chunk_prepare(*operands(rf, ln, kf_ref, lf_ref), cst, False))
            out.append(_chunk_prepare(*operands(rb, ln, kb_ref, lbw_ref), cst, True))
        return tuple(out)

    def finish(j, prepared):
        rf, rb = rows(j)
        for h, ln in enumerate(lanes):
            of_ref[pl.ds(rf, c), ln] = _chunk_finish(st_f[h], prepared[2 * h])
            ob_ref[pl.ds(rb, c), ln] = _chunk_finish(st_b[h], prepared[2 * h + 1])

    def factored_body(j, prepared):
        upcoming = prepare(j + 1)
        finish(j, prepared)
        return upcoming

    bounded = jnp.min(lb) >= math.exp(-MAX_FAST_EXPONENT / FAST_BLOCK)

    @pl.when(bounded)
    def _():
        looped = (n_chunks - 1) // SCAN_UNROLL * SCAN_UNROLL
        prepared = lax.fori_loop(0, looped, factored_body, prepare(0), unroll=SCAN_UNROLL)
        for j in range(looped, n_chunks - 1):
            prepared = factored_body(j, prepared)
        finish(n_chunks - 1, prepared)

    @pl.when(jnp.logical_not(bounded))
    def _():
        lax.fori_loop(0, n_chunks, direct_body, 0)

    for ln in lanes:
        o = of_ref[:, ln] + ob_ref[:, ln]
        y = o * lax.rsqrt(jnp.mean(o * o, axis=-1, keepdims=True) + EPS) * ng_ref[...]
        o_ref[0, :, ln] = y * _silu(g_ref[0, :, ln])


def _hgrn(p, col0, logf, pc, hg_lb, norm_g, heads_per_step):
    bsz, seq, _ = p.shape
    ctx_len = pc.shape[1]
    hd = norm_g.shape[0]
    w = heads_per_step * hd
    steps = HG_HEADS // heads_per_step
    lat = lambda grp: pl.BlockSpec((1, seq, w), lambda b, h: (b, 0, (col0 + grp * HG_HEADS) // heads_per_step + h))
    lgf = lambda d: pl.BlockSpec((1, seq, w), lambda b, h: (b, 0, d * steps + h))
    ctx = lambda grp: pl.BlockSpec((1, ctx_len, w), lambda b, h: (b, 0, grp * steps + h))
    return pl.pallas_call(
        _hgrn_kernel,
        grid=(bsz, steps),
        in_specs=[lat(0), lat(1), lat(2), lat(3), lat(4), lgf(0), lgf(1), ctx(0), ctx(1), ctx(2),
                  pl.BlockSpec((hg_lb.shape[0], 2, w), lambda b, h: (0, 0, h)),
                  pl.BlockSpec((1, hd), lambda b, h: (0, 0))],
        out_specs=pl.BlockSpec((1, seq, w), lambda b, h: (b, 0, h)),
        out_shape=jax.ShapeDtypeStruct((bsz, seq, HG_HEADS * hd), F32),
        scratch_shapes=[pltpu.VMEM((seq, w), F32), pltpu.VMEM((seq, w), F32),
                        pltpu.VMEM((2 * heads_per_step, hd, hd), F32)],
        compiler_params=_cparams(("arbitrary", "arbitrary")),
        name="hgrn_scan",
    )(p, p, p, p, p, logf, logf, pc, pc, pc, hg_lb, norm_g.reshape(1, hd))


def _outproj_kernel(yhy_ref, yhg_ref, x_ref, g1_ref, sh2_ref, sc2_ref, hyg_ref, n2g_ref, wt_ref, wb_ref, rw_ref,
                    x1_ref, h2_ref, aff_ref):
    yhy = yhy_ref[0]
    yn = yhy * lax.rsqrt(jnp.mean(yhy * yhy, axis=-1, keepdims=True) + EPS) * hyg_ref[...]
    mix = _dot(yn.astype(BF16), wt_ref[...]) + _dot(yhg_ref[0].astype(BF16), wb_ref[...])
    x1 = x_ref[0] + g1_ref[0] * mix
    x1_ref[0] = x1
    h2 = (x1 * lax.rsqrt(jnp.mean(x1 * x1, axis=-1, keepdims=True) + EPS) * n2g_ref[...]
          * (1.0 + sc2_ref[0]) + sh2_ref[0])
    h_hi = h2.astype(BF16)
    h2_ref[0] = h_hi
    h_lo = (h2 - h_hi.astype(F32)).astype(BF16)
    rw = rw_ref[...]
    r_hi = rw.astype(BF16)
    r_lo = (rw - r_hi.astype(F32)).astype(BF16)
    ne = rw.shape[0]
    both = _dot_nt(jnp.concatenate([r_hi, r_lo], axis=0), h_hi)
    logits = both[:ne] + both[ne:] + _dot_nt(r_hi, h_lo)
    ex = jnp.exp(logits - jnp.max(logits, axis=0, keepdims=True))
    aff_ref[0] = ex / jnp.sum(ex, axis=0, keepdims=True)


def _outproj(y_hy, y_hg, x, mods3, hy_norm_g, norm2_g, w_top, w_bot, router_wt, tm):
    bsz, seq, d = x.shape
    wh = y_hy.shape[2]
    ne = router_wt.shape[0]
    mod = lambda k: pl.BlockSpec((1, 1, d), lambda b, i: (b, 0, k))
    const = lambda shape: pl.BlockSpec(shape, lambda b, i: (0,) * len(shape))
    return pl.pallas_call(
        _outproj_kernel,
        grid=(bsz, seq // tm),
        in_specs=[pl.BlockSpec((1, tm, wh), lambda b, i: (b, i, 0)),
                  pl.BlockSpec((1, tm, wh), lambda b, i: (b, i, 0)),
                  pl.BlockSpec((1, tm, d), lambda b, i: (b, i, 0)),
                  mod(2), mod(3), mod(4),
                  const((1, wh)), const((1, d)), const((wh, d)), const((wh, d)), const((ne, d))],
        out_specs=[pl.BlockSpec((1, tm, d), lambda b, i: (b, i, 0)),
                   pl.BlockSpec((1, tm, d), lambda b, i: (b, i, 0)),
                   pl.BlockSpec((1, ne, tm), lambda b, i: (b, 0, i))],
        out_shape=[jax.ShapeDtypeStruct((bsz, seq, d), F32),
                   jax.ShapeDtypeStruct((bsz, seq, d), BF16),
                   jax.ShapeDtypeStruct((bsz, ne, seq), F32)],
        compiler_params=_cparams(("arbitrary", "arbitrary")),
        name="outproj_router",
    )(y_hy, y_hg, x, mods3, mods3, mods3, hy_norm_g.reshape(1, wh), norm2_g.reshape(1, d), w_top, w_bot, router_wt)


def _select_kernel(a_ref, slot_ref, *, cap, rows_per_expert):
    nr = a_ref.shape[1]
    affs = [a_ref[i] for i in range(a_ref.shape[0])]
    r = lax.broadcasted_iota(jnp.int32, (nr, nr), 0)
    s = lax.broadcasted_iota(jnp.int32, (nr, nr), 1)
    same = (r // rows_per_expert) == (s // rows_per_expert)
    grp = same.astype(BF16)
    before = (same & (s < r)).astype(BF16)
    li = lax.broadcasted_iota(jnp.int32, (LANES, LANES), 0)
    lj = lax.broadcasted_iota(jnp.int32, (LANES, LANES), 1)
    ones = jnp.ones((LANES, LANES), BF16)
    strict = (li < lj).astype(BF16)

    def row_total(ind):
        return _dot(ind, ones).astype(BF16)

    def expert_total(ind):
        return _dot(grp, row_total(ind))

    def prefix(ind):
        return _dot(ind, strict) + _dot(before, row_total(ind))

    ne = nr // rows_per_expert

    def count_at_least(a, cand):
        ind = jnp.where(a >= cand, 1.0, 0.0).reshape(ne, rows_per_expert, LANES)
        tot = jnp.sum(jnp.sum(ind, axis=1, keepdims=True), axis=2, keepdims=True)
        return jnp.broadcast_to(tot, ind.shape).reshape(nr, LANES)

    def search(i, thrs):
        bit = jnp.left_shift(jnp.int32(1), 30 - i)
        out = []
        for a, t in zip(affs, thrs):
            cand = t | bit
            out.append(jnp.where(count_at_least(a, pltpu.bitcast(cand, F32)) >= cap, cand, t))
        return tuple(out)

    thrs = lax.fori_loop(0, 31, search, tuple(jnp.zeros((nr, LANES), jnp.int32) for _ in affs))
    for i, (a, thr) in enumerate(zip(affs, thrs)):
        thr_f = pltpu.bitcast(thr, F32)
        gt = a > thr_f
        eq = a == thr_f
        need = cap - expert_total(jnp.where(gt, 1.0, 0.0).astype(BF16))
        eq_rank = prefix(jnp.where(eq, 1.0, 0.0).astype(BF16))
        sel = gt | (eq & (eq_rank < need))
        rank = prefix(jnp.where(sel, 1.0, 0.0).astype(BF16))
        slot_ref[i] = jnp.where(sel, rank, -1.0)


def _select(aff_t, cap, bt):
    bsz, ne, n = aff_t.shape
    rpe = n // LANES
    a2 = aff_t.reshape(bsz, ne * rpe, LANES)
    slot = pl.pallas_call(
        functools.partial(_select_kernel, cap=cap, rows_per_expert=rpe),
        grid=(bsz // bt,),
        in_specs=[pl.BlockSpec((bt, ne * rpe, LANES), lambda b: (b, 0, 0))],
        out_specs=pl.BlockSpec((bt, ne * rpe, LANES), lambda b: (b, 0, 0)),
        out_shape=jax.ShapeDtypeStruct((bsz, ne * rpe, LANES), F32),
        compiler_params=_cparams(("arbitrary",)),
        name="ec_select",
    )(a2)
    return slot.reshape(bsz, ne, n)


def _gather_kernel(slot_ref, h_ref, o_ref, *, cap):
    slot = slot_ref[0, 0]
    j = lax.broadcasted_iota(jnp.int32, (cap, slot.shape[1]), 0).astype(F32)
    onehot = jnp.where(slot == j, 1.0, 0.0).astype(BF16)
    o_ref[0, 0] = _dot(onehot, h_ref[0]).astype(BF16)


def _gather(slot4, h2, cap):
    bsz, ne, _, n = slot4.shape
    d = h2.shape[2]
    return pl.pallas_call(
        functools.partial(_gather_kernel, cap=cap),
        grid=(bsz, ne),
        in_specs=[pl.BlockSpec((1, 1, 1, n), lambda b, e: (b, e, 0, 0)),
                  pl.BlockSpec((1, n, d), lambda b, e: (b, 0, 0))],
        out_specs=pl.BlockSpec((1, 1, cap, d), lambda b, e: (b, e, 0, 0)),
        out_shape=jax.ShapeDtypeStruct((bsz, ne, cap, d), BF16),
        compiler_params=_cparams(("arbitrary", "arbitrary")),
        name="ec_gather",
    )(slot4, h2)


def _expert_kernel(x_ref, w1_ref, w3_ref, w2_ref, o_ref, acc_ref):
    f = pl.program_id(2)
    bt, _, cap, d = x_ref.shape
    x = x_ref[...].reshape(bt * cap, d)

    @pl.when(f == 0)
    def _():
        acc_ref[...] = jnp.zeros_like(acc_ref)

    w1, w3, w2 = w1_ref[0].astype(BF16), w3_ref[0].astype(BF16), w2_ref[0].astype(BF16)
    rows = bt * cap // EXPERT_ROW_TILES
    for r in range(EXPERT_ROW_TILES):
        sl = slice(r * rows, (r + 1) * rows)
        a = _dot(x[sl], w1)
        b = _dot(x[sl], w3)
        acc_ref[sl, :] += _dot((_silu(a) * b).astype(BF16), w2)

    @pl.when(f == pl.num_programs(2) - 1)
    def _():
        o_ref[...] = acc_ref[...].astype(BF16).reshape(bt, 1, cap, d)


def _experts(xg, w1, w3, w2, bt, ft):
    bsz, ne, cap, d = xg.shape
    ff = w1.shape[2]
    return pl.pallas_call(
        _expert_kernel,
        grid=(ne, bsz // bt, ff // ft),
        in_specs=[pl.BlockSpec((bt, 1, cap, d), lambda e, m, f: (m, e, 0, 0)),
                  pl.BlockSpec((1, d, ft), lambda e, m, f: (e, 0, f)),
                  pl.BlockSpec((1, d, ft), lambda e, m, f: (e, 0, f)),
                  pl.BlockSpec((1, ft, d), lambda e, m, f: (e, f, 0))],
        out_specs=pl.BlockSpec((bt, 1, cap, d), lambda e, m, f: (m, e, 0, 0)),
        out_shape=jax.ShapeDtypeStruct((bsz, ne, cap, d), BF16),
        scratch_shapes=[pltpu.VMEM((bt * cap, d), F32)],
        compiler_params=_cparams(("arbitrary", "arbitrary", "arbitrary")),
        name="ec_experts",
    )(xg, w1, w3, w2)


def _combine_kernel(y_ref, slot_ref, aff_ref, x1_ref, g2_ref, fg_ref, o_ref, *, cap):
    _, ne, _, d = y_ref.shape
    slot = slot_ref[0]
    aff = aff_ref[0]
    j = lax.broadcasted_iota(jnp.int32, (slot.shape[0], cap), 1).astype(F32)
    gated = jnp.concatenate([jnp.where(slot[:, e:e + 1] == j, aff[:, e:e + 1], 0.0).astype(BF16)
                             for e in range(ne)], axis=1)
    moe = _dot(gated, y_ref[0].reshape(ne * cap, d))
    x2 = x1_ref[0] + g2_ref[0] * moe
    o_ref[0] = x2 * lax.rsqrt(jnp.mean(x2 * x2, axis=-1, keepdims=True) + EPS) * fg_ref[...]


def _combine(y, slot_nt, aff_nt, x1, mods3, final_g, cap, tn):
    bsz, ne, _, d = y.shape
    n = x1.shape[1]
    return pl.pallas_call(
        functools.partial(_combine_kernel, cap=cap),
        grid=(bsz, n // tn),
        in_specs=[pl.BlockSpec((1, ne, cap, d), lambda b, t: (b, 0, 0, 0)),
                  pl.BlockSpec((1, tn, ne), lambda b, t: (b, t, 0)),
                  pl.BlockSpec((1, tn, ne), lambda b, t: (b, t, 0)),
                  pl.BlockSpec((1, tn, d), lambda b, t: (b, t, 0)),
                  pl.BlockSpec((1, 1, d), lambda b, t: (b, 0, 5)),
                  pl.BlockSpec((1, d), lambda b, t: (0, 0))],
        out_specs=pl.BlockSpec((1, tn, d), lambda b, t: (b, t, 0)),
        out_shape=jax.ShapeDtypeStruct((bsz, n, d), F32),
        compiler_params=_cparams(("arbitrary", "arbitrary")),
        name="ec_combine",
    )(y, slot_nt, aff_nt, x1, mods3, final_g.reshape(1, d))


def kernel(x, c, ctx, c_ctx, ada_w, ada_b, norm1_g, w_in, hy_conv_w, hy_conv_b, hy_pos_w1, hy_pos_b1, hy_freq1,
           hy_pos_w2, hy_pos_b2, hy_freq2, hy_pos_w3, hy_bias, hy_norm_g, hg_lb, hg_norm_g, w_out, norm2_g,
           router_w, exp_w1, exp_w3, exp_w2, final_g):
    bsz, seq, d = x.shape
    hyw = hy_norm_g.shape[1]
    hgw = w_in.shape[2] - 3 * hyw
    hg = hgw // 5
    assert hyw == hg, "column groups of the input projection are addressed with one width"
    cap = CAPACITY_FACTOR * seq // N_EXPERTS

    rows = -(-(bsz + 1) // 8) * 8
    s_rows = jnp.zeros((rows, d), F32).at[:bsz].set(c).at[bsz].set(c_ctx)
    mods3 = _ada(s_rows, ada_w[0], ada_b[0]).reshape(rows, 1, 6 * d)

    w_bf = w_in[0].astype(BF16)
    p, logf = _inproj_mix(x, mods3, norm1_g[0], w_bf, hg_lb, INPROJ_COL_TILE, hg, 3, 5, 6, (hg // HG_HEADS) ** -0.5)
    w_ctx = w_bf[:, 3 * hyw + hg:3 * hyw + 4 * hg]
    pc = _inproj(ctx, mods3, lambda b: bsz, norm1_g[0], w_ctx, 0, 1, ctx.shape[1], 3 * hg)

    dft_c, dft_s = _dft_tables(seq)
    ct = HYENA_CH_TILE
    ptab, hm = _hyena_filters(seq, hy_pos_w1[0], hy_pos_b1[0], hy_freq1[0], hy_pos_w2[0], hy_pos_b2[0],
                              hy_freq2[0], hy_pos_w3[0], dft_c, dft_s, FILTER_CH_TILE)
    z = _hyena_conv(p, 0, p, 1, hy_conv_w[0], hy_conv_b[0], dft_c, dft_s, ptab, hm, hy_bias[0], 0, True, ct)
    y_hy = _hyena_conv(z, 0, p, 2, hy_conv_w[0], hy_conv_b[0], dft_c, dft_s, ptab, hm, hy_bias[0], 1, False, ct)

    y_hg = _hgrn(p, 3 * hyw // (hg // HG_HEADS), logf, pc, hg_lb, hg_norm_g[0], SCAN_HEADS_PER_STEP)

    w_o = w_out[0].astype(BF16)
    x1, h2, aff_t = _outproj(y_hy, y_hg, x, mods3, hy_norm_g[0], norm2_g[0], w_o[:hyw], w_o[hyw:],
                             router_w[0].T, OUTPROJ_ROW_TILE)

    slot = _select(aff_t, cap, SELECT_SAMPLES)
    xg = _gather(slot.reshape(bsz, N_EXPERTS, 1, seq), h2, cap)
    y = _experts(xg, exp_w1[0], exp_w3[0], exp_w2[0], EXPERT_SAMPLES, EXPERT_FF_TILE)
    return _combine(y, jnp.swapaxes(slot, 1, 2), jnp.swapaxes(aff_t, 1, 2), x1, mods3, final_g, cap,
                    COMBINE_ROW_TILE)
```

```python
import functools
import math

import numpy as np
import jax
import jax.numpy as jnp
from jax import lax
from jax.experimental import pallas as pl
from jax.experimental.pallas import tpu as pltpu

F32 = jnp.float32
BF16 = jnp.bfloat16
HIGHEST = lax.Precision.HIGHEST

GRID_W = 64
HG_HEADS = 4
HY_BANDS = 16
HY_DECAY_TARGET = 1e-2
HY_FAST_DECAY = 0.3
HY_SLOW_DECAY = 1.5
HY_WINDOW_SHIFT = 0.05
N_EXPERTS = 16
CAPACITY_FACTOR = 2
EPS = 1e-6

SCAN_CHUNK = 64
SCAN_BLOCK = 16
FAST_BLOCK = 8
MAX_FAST_EXPONENT = 80.0
SCAN_UNROLL = 2
SCAN_HEADS_PER_STEP = 2
LANES = 128
VMEM_LIMIT = 56 * 1024 * 1024

INPROJ_COL_TILE = 1024
INPROJ_ROW_TILES = 16
FILTER_CH_TILE = 256
HYENA_CH_TILE = LANES
OUTPROJ_ROW_TILE = 1024
SELECT_SAMPLES = 8
EXPERT_SAMPLES = 8
EXPERT_FF_TILE = 512
EXPERT_ROW_TILES = 2
COMBINE_ROW_TILE = 512


def _cparams(sem):
    return pltpu.CompilerParams(dimension_semantics=sem, vmem_limit_bytes=VMEM_LIMIT)


def _dot(a, b):
    return jnp.dot(a, b, preferred_element_type=F32)


def _dot_nt(a, b):
    return lax.dot_general(a, b, (((1,), (1,)), ((), ())), preferred_element_type=F32)


def _dot_tn(a, b):
    return lax.dot_general(a, b, (((0,), (0,)), ((), ())), preferred_element_type=F32)


def _silu(x):
    return x * jax.nn.sigmoid(x)


def _ada_kernel(s_ref, w_ref, b_ref, o_ref):
    s = _silu(s_ref[...])
    o_ref[...] = jnp.dot(s, w_ref[...], preferred_element_type=F32, precision=HIGHEST) + b_ref[...]


def _ada(s_rows, ada_w, ada_b):
    rows, d = s_rows.shape
    n = ada_w.shape[1]
    tn = 1024
    return pl.pallas_call(
        _ada_kernel,
        grid=(n // tn,),
        in_specs=[pl.BlockSpec((rows, d), lambda j: (0, 0)),
                  pl.BlockSpec((d, tn), lambda j: (0, j)),
                  pl.BlockSpec((1, tn), lambda j: (0, j))],
        out_specs=pl.BlockSpec((rows, tn), lambda j: (0, j)),
        out_shape=jax.ShapeDtypeStruct((rows, n), F32),
        compiler_params=_cparams(("arbitrary",)),
        name="ada",
    )(s_rows, ada_w, ada_b.reshape(1, n))


def _inproj_kernel(x_ref, sh_ref, sc_ref, g_ref, w_ref, o_ref, h_scr):
    @pl.when(pl.program_id(2) == 0)
    def _():
        x = x_ref[0]
        y = x * lax.rsqrt(jnp.mean(x * x, axis=-1, keepdims=True) + EPS) * g_ref[...]
        h_scr[...] = (y * (1.0 + sc_ref[0]) + sh_ref[0]).astype(BF16)

    o_ref[0] = _dot(h_scr[...], w_ref[...])


def _inproj(x, mods3, mod_row, norm_g, w_bf, col_blk0, n_col_blk, tm, tn):
    bsz, seq, d = x.shape
    return pl.pallas_call(
        _inproj_kernel,
        grid=(bsz, seq // tm, n_col_blk),
        in_specs=[pl.BlockSpec((1, tm, d), lambda b, i, j: (b, i, 0)),
                  pl.BlockSpec((1, 1, d), lambda b, i, j: (mod_row(b), 0, 0)),
                  pl.BlockSpec((1, 1, d), lambda b, i, j: (mod_row(b), 0, 1)),
                  pl.BlockSpec((1, d), lambda b, i, j: (0, 0)),
                  pl.BlockSpec((d, tn), lambda b, i, j: (0, col_blk0 + j))],
        out_specs=pl.BlockSpec((1, tm, tn), lambda b, i, j: (b, i, j)),
        out_shape=jax.ShapeDtypeStruct((bsz, seq, n_col_blk * tn), F32),
        scratch_shapes=[pltpu.VMEM((tm, d), BF16)],
        compiler_params=_cparams(("arbitrary", "arbitrary", "arbitrary")),
        name="inproj",
    )(x, mods3, mods3, norm_g.reshape(1, d), w_bf)


def _gate_floor(lb_raw):
    e = jnp.exp(lb_raw - jnp.max(lb_raw, axis=0, keepdims=True))
    return e[0] / jnp.sum(e, axis=0)


def _forget_gate(z, lb):
    f = lb + (1.0 - lb) * jax.nn.sigmoid(z)
    return 1.0 - f, jnp.log(f)


def _inproj_mix_kernel(x_ref, sh_ref, sc_ref, g_ref, w_ref, lb_ref, o_ref, lf_ref, h_scr, *, n_col_blk, gw,
                       q_group, zf_group, zb_group, q_scale):
    j = pl.program_id(2)
    lb = _gate_floor(lb_ref[...])
    per_tile = o_ref.shape[2] // gw
    for jj in range(n_col_blk):
        @pl.when(j == jj)
        def _(jj=jj):
            rt = h_scr.shape[0] // INPROJ_ROW_TILES
            for loc in range(per_tile):
                grp = jj * per_tile + loc
                cols = slice(loc * gw, (loc + 1) * gw)
                for r in range(INPROJ_ROW_TILES):
                    rows = slice(r * rt, (r + 1) * rt)
                    if jj == 0 and loc == 0:
                        x = x_ref[0, rows, :]
                        y = x * lax.rsqrt(jnp.mean(x * x, axis=-1, keepdims=True) + EPS) * g_ref[...]
                        h_scr[rows, :] = (y * (1.0 + sc_ref[0]) + sh_ref[0]).astype(BF16)
                    part = _dot(h_scr[rows, :], w_ref[:, cols])
                    if grp == q_group:
                        part = _silu(part) * q_scale
                    elif grp in (zf_group, zb_group):
                        d = 0 if grp == zf_group else 1
                        part, lf_ref[0, rows, :] = _forget_gate(part, lb[d:d + 1, :])
                    o_ref[0, rows, cols] = part


def _inproj_mix(x, mods3, norm_g, w_bf, hg_lb, tn, gw, q_group, zf_group, zb_group, q_scale):
    bsz, seq, d = x.shape
    n_col_blk = w_bf.shape[1] // tn
    z_tile0 = zf_group * gw // tn
    assert zb_group * gw // tn == z_tile0 + 1 and zf_group * gw // tn != q_group * gw // tn
    return pl.pallas_call(
        functools.partial(_inproj_mix_kernel, n_col_blk=n_col_blk, gw=gw, q_group=q_group, zf_group=zf_group,
                          zb_group=zb_group, q_scale=q_scale),
        grid=(bsz, 1, n_col_blk),
        in_specs=[pl.BlockSpec((1, seq, d), lambda b, i, j: (b, 0, 0)),
                  pl.BlockSpec((1, 1, d), lambda b, i, j: (b, 0, 0)),
                  pl.BlockSpec((1, 1, d), lambda b, i, j: (b, 0, 1)),
                  pl.BlockSpec((1, d), lambda b, i, j: (0, 0)),
                  pl.BlockSpec((d, tn), lambda b, i, j: (0, j)),
                  pl.BlockSpec((hg_lb.shape[0], 2, gw), lambda b, i, j: (0, 0, 0))],
        out_specs=[pl.BlockSpec((1, seq, tn), lambda b, i, j: (b, 0, j)),
                   pl.BlockSpec((1, seq, gw), lambda b, i, j: (b, 0, jnp.maximum(j - z_tile0, 0)))],
        out_shape=[jax.ShapeDtypeStruct((bsz, seq, w_bf.shape[1]), F32),
                   jax.ShapeDtypeStruct((bsz, seq, 2 * gw), F32)],
        scratch_shapes=[pltpu.VMEM((seq, d), BF16)],
        compiler_params=_cparams(("arbitrary", "arbitrary", "arbitrary")),
        name="inproj_mix",
    )(x, mods3, mods3, norm_g.reshape(1, d), w_bf, hg_lb)


def _dft_tables(seq):
    half = seq // 2
    k = np.arange(half, dtype=np.int64)
    idx = (k[:, None] * k[None, :]) % seq
    ang = 2.0 * np.pi * np.arange(seq, dtype=np.float64) / seq
    cos_t = jnp.asarray(np.cos(ang)[idx].astype(np.float32))
    sin_t = jnp.asarray(np.sin(ang)[idx].astype(np.float32))
    return cos_t.astype(BF16), sin_t.astype(BF16)


def _pos_features(seq):
    t = np.linspace(0.0, 1.0, seq, dtype=np.float32)[:, None]
    w = (2.0 * math.pi * np.arange(seq, dtype=np.float32)[:, None] / seq).astype(np.float32)
    f = np.linspace(1e-4, HY_BANDS - 1, HY_BANDS, dtype=np.float32)[None, :]
    fw = (f * w).astype(np.float32)
    return np.concatenate([t, np.cos(fw), -np.sin(fw)], axis=-1).astype(np.float32)


def _filter_kernel(z_ref, w1_ref, b1_ref, f1_ref, w2_ref, b2_ref, f2_ref, w3f_ref, w3b_ref, t_ref, dl_ref,
                   sgn_ref, wk_ref, c2_ref, s2_ref, c_ref, s_ref, p_ref, hm_ref, hid_ref):
    half = c_ref.shape[0]
    n = 4 * half

    @pl.when((pl.program_id(0) == 0) & (pl.program_id(1) == 0))
    def _():
        for par in range(2):
            h = jnp.sin(f1_ref[...] * (jnp.dot(z_ref[par], w1_ref[...], preferred_element_type=F32,
                                               precision=HIGHEST) + b1_ref[...]))
            hid_ref[par] = jnp.sin(f2_ref[...] * (jnp.dot(h, w2_ref[...], preferred_element_type=F32,
                                                          precision=HIGHEST) + b2_ref[...]))

    taps = {}
    for par in range(2):
        h = hid_ref[par]
        window = jnp.exp(-t_ref[par] * dl_ref[...]) + HY_WINDOW_SHIFT
        taps["f", par] = jnp.dot(h, w3f_ref[...], preferred_element_type=F32, precision=HIGHEST) * window
        taps["b", par] = jnp.dot(h, w3b_ref[...], preferred_element_type=F32, precision=HIGHEST) * window
    row = lax.broadcasted_iota(jnp.int32, taps["b", 0].shape, 0)
    taps["b", 0] = jnp.where(row == 0, 0.0, taps["b", 0])
    inv = 1.0 / (sum(jnp.sum(jnp.abs(x), axis=0, keepdims=True) for x in taps.values()) + EPS)
    sgn = sgn_ref[...]
    spec = {}
    for key, x in taps.items():
        x = x * inv
        xb = x.astype(BF16)
        spec[key] = (_dot(c_ref[...], xb), _dot(s_ref[...], xb), jnp.sum(x * sgn, axis=0, keepdims=True))
    (fec, fes, fen), (foc, fos, fon) = spec["f", 0], spec["f", 1]
    (bec, bes, ben), (boc, bos, bon) = spec["b", 0], spec["b", 1]
    c2, s2 = c2_ref[...], s2_ref[...]
    wk = 2.0 * wk_ref[...]
    p_ref[0, 0] = (fec + bec) * wk
    p_ref[0, 1] = (bes - fes) * wk
    p_ref[0, 2] = (c2 * foc - s2 * fos + boc) * wk
    p_ref[0, 3] = (bos - c2 * fos - s2 * foc) * wk
    p_ref[0, 4] = (foc + c2 * boc - s2 * bos) * wk
    p_ref[0, 5] = (c2 * bos + s2 * boc - fos) * wk
    hm_ref[0] = jnp.concatenate([fen + ben, bon - fon], axis=0) * (2.0 / n)


def _hyena_filters(seq, pos_w1, pos_b1, freq1, pos_w2, pos_b2, freq2, pos_w3, dft_c, dft_s, ct):
    orders = 2
    half = seq // 2
    ch = pos_w3.shape[1] // (orders * 2)
    zt = _pos_features(seq)
    pdim = -(-zt.shape[1] // LANES) * LANES
    hid = -(-pos_w1.shape[1] // LANES) * LANES
    zt = np.pad(zt, ((0, 0), (0, pdim - zt.shape[1])))
    zt = np.stack([zt[0::2], zt[1::2]])
    pad2 = lambda a, r, c_: jnp.pad(a, ((0, r - a.shape[0]), (0, c_ - a.shape[1])))
    pos_w1 = pad2(pos_w1, pdim, hid)
    pos_w2 = pad2(pos_w2, hid, hid)
    pos_w3 = pad2(pos_w3, hid, pos_w3.shape[1])
    pos_b1, freq1, pos_b2, freq2 = (pad2(a.reshape(1, -1), 1, hid) for a in (pos_b1, freq1, pos_b2, freq2))
    t_col = np.linspace(0.0, 1.0, seq, dtype=np.float32)[:, None]
    t_col = np.stack([t_col[0::2], t_col[1::2]])
    min_decay = math.log(HY_DECAY_TARGET) / HY_SLOW_DECAY
    max_decay = math.log(HY_DECAY_TARGET) / HY_FAST_DECAY
    deltas = np.abs(np.linspace(min_decay, max_decay, ch, dtype=np.float32))[None, :]
    sgn = np.where(np.arange(half) % 2 == 0, 1.0, -1.0).astype(np.float32)[:, None]
    wk = np.full((half, 1), 2.0 / (2 * seq), np.float32)
    wk[0, 0] = 1.0 / (2 * seq)
    ang = 2.0 * np.pi * np.arange(half, dtype=np.float64)[:, None] / seq
    ncb = ch // ct
    const = lambda shape: pl.BlockSpec(shape, lambda o, c: (0,) * len(shape))
    return pl.pallas_call(
        _filter_kernel,
        grid=(orders, ncb),
        in_specs=[const((2, half, pdim)), const((pdim, hid)), const((1, hid)), const((1, hid)),
                  const((hid, hid)), const((1, hid)), const((1, hid)),
                  pl.BlockSpec((hid, ct), lambda o, c: (0, (2 * o) * ncb + c)),
                  pl.BlockSpec((hid, ct), lambda o, c: (0, (2 * o + 1) * ncb + c)),
                  const((2, half, 1)),
                  pl.BlockSpec((1, ct), lambda o, c: (0, c)),
                  const((half, 1)), const((half, 1)), const((half, 1)), const((half, 1)),
                  const((half, half)), const((half, half))],
        out_specs=[pl.BlockSpec((1, 6, half, ct), lambda o, c: (o, 0, 0, c)),
                   pl.BlockSpec((1, 2, ct), lambda o, c: (o, 0, c))],
        out_shape=[jax.ShapeDtypeStruct((orders, 6, half, ch), F32),
                   jax.ShapeDtypeStruct((orders, 2, ch), F32)],
        scratch_shapes=[pltpu.VMEM((2, half, hid), F32)],
        compiler_params=_cparams(("arbitrary", "arbitrary")),
        name="hyena_filter",
    )(jnp.asarray(zt), pos_w1, pos_b1, freq1, pos_w2, pos_b2, freq2, pos_w3, pos_w3, jnp.asarray(t_col),
      jnp.asarray(deltas), jnp.asarray(sgn), jnp.asarray(wk), jnp.asarray(np.cos(ang).astype(np.float32)),
      jnp.asarray(np.sin(ang).astype(np.float32)), dft_c, dft_s)


def _split_rows(x_ref):
    half = x_ref.shape[1] // 2
    return x_ref[0, pl.ds(0, half, stride=2), :], x_ref[0, pl.ds(1, half, stride=2), :]


def _short_conv_split(p_ref, w_ref, b_ref):
    pe, po = _split_rows(p_ref)
    half = pe.shape[0]
    pos = lax.broadcasted_iota(jnp.int32, pe.shape, 0) % (GRID_W // 2)
    before_even = jnp.where(pos == 0, 0.0, pltpu.roll(po, 1, axis=0))
    after_odd = jnp.where(pos == GRID_W // 2 - 1, 0.0, pltpu.roll(pe, half - 1, axis=0))
    w0, w1, w2 = w_ref[0:1, :], w_ref[1:2, :], w_ref[2:3, :]
    return (b_ref[...] + before_even * w0 + pe * w1 + po * w2,
            b_ref[...] + pe * w0 + po * w1 + after_odd * w2)


def _hyena_kernel(u_ref, g_ref, cwu_ref, cbu_ref, cwg_ref, cbg_ref, fwd_ref, inv_ref, p_ref, hm_ref,
                  bias_ref, sgn_ref, o_ref, spec_ref, *, conv_u, freq_tile):
    half = inv_ref.shape[0]
    ct = o_ref.shape[2]
    ue, uo = _short_conv_split(u_ref, cwu_ref, cbu_ref) if conv_u else _split_rows(u_ref)
    ub = jnp.concatenate([ue, uo], axis=1).astype(BF16)
    for ft in range(half // freq_tile):
        lo, hi = ft * freq_tile, (ft + 1) * freq_tile
        dc = _dot(fwd_ref[lo:hi, :], ub)
        ds = _dot(fwd_ref[half + lo:half + hi, :], ub)
        er, orr, es, os_ = dc[:, :ct], dc[:, ct:], ds[:, :ct], ds[:, ct:]
        p1r, p1i, p2r, p2i, p3r, p3i = (p_ref[0, i, lo:hi, :] for i in range(6))
        ar = er * p1r + es * p1i + orr * p2r + os_ * p2i
        br = er * p3r + es * p3i + orr * p1r + os_ * p1i
        ai = es * p1r - er * p1i + os_ * p2r - orr * p2i
        bi = es * p3r - er * p3i + os_ * p1r - orr * p1i
        spec_ref[lo:hi, :] = jnp.concatenate([ar, br], axis=1).astype(BF16)
        spec_ref[half + lo:half + hi, :] = jnp.concatenate([ai, bi], axis=1).astype(BF16)
    y = _dot(inv_ref[...], spec_ref[...])
    sgn = sgn_ref[...]
    en = jnp.sum(ue * sgn, axis=0, keepdims=True)
    on = jnp.sum(uo * sgn, axis=0, keepdims=True)
    hm_r, hm_i = hm_ref[0, 0:1, :], hm_ref[0, 1:2, :]
    bias = bias_ref[0]
    ge, go = _short_conv_split(g_ref, cwg_ref, cbg_ref)
    o_ref[0, pl.ds(0, half, stride=2), :] = ge * (y[:, :ct] + ue * bias + sgn * (en * hm_r + on * hm_i))
    o_ref[0, pl.ds(1, half, stride=2), :] = go * (y[:, ct:] + uo * bias + sgn * (on * hm_r - en * hm_i))


def _hyena_conv(u, u_col_blk, p_hy, gate_col_blk, conv_w, conv_b, dft_c, dft_s, ptab, hm, bias, order,
                conv_u, ct):
    bsz, seq, _ = p_hy.shape
    half = seq // 2
    ch = ptab.shape[3]
    ncb = ch // ct
    sgn = jnp.asarray(np.where(np.arange(half) % 2 == 0, 1.0, -1.0).astype(np.float32)[:, None])
    taps = conv_w.shape[0]
    cbias = conv_b.reshape(1, -1)
    const = lambda shape: pl.BlockSpec(shape, lambda c, b: (0,) * len(shape))
    return pl.pallas_call(
        functools.partial(_hyena_kernel, conv_u=conv_u, freq_tile=half // 4),
        grid=(ncb, bsz),
        in_specs=[pl.BlockSpec((1, seq, ct), lambda c, b: (b, 0, u_col_blk * ncb + c)),
                  pl.BlockSpec((1, seq, ct), lambda c, b: (b, 0, gate_col_blk * ncb + c)),
                  pl.BlockSpec((taps, ct), lambda c, b: (0, u_col_blk * ncb + c)),
                  pl.BlockSpec((1, ct), lambda c, b: (0, u_col_blk * ncb + c)),
                  pl.BlockSpec((taps, ct), lambda c, b: (0, gate_col_blk * ncb + c)),
                  pl.BlockSpec((1, ct), lambda c, b: (0, gate_col_blk * ncb + c)),
                  const((2 * half, half)), const((half, 2 * half)),
                  pl.BlockSpec((1, 6, half, ct), lambda c, b: (order, 0, 0, c)),
                  pl.BlockSpec((1, 2, ct), lambda c, b: (order, 0, c)),
                  pl.BlockSpec((1, 1, ct), lambda c, b: (order, 0, c)),
                  const((half, 1))],
        out_specs=pl.BlockSpec((1, seq, ct), lambda c, b: (b, 0, c)),
        out_shape=jax.ShapeDtypeStruct((bsz, seq, ch), F32),
        scratch_shapes=[pltpu.VMEM((2 * half, 2 * ct), BF16)],
        compiler_params=_cparams(("arbitrary", "arbitrary")),
        name=f"hyena_conv{order}",
    )(u, p_hy, conv_w, cbias, conv_w, cbias, jnp.concatenate([dft_c, dft_s], axis=0),
      jnp.concatenate([dft_c, dft_s], axis=1), ptab, hm, bias.reshape(bias.shape[0], 1, ch), sgn)


def _scan_consts():
    c, blk = SCAN_CHUNK, SCAN_BLOCK
    r = lax.broadcasted_iota(jnp.int32, (c, c), 0)
    s = lax.broadcasted_iota(jnp.int32, (c, c), 1)
    row = lax.broadcasted_iota(jnp.int32, (c, LANES), 0)
    fb = FAST_BLOCK

    def pair_mask(size, rev):
        same = (r // (2 * size)) == (s // (2 * size))
        hi_r, hi_s = r % (2 * size) >= size, s % (2 * size) >= size
        return same & ((~hi_r & hi_s) if rev else (hi_r & ~hi_s))

    return dict(
        cum_f=(s <= r).astype(F32), cum_b=(s >= r).astype(F32),
        m32_f=(r >= c // 2) & (s < c // 2), m32_b=(r < c // 2) & (s >= c // 2),
        m16_f=pair_mask(blk, False), m16_b=pair_mask(blk, True),
        m8_f=pair_mask(fb, False), m8_b=pair_mask(fb, True),
        md_f=(r // fb == s // fb) & (s <= r), md_b=(r // fb == s // fb) & (s >= r),
        sub=lax.broadcasted_iota(jnp.int32, (fb, LANES), 0),
        row=row, ones=jnp.ones((LANES, LANES), BF16))


def _state_update(st_ref, k, v, g, g_edge):
    kd = (k * jnp.exp(g_edge - g)).astype(BF16)
    st_ref[...] = st_ref[...] * jnp.exp(g_edge) + _dot_tn(v.astype(BF16), kd)


def _chunk_cumsum(logf, cst, reverse):
    return jnp.dot(cst["cum_b" if reverse else "cum_f"], logf, preferred_element_type=F32, precision=HIGHEST)


def _scan_chunk(st_ref, q, k, v, logf, cst, reverse):
    c, blk = SCAN_CHUNK, SCAN_BLOCK
    row = cst["row"]
    g = _chunk_cumsum(logf, cst, reverse)
    o = _dot_nt((q * jnp.exp(g)).astype(BF16), st_ref[...].astype(BF16))
    half = c // 2
    if not reverse:
        ref32 = g[half - 1:half, :]
        ref16 = jnp.where(row < half, g[blk - 1:blk, :], g[half + blk - 1:half + blk, :])
        t32, s32 = row >= half, row < half
        t16, s16 = row % (2 * blk) >= blk, row % (2 * blk) < blk
        g_edge = g[c - 1:c, :]
    else:
        ref32 = g[half:half + 1, :]
        ref16 = jnp.where(row < half, g[blk:blk + 1, :], g[half + blk:half + blk + 1, :])
        t32, s32 = row < half, row >= half
        t16, s16 = row % (2 * blk) < blk, row % (2 * blk) >= blk
        g_edge = g[0:1, :]
    a32 = jnp.where(t32, q * jnp.exp(jnp.minimum(g - ref32, 0.0)), 0.0).astype(BF16)
    b32 = jnp.where(s32, k * jnp.exp(jnp.minimum(ref32 - g, 0.0)), 0.0).astype(BF16)
    a16 = jnp.where(t16, q * jnp.exp(jnp.minimum(g - ref16, 0.0)), 0.0).astype(BF16)
    b16 = jnp.where(s16, k * jnp.exp(jnp.minimum(ref16 - g, 0.0)), 0.0).astype(BF16)
    att = (jnp.where(cst["m32_b" if reverse else "m32_f"], _dot_nt(a32, b32), 0.0)
           + jnp.where(cst["m16_b" if reverse else "m16_f"], _dot_nt(a16, b16), 0.0))
    o = o + _dot(att.astype(BF16), v.astype(BF16))
    pos = row % blk
    for d in range(blk):
        shift = d if not reverse else (c - d) % c
        if d == 0:
            ks, gs, vs = k, g, v
        else:
            ks = pltpu.roll(k, shift, axis=0)
            gs = pltpu.roll(g, shift, axis=0)
            vs = pltpu.roll(v, shift, axis=0)
        valid = (pos >= d) if not reverse else (pos + d < blk)
        w = jnp.where(valid, q * ks * jnp.exp(jnp.minimum(g - gs, 0.0)), 0.0)
        o = o + _dot(w.astype(BF16), cst["ones"]) * vs
    _state_update(st_ref, k, v, g, g_edge)
    return o


def _chunk_prepare(q, k, v, logf, cst, reverse):
    c, fb = SCAN_CHUNK, FAST_BLOCK
    nb = c // fb
    d = "b" if reverse else "f"
    sub = cst["sub"]
    blocks = lambda x: [x[fb * j:fb * (j + 1), :] for j in range(nb)]
    qs, ks = blocks(q), blocks(k)
    ps = []
    for p in blocks(logf):
        for s in (1, 2, 4):
            if reverse:
                p = p + jnp.where(sub + s < fb, pltpu.roll(p, fb - s, axis=0), 0.0)
            else:
                p = p + jnp.where(sub >= s, pltpu.roll(p, s, axis=0), 0.0)
        ps.append(p)
    last = 0 if reverse else fb - 1
    tot = [jnp.broadcast_to(p[last:last + 1, :], p.shape) for p in ps]
    decay = [jnp.exp(t) for t in tot]
    memo = {}

    def span(lo, hi):
        if hi <= lo:
            return None
        if (lo, hi) not in memo:
            if hi == nb and lo > 0:
                inner, last = span(lo + 1, hi), decay[lo]
            else:
                inner, last = span(lo, hi - 1), decay[hi - 1]
            memo[lo, hi] = last if inner is None else inner * last
        return memo[lo, hi]

    scale = lambda x, f: x if f is None else x * f
    qe = [qs[j] * jnp.exp(ps[j]) for j in range(nb)]
    kb = [ks[j] * jnp.exp(tot[j] - ps[j]) for j in range(nb)]
    zero = jnp.zeros_like(tot[0])
    cat = lambda parts: jnp.concatenate(parts, axis=0).astype(BF16)

    def level(n):
        a, b = [], []
        for j in range(nb):
            mid = (j // (2 * n)) * 2 * n + n
            later = j >= mid
            if later != reverse:
                a.append(scale(qe[j], span(j + 1, mid) if reverse else span(mid, j)))
                b.append(zero)
            else:
                a.append(zero)
                b.append(scale(kb[j], span(mid, j) if reverse else span(j + 1, mid)))
        return _dot_nt(cat(a), cat(b))

    bd = cat([ks[j] * jnp.exp(-ps[j]) for j in range(nb)])
    att = (level(nb // 2)
           + jnp.where(cst["m16_" + d], level(nb // 4), 0.0)
           + jnp.where(cst["m8_" + d], level(nb // 8), 0.0)
           + jnp.where(cst["md_" + d], _dot_nt(cat(qe), bd), 0.0))
    before = lambda j: span(j + 1, nb) if reverse else span(0, j)
    after = lambda j: span(0, j) if reverse else span(j + 1, nb)
    qd = cat([scale(qe[j], before(j)) for j in range(nb)])
    kd = cat([scale(kb[j], after(j)) for j in range(nb)])
    vb = v.astype(BF16)
    return att.astype(BF16), qd, vb, _dot_tn(vb, kd), span(0, nb)[0:1, :]


def _chunk_finish(st_ref, prepared):
    att, qd, vb, kv, decay = prepared
    o = _dot_nt(qd, st_ref[...].astype(BF16)) + _dot(att, vb)
    st_ref[...] = st_ref[...] * decay + kv
    return o


def _hgrn_kernel(q_ref, i_ref, kf_ref, kb_ref, g_ref, lf_ref, lbw_ref, ic_ref, zfc_ref, zbc_ref, lb_ref, ng_ref,
                 o_ref, of_ref, ob_ref, st_ref):
    seq = q_ref.shape[1]
    ctx_len = ic_ref.shape[1]
    hd = ng_ref.shape[1]
    heads = q_ref.shape[2] // hd
    c = SCAN_CHUNK
    n_chunks, n_ctx = seq // c, ctx_len // c
    cst = _scan_consts()
    lb = _gate_floor(lb_ref[...])
    lanes = [slice(h * hd, (h + 1) * hd) for h in range(heads)]
    st_f = [st_ref.at[2 * h] for h in range(heads)]
    st_b = [st_ref.at[2 * h + 1] for h in range(heads)]

    st_ref[...] = jnp.zeros_like(st_ref)

    def ctx_body(j, carry):
        rf = pl.multiple_of(j * c, c)
        rb = pl.multiple_of((n_ctx - 1 - j) * c, c)
        for h, ln in enumerate(lanes):
            for st, z_ref, r, lb_row, rev in ((st_f[h], zfc_ref, rf, lb[0:1, ln], False),
                                              (st_b[h], zbc_ref, rb, lb[1:2, ln], True)):
                k, logf = _forget_gate(z_ref[0, pl.ds(r, c), ln], lb_row)
                *_, kv, decay = _chunk_prepare(jnp.zeros_like(k), k, ic_ref[0, pl.ds(r, c), ln], logf, cst, rev)
                st[...] = st[...] * decay + kv
        return carry

    lax.fori_loop(0, n_ctx, ctx_body, 0)

    def rows(j):
        if isinstance(j, int):
            return j * c, (n_chunks - 1 - j) * c
        return pl.multiple_of(j * c, c), pl.multiple_of((n_chunks - 1 - j) * c, c)

    def operands(r, ln, k_ref, logf_ref):
        sl = (0, pl.ds(r, c), ln)
        return q_ref[sl], k_ref[sl], i_ref[sl], logf_ref[sl]

    def direct_body(j, carry):
        rf, rb = rows(j)
        for h, ln in enumerate(lanes):
            of_ref[pl.ds(rf, c), ln] = _scan_chunk(st_f[h], *operands(rf, ln, kf_ref, lf_ref), cst, False)
            ob_ref[pl.ds(rb, c), ln] = _scan_chunk(st_b[h], *operands(rb, ln, kb_ref, lbw_ref), cst, True)
        return carry

    def prepare(j):
        rf, rb = rows(j)
        out = []
        for ln in lanes:
            out.append(_chunk_prepare(*operands(rf, ln, kf_ref, lf_ref), cst, False))
            out.append(_chunk_prepare(*operands(rb, ln, kb_ref, lbw_ref), cst, True))
        return tuple(out)

    def finish(j, prepared):
        rf, rb = rows(j)
        for h, ln in enumerate(lanes):
            of_ref[pl.ds(rf, c), ln] = _chunk_finish(st_f[h], prepared[2 * h])
            ob_ref[pl.ds(rb, c), ln] = _chunk_finish(st_b[h], prepared[2 * h + 1])

    def factored_body(j, prepared):
        upcoming = prepare(j + 1)
        finish(j, prepared)
        return upcoming

    bounded = jnp.min(lb) >= math.exp(-MAX_FAST_EXPONENT / FAST_BLOCK)

    @pl.when(bounded)
    def _():
        looped = (n_chunks - 1) // SCAN_UNROLL * SCAN_UNROLL
        prepared = lax.fori_loop(0, looped, factored_body, prepare(0), unroll=SCAN_UNROLL)
        for j in range(looped, n_chunks - 1):
            prepared = factored_body(j, prepared)
        finish(n_chunks - 1, prepared)

    @pl.when(jnp.logical_not(bounded))
    def _():
        lax.fori_loop(0, n_chunks, direct_body, 0)

    for ln in lanes:
        o = of_ref[:, ln] + ob_ref[:, ln]
        y = o * lax.rsqrt(jnp.mean(o * o, axis=-1, keepdims=True) + EPS) * ng_ref[...]
        o_ref[0, :, ln] = y * _silu(g_ref[0, :, ln])


def _hgrn(p, col0, logf, pc, hg_lb, norm_g, heads_per_step):
    bsz, seq, _ = p.shape
    ctx_len = pc.shape[1]
    hd = norm_g.shape[0]
    w = heads_per_step * hd
    steps = HG_HEADS // heads_per_step
    lat = lambda grp: pl.BlockSpec((1, seq, w), lambda b, h: (b, 0, (col0 + grp * HG_HEADS) // heads_per_step + h))
    lgf = lambda d: pl.BlockSpec((1, seq, w), lambda b, h: (b, 0, d * steps + h))
    ctx = lambda grp: pl.BlockSpec((1, ctx_len, w), lambda b, h: (b, 0, grp * steps + h))
    return pl.pallas_call(
        _hgrn_kernel,
        grid=(bsz, steps),
        in_specs=[lat(0), lat(1), lat(2), lat(3), lat(4), lgf(0), lgf(1), ctx(0), ctx(1), ctx(2),
                  pl.BlockSpec((hg_lb.shape[0], 2, w), lambda b, h: (0, 0, h)),
                  pl.BlockSpec((1, hd), lambda b, h: (0, 0))],
        out_specs=pl.BlockSpec((1, seq, w), lambda b, h: (b, 0, h)),
        out_shape=jax.ShapeDtypeStruct((bsz, seq, HG_HEADS * hd), F32),
        scratch_shapes=[pltpu.VMEM((seq, w), F32), pltpu.VMEM((seq, w), F32),
                        pltpu.VMEM((2 * heads_per_step, hd, hd), F32)],
        compiler_params=_cparams(("arbitrary", "arbitrary")),
        name="hgrn_scan",
    )(p, p, p, p, p, logf, logf, pc, pc, pc, hg_lb, norm_g.reshape(1, hd))


def _outproj_kernel(yhy_ref, yhg_ref, x_ref, g1_ref, sh2_ref, sc2_ref, hyg_ref, n2g_ref, wt_ref, wb_ref, rw_ref,
                    x1_ref, h2_ref, aff_ref):
    yhy = yhy_ref[0]
    yn = yhy * lax.rsqrt(jnp.mean(yhy * yhy, axis=-1, keepdims=True) + EPS) * hyg_ref[...]
    mix = _dot(yn.astype(BF16), wt_ref[...]) + _dot(yhg_ref[0].astype(BF16), wb_ref[...])
    x1 = x_ref[0] + g1_ref[0] * mix
    x1_ref[0] = x1
    h2 = (x1 * lax.rsqrt(jnp.mean(x1 * x1, axis=-1, keepdims=True) + EPS) * n2g_ref[...]
          * (1.0 + sc2_ref[0]) + sh2_ref[0])
    h_hi = h2.astype(BF16)
    h2_ref[0] = h_hi
    h_lo = (h2 - h_hi.astype(F32)).astype(BF16)
    rw = rw_ref[...]
    r_hi = rw.astype(BF16)
    r_lo = (rw - r_hi.astype(F32)).astype(BF16)
    ne = rw.shape[0]
    both = _dot_nt(jnp.concatenate([r_hi, r_lo], axis=0), h_hi)
    logits = both[:ne] + both[ne:] + _dot_nt(r_hi, h_lo)
    ex = jnp.exp(logits - jnp.max(logits, axis=0, keepdims=True))
    aff_ref[0] = ex / jnp.sum(ex, axis=0, keepdims=True)


def _outproj(y_hy, y_hg, x, mods3, hy_norm_g, norm2_g, w_top, w_bot, router_wt, tm):
    bsz, seq, d = x.shape
    wh = y_hy.shape[2]
    ne = router_wt.shape[0]
    mod = lambda k: pl.BlockSpec((1, 1, d), lambda b, i: (b, 0, k))
    const = lambda shape: pl.BlockSpec(shape, lambda b, i: (0,) * len(shape))
    return pl.pallas_call(
        _outproj_kernel,
        grid=(bsz, seq // tm),
        in_specs=[pl.BlockSpec((1, tm, wh), lambda b, i: (b, i, 0)),
                  pl.BlockSpec((1, tm, wh), lambda b, i: (b, i, 0)),
                  pl.BlockSpec((1, tm, d), lambda b, i: (b, i, 0)),
                  mod(2), mod(3), mod(4),
                  const((1, wh)), const((1, d)), const((wh, d)), const((wh, d)), const((ne, d))],
        out_specs=[pl.BlockSpec((1, tm, d), lambda b, i: (b, i, 0)),
                   pl.BlockSpec((1, tm, d), lambda b, i: (b, i, 0)),
                   pl.BlockSpec((1, ne, tm), lambda b, i: (b, 0, i))],
        out_shape=[jax.ShapeDtypeStruct((bsz, seq, d), F32),
                   jax.ShapeDtypeStruct((bsz, seq, d), BF16),
                   jax.ShapeDtypeStruct((bsz, ne, seq), F32)],
        compiler_params=_cparams(("arbitrary", "arbitrary")),
        name="outproj_router",
    )(y_hy, y_hg, x, mods3, mods3, mods3, hy_norm_g.reshape(1, wh), norm2_g.reshape(1, d), w_top, w_bot, router_wt)


def _select_kernel(a_ref, slot_ref, *, cap, rows_per_expert):
    nr = a_ref.shape[1]
    affs = [a_ref[i] for i in range(a_ref.shape[0])]
    r = lax.broadcasted_iota(jnp.int32, (nr, nr), 0)
    s = lax.broadcasted_iota(jnp.int32, (nr, nr), 1)
    same = (r // rows_per_expert) == (s // rows_per_expert)
    grp = same.astype(BF16)
    before = (same & (s < r)).astype(BF16)
    li = lax.broadcasted_iota(jnp.int32, (LANES, LANES), 0)
    lj = lax.broadcasted_iota(jnp.int32, (LANES, LANES), 1)
    ones = jnp.ones((LANES, LANES), BF16)
    strict = (li < lj).astype(BF16)

    def row_total(ind):
        return _dot(ind, ones).astype(BF16)

    def expert_total(ind):
        return _dot(grp, row_total(ind))

    def prefix(ind):
        return _dot(ind, strict) + _dot(before, row_total(ind))

    ne = nr // rows_per_expert

    def count_at_least(a, cand):
        ind = jnp.where(a >= cand, 1.0, 0.0).reshape(ne, rows_per_expert, LANES)
        tot = jnp.sum(jnp.sum(ind, axis=1, keepdims=True), axis=2, keepdims=True)
        return jnp.broadcast_to(tot, ind.shape).reshape(nr, LANES)

    def search(i, thrs):
        bit = jnp.left_shift(jnp.int32(1), 30 - i)
        out = []
        for a, t in zip(affs, thrs):
            cand = t | bit
            out.append(jnp.where(count_at_least(a, pltpu.bitcast(cand, F32)) >= cap, cand, t))
        return tuple(out)

    thrs = lax.fori_loop(0, 31, search, tuple(jnp.zeros((nr, LANES), jnp.int32) for _ in affs))
    for i, (a, thr) in enumerate(zip(affs, thrs)):
        thr_f = pltpu.bitcast(thr, F32)
        gt = a > thr_f
        eq = a == thr_f
        need = cap - expert_total(jnp.where(gt, 1.0, 0.0).astype(BF16))
        eq_rank = prefix(jnp.where(eq, 1.0, 0.0).astype(BF16))
        sel = gt | (eq & (eq_rank < need))
        rank = prefix(jnp.where(sel, 1.0, 0.0).astype(BF16))
        slot_ref[i] = jnp.where(sel, rank, -1.0)


def _select(aff_t, cap, bt):
    bsz, ne, n = aff_t.shape
    rpe = n // LANES
    a2 = aff_t.reshape(bsz, ne * rpe, LANES)
    slot = pl.pallas_call(
        functools.partial(_select_kernel, cap=cap, rows_per_expert=rpe),
        grid=(bsz // bt,),
        in_specs=[pl.BlockSpec((bt, ne * rpe, LANES), lambda b: (b, 0, 0))],
        out_specs=pl.BlockSpec((bt, ne * rpe, LANES), lambda b: (b, 0, 0)),
        out_shape=jax.ShapeDtypeStruct((bsz, ne * rpe, LANES), F32),
        compiler_params=_cparams(("arbitrary",)),
        name="ec_select",
    )(a2)
    return slot.reshape(bsz, ne, n)


def _gather_kernel(slot_ref, h_ref, o_ref, *, cap):
    slot = slot_ref[0, 0]
    j = lax.broadcasted_iota(jnp.int32, (cap, slot.shape[1]), 0).astype(F32)
    onehot = jnp.where(slot == j, 1.0, 0.0).astype(BF16)
    o_ref[0, 0] = _dot(onehot, h_ref[0]).astype(BF16)


def _gather(slot4, h2, cap):
    bsz, ne, _, n = slot4.shape
    d = h2.shape[2]
    return pl.pallas_call(
        functools.partial(_gather_kernel, cap=cap),
        grid=(bsz, ne),
        in_specs=[pl.BlockSpec((1, 1, 1, n), lambda b, e: (b, e, 0, 0)),
                  pl.BlockSpec((1, n, d), lambda b, e: (b, 0, 0))],
        out_specs=pl.BlockSpec((1, 1, cap, d), lambda b, e: (b, e, 0, 0)),
        out_shape=jax.ShapeDtypeStruct((bsz, ne, cap, d), BF16),
        compiler_params=_cparams(("arbitrary", "arbitrary")),
        name="ec_gather",
    )(slot4, h2)


def _expert_kernel(x_ref, w1_ref, w3_ref, w2_ref, o_ref, acc_ref):
    f = pl.program_id(2)
    bt, _, cap, d = x_ref.shape
    x = x_ref[...].reshape(bt * cap, d)

    @pl.when(f == 0)
    def _():
        acc_ref[...] = jnp.zeros_like(acc_ref)

    w1, w3, w2 = w1_ref[0].astype(BF16), w3_ref[0].astype(BF16), w2_ref[0].astype(BF16)
    rows = bt * cap // EXPERT_ROW_TILES
    for r in range(EXPERT_ROW_TILES):
        sl = slice(r * rows, (r + 1) * rows)
        a = _dot(x[sl], w1)
        b = _dot(x[sl], w3)
        acc_ref[sl, :] += _dot((_silu(a) * b).astype(BF16), w2)

    @pl.when(f == pl.num_programs(2) - 1)
    def _():
        o_ref[...] = acc_ref[...].astype(BF16).reshape(bt, 1, cap, d)


def _experts(xg, w1, w3, w2, bt, ft):
    bsz, ne, cap, d = xg.shape
    ff = w1.shape[2]
    return pl.pallas_call(
        _expert_kernel,
        grid=(ne, bsz // bt, ff // ft),
        in_specs=[pl.BlockSpec((bt, 1, cap, d), lambda e, m, f: (m, e, 0, 0)),
                  pl.BlockSpec((1, d, ft), lambda e, m, f: (e, 0, f)),
                  pl.BlockSpec((1, d, ft), lambda e, m, f: (e, 0, f)),
                  pl.BlockSpec((1, ft, d), lambda e, m, f: (e, f, 0))],
        out_specs=pl.BlockSpec((bt, 1, cap, d), lambda e, m, f: (m, e, 0, 0)),
        out_shape=jax.ShapeDtypeStruct((bsz, ne, cap, d), BF16),
        scratch_shapes=[pltpu.VMEM((bt * cap, d), F32)],
        compiler_params=_cparams(("arbitrary", "arbitrary", "arbitrary")),
        name="ec_experts",
    )(xg, w1, w3, w2)


def _combine_kernel(y_ref, slot_ref, aff_ref, x1_ref, g2_ref, fg_ref, o_ref, *, cap):
    _, ne, _, d = y_ref.shape
    slot = slot_ref[0]
    aff = aff_ref[0]
    j = lax.broadcasted_iota(jnp.int32, (slot.shape[0], cap), 1).astype(F32)
    gated = jnp.concatenate([jnp.where(slot[:, e:e + 1] == j, aff[:, e:e + 1], 0.0).astype(BF16)
                             for e in range(ne)], axis=1)
    moe = _dot(gated, y_ref[0].reshape(ne * cap, d))
    x2 = x1_ref[0] + g2_ref[0] * moe
    o_ref[0] = x2 * lax.rsqrt(jnp.mean(x2 * x2, axis=-1, keepdims=True) + EPS) * fg_ref[...]


def _combine(y, slot_nt, aff_nt, x1, mods3, final_g, cap, tn):
    bsz, ne, _, d = y.shape
    n = x1.shape[1]
    return pl.pallas_call(
        functools.partial(_combine_kernel, cap=cap),
        grid=(bsz, n // tn),
        in_specs=[pl.BlockSpec((1, ne, cap, d), lambda b, t: (b, 0, 0, 0)),
                  pl.BlockSpec((1, tn, ne), lambda b, t: (b, t, 0)),
                  pl.BlockSpec((1, tn, ne), lambda b, t: (b, t, 0)),
                  pl.BlockSpec((1, tn, d), lambda b, t: (b, t, 0)),
                  pl.BlockSpec((1, 1, d), lambda b, t: (b, 0, 5)),
                  pl.BlockSpec((1, d), lambda b, t: (0, 0))],
        out_specs=pl.BlockSpec((1, tn, d), lambda b, t: (b, t, 0)),
        out_shape=jax.ShapeDtypeStruct((bsz, n, d), F32),
        compiler_params=_cparams(("arbitrary", "arbitrary")),
        name="ec_combine",
    )(y, slot_nt, aff_nt, x1, mods3, final_g.reshape(1, d))


def kernel(x, c, ctx, c_ctx, ada_w, ada_b, norm1_g, w_in, hy_conv_w, hy_conv_b, hy_pos_w1, hy_pos_b1, hy_freq1,
           hy_pos_w2, hy_pos_b2, hy_freq2, hy_pos_w3, hy_bias, hy_norm_g, hg_lb, hg_norm_g, w_out, norm2_g,
           router_w, exp_w1, exp_w3, exp_w2, final_g):
    bsz, seq, d = x.shape
    hyw = hy_norm_g.shape[1]
    hgw = w_in.shape[2] - 3 * hyw
    hg = hgw // 5
    assert hyw == hg, "column groups of the input projection are addressed with one width"
    cap = CAPACITY_FACTOR * seq // N_EXPERTS

    rows = -(-(bsz + 1) // 8) * 8
    s_rows = jnp.zeros((rows, d), F32).at[:bsz].set(c).at[bsz].set(c_ctx)
    mods3 = _ada(s_rows, ada_w[0], ada_b[0]).reshape(rows, 1, 6 * d)

    w_bf = w_in[0].astype(BF16)
    p, logf = _inproj_mix(x, mods3, norm1_g[0], w_bf, hg_lb, INPROJ_COL_TILE, hg, 3, 5, 6, (hg // HG_HEADS) ** -0.5)
    w_ctx = w_bf[:, 3 * hyw + hg:3 * hyw + 4 * hg]
    pc = _inproj(ctx, mods3, lambda b: bsz, norm1_g[0], w_ctx, 0, 1, ctx.shape[1], 3 * hg)

    dft_c, dft_s = _dft_tables(seq)
    ct = HYENA_CH_TILE
    ptab, hm = _hyena_filters(seq, hy_pos_w1[0], hy_pos_b1[0], hy_freq1[0], hy_pos_w2[0], hy_pos_b2[0],
                              hy_freq2[0], hy_pos_w3[0], dft_c, dft_s, FILTER_CH_TILE)
    z = _hyena_conv(p, 0, p, 1, hy_conv_w[0], hy_conv_b[0], dft_c, dft_s, ptab, hm, hy_bias[0], 0, True, ct)
    y_hy = _hyena_conv(z, 0, p, 2, hy_conv_w[0], hy_conv_b[0], dft_c, dft_s, ptab, hm, hy_bias[0], 1, False, ct)

    y_hg = _hgrn(p, 3 * hyw // (hg // HG_HEADS), logf, pc, hg_lb, hg_norm_g[0], SCAN_HEADS_PER_STEP)

    w_o = w_out[0].astype(BF16)
    x1, h2, aff_t = _outproj(y_hy, y_hg, x, mods3, hy_norm_g[0], norm2_g[0], w_o[:hyw], w_o[hyw:],
                             router_w[0].T, OUTPROJ_ROW_TILE)

    slot = _select(aff_t, cap, SELECT_SAMPLES)
    xg = _gather(slot.reshape(bsz, N_EXPERTS, 1, seq), h2, cap)
    y = _experts(xg, exp_w1[0], exp_w3[0], exp_w2[0], EXPERT_SAMPLES, EXPERT_FF_TILE)
    return _combine(y, jnp.swapaxes(slot, 1, 2), jnp.swapaxes(aff_t, 1, 2), x1, mods3, final_g, cap,
                    COMBINE_ROW_TILE)
```

```python
import functools
import math

import numpy as np
import jax
import jax.numpy as jnp
from jax import lax
from jax.experimental import pallas as pl
from jax.experimental.pallas import tpu as pltpu

F32 = jnp.float32
BF16 = jnp.bfloat16
HIGHEST = lax.Precision.HIGHEST

GRID_W = 64
HG_HEADS = 4
HY_BANDS = 16
HY_DECAY_TARGET = 1e-2
HY_FAST_DECAY = 0.3
HY_SLOW_DECAY = 1.5
HY_WINDOW_SHIFT = 0.05
N_EXPERTS = 16
CAPACITY_FACTOR = 2
EPS = 1e-6

SCAN_CHUNK = 64
SCAN_BLOCK = 16
FAST_BLOCK = 8
MAX_FAST_EXPONENT = 80.0
SCAN_UNROLL = 2
SCAN_HEADS_PER_STEP = 2
LANES = 128
VMEM_LIMIT = 56 * 1024 * 1024

INPROJ_COL_TILE = 1024
INPROJ_ROW_TILES = 8
FILTER_CH_TILE = 256
HYENA_CH_TILE = LANES
OUTPROJ_ROW_TILE = 1024
SELECT_SAMPLES = 8
EXPERT_SAMPLES = 8
EXPERT_FF_TILE = 512
EXPERT_ROW_TILES = 2
COMBINE_ROW_TILE = 512


def _cparams(sem):
    return pltpu.CompilerParams(dimension_semantics=sem, vmem_limit_bytes=VMEM_LIMIT)


def _dot(a, b):
    return jnp.dot(a, b, preferred_element_type=F32)


def _dot_nt(a, b):
    return lax.dot_general(a, b, (((1,), (1,)), ((), ())), preferred_element_type=F32)


def _dot_tn(a, b):
    return lax.dot_general(a, b, (((0,), (0,)), ((), ())), preferred_element_type=F32)


def _silu(x):
    return x * jax.nn.sigmoid(x)


def _ada_kernel(s_ref, w_ref, b_ref, o_ref):
    s = _silu(s_ref[...])
    o_ref[...] = jnp.dot(s, w_ref[...], preferred_element_type=F32, precision=HIGHEST) + b_ref[...]


def _ada(s_rows, ada_w, ada_b):
    rows, d = s_rows.shape
    n = ada_w.shape[1]
    tn = 1024
    return pl.pallas_call(
        _ada_kernel,
        grid=(n // tn,),
        in_specs=[pl.BlockSpec((rows, d), lambda j: (0, 0)),
                  pl.BlockSpec((d, tn), lambda j: (0, j)),
                  pl.BlockSpec((1, tn), lambda j: (0, j))],
        out_specs=pl.BlockSpec((rows, tn), lambda j: (0, j)),
        out_shape=jax.ShapeDtypeStruct((rows, n), F32),
        compiler_params=_cparams(("arbitrary",)),
        name="ada",
    )(s_rows, ada_w, ada_b.reshape(1, n))


def _inproj_kernel(x_ref, sh_ref, sc_ref, g_ref, w_ref, o_ref, h_scr):
    @pl.when(pl.program_id(2) == 0)
    def _():
        x = x_ref[0]
        y = x * lax.rsqrt(jnp.mean(x * x, axis=-1, keepdims=True) + EPS) * g_ref[...]
        h_scr[...] = (y * (1.0 + sc_ref[0]) + sh_ref[0]).astype(BF16)

    o_ref[0] = _dot(h_scr[...], w_ref[...])


def _inproj(x, mods3, mod_row, norm_g, w_bf, col_blk0, n_col_blk, tm, tn):
    bsz, seq, d = x.shape
    return pl.pallas_call(
        _inproj_kernel,
        grid=(bsz, seq // tm, n_col_blk),
        in_specs=[pl.BlockSpec((1, tm, d), lambda b, i, j: (b, i, 0)),
                  pl.BlockSpec((1, 1, d), lambda b, i, j: (mod_row(b), 0, 0)),
                  pl.BlockSpec((1, 1, d), lambda b, i, j: (mod_row(b), 0, 1)),
                  pl.BlockSpec((1, d), lambda b, i, j: (0, 0)),
                  pl.BlockSpec((d, tn), lambda b, i, j: (0, col_blk0 + j))],
        out_specs=pl.BlockSpec((1, tm, tn), lambda b, i, j: (b, i, j)),
        out_shape=jax.ShapeDtypeStruct((bsz, seq, n_col_blk * tn), F32),
        scratch_shapes=[pltpu.VMEM((tm, d), BF16)],
        compiler_params=_cparams(("arbitrary", "arbitrary", "arbitrary")),
        name="inproj",
    )(x, mods3, mods3, norm_g.reshape(1, d), w_bf)


def _gate_floor(lb_raw):
    e = jnp.exp(lb_raw - jnp.max(lb_raw, axis=0, keepdims=True))
    return e[0] / jnp.sum(e, axis=0)


def _forget_gate(z, lb):
    f = lb + (1.0 - lb) * jax.nn.sigmoid(z)
    return 1.0 - f, jnp.log(f)


def _inproj_mix_kernel(x_ref, sh_ref, sc_ref, g_ref, w_ref, lb_ref, o_ref, lf_ref, h_scr, *, n_col_blk, gw,
                       q_group, zf_group, zb_group, q_scale):
    j = pl.program_id(2)
    lb = _gate_floor(lb_ref[...])
    per_tile = o_ref.shape[2] // gw
    for jj in range(n_col_blk):
        @pl.when(j == jj)
        def _(jj=jj):
            rt = h_scr.shape[0] // INPROJ_ROW_TILES
            for loc in range(per_tile):
                grp = jj * per_tile + loc
                cols = slice(loc * gw, (loc + 1) * gw)
                for r in range(INPROJ_ROW_TILES):
                    rows = slice(r * rt, (r + 1) * rt)
                    if jj == 0 and loc == 0:
                        x = x_ref[0, rows, :]
                        y = x * lax.rsqrt(jnp.mean(x * x, axis=-1, keepdims=True) + EPS) * g_ref[...]
                        h_scr[rows, :] = (y * (1.0 + sc_ref[0]) + sh_ref[0]).astype(BF16)
                    part = _dot(h_scr[rows, :], w_ref[:, cols])
                    if grp == q_group:
                        part = _silu(part) * q_scale
                    elif grp in (zf_group, zb_group):
                        d = 0 if grp == zf_group else 1
                        part, lf_ref[0, rows, :] = _forget_gate(part, lb[d:d + 1, :])
                    o_ref[0, rows, cols] = part


def _inproj_mix(x, mods3, norm_g, w_bf, hg_lb, tn, gw, q_group, zf_group, zb_group, q_scale):
    bsz, seq, d = x.shape
    n_col_blk = w_bf.shape[1] // tn
    z_tile0 = zf_group * gw // tn
    assert zb_group * gw // tn == z_tile0 + 1 and zf_group * gw // tn != q_group * gw // tn
    return pl.pallas_call(
        functools.partial(_inproj_mix_kernel, n_col_blk=n_col_blk, gw=gw, q_group=q_group, zf_group=zf_group,
                          zb_group=zb_group, q_scale=q_scale),
        grid=(bsz, 1, n_col_blk),
        in_specs=[pl.BlockSpec((1, seq, d), lambda b, i, j: (b, 0, 0)),
                  pl.BlockSpec((1, 1, d), lambda b, i, j: (b, 0, 0)),
                  pl.BlockSpec((1, 1, d), lambda b, i, j: (b, 0, 1)),
                  pl.BlockSpec((1, d), lambda b, i, j: (0, 0)),
                  pl.BlockSpec((d, tn), lambda b, i, j: (0, j)),
                  pl.BlockSpec((hg_lb.shape[0], 2, gw), lambda b, i, j: (0, 0, 0))],
        out_specs=[pl.BlockSpec((1, seq, tn), lambda b, i, j: (b, 0, j)),
                   pl.BlockSpec((1, seq, gw), lambda b, i, j: (b, 0, jnp.maximum(j - z_tile0, 0)))],
        out_shape=[jax.ShapeDtypeStruct((bsz, seq, w_bf.shape[1]), F32),
                   jax.ShapeDtypeStruct((bsz, seq, 2 * gw), F32)],
        scratch_shapes=[pltpu.VMEM((seq, d), BF16)],
        compiler_params=_cparams(("arbitrary", "arbitrary", "arbitrary")),
        name="inproj_mix",
    )(x, mods3, mods3, norm_g.reshape(1, d), w_bf, hg_lb)


def _dft_tables(seq):
    half = seq // 2
    k = np.arange(half, dtype=np.int64)
    idx = (k[:, None] * k[None, :]) % seq
    ang = 2.0 * np.pi * np.arange(seq, dtype=np.float64) / seq
    cos_t = jnp.asarray(np.cos(ang)[idx].astype(np.float32))
    sin_t = jnp.asarray(np.sin(ang)[idx].astype(np.float32))
    return cos_t.astype(BF16), sin_t.astype(BF16)


def _pos_features(seq):
    t = np.linspace(0.0, 1.0, seq, dtype=np.float32)[:, None]
    w = (2.0 * math.pi * np.arange(seq, dtype=np.float32)[:, None] / seq).astype(np.float32)
    f = np.linspace(1e-4, HY_BANDS - 1, HY_BANDS, dtype=np.float32)[None, :]
    fw = (f * w).astype(np.float32)
    return np.concatenate([t, np.cos(fw), -np.sin(fw)], axis=-1).astype(np.float32)


def _filter_kernel(z_ref, w1_ref, b1_ref, f1_ref, w2_ref, b2_ref, f2_ref, w3f_ref, w3b_ref, t_ref, dl_ref,
                   sgn_ref, wk_ref, c2_ref, s2_ref, c_ref, s_ref, p_ref, hm_ref, hid_ref):
    half = c_ref.shape[0]
    n = 4 * half

    @pl.when((pl.program_id(0) == 0) & (pl.program_id(1) == 0))
    def _():
        for par in range(2):
            h = jnp.sin(f1_ref[...] * (jnp.dot(z_ref[par], w1_ref[...], preferred_element_type=F32,
                                               precision=HIGHEST) + b1_ref[...]))
            hid_ref[par] = jnp.sin(f2_ref[...] * (jnp.dot(h, w2_ref[...], preferred_element_type=F32,
                                                          precision=HIGHEST) + b2_ref[...]))

    taps = {}
    for par in range(2):
        h = hid_ref[par]
        window = jnp.exp(-t_ref[par] * dl_ref[...]) + HY_WINDOW_SHIFT
        taps["f", par] = jnp.dot(h, w3f_ref[...], preferred_element_type=F32, precision=HIGHEST) * window
        taps["b", par] = jnp.dot(h, w3b_ref[...], preferred_element_type=F32, precision=HIGHEST) * window
    row = lax.broadcasted_iota(jnp.int32, taps["b", 0].shape, 0)
    taps["b", 0] = jnp.where(row == 0, 0.0, taps["b", 0])
    inv = 1.0 / (sum(jnp.sum(jnp.abs(x), axis=0, keepdims=True) for x in taps.values()) + EPS)
    sgn = sgn_ref[...]
    spec = {}
    for key, x in taps.items():
        x = x * inv
        xb = x.astype(BF16)
        spec[key] = (_dot(c_ref[...], xb), _dot(s_ref[...], xb), jnp.sum(x * sgn, axis=0, keepdims=True))
    (fec, fes, fen), (foc, fos, fon) = spec["f", 0], spec["f", 1]
    (bec, bes, ben), (boc, bos, bon) = spec["b", 0], spec["b", 1]
    c2, s2 = c2_ref[...], s2_ref[...]
    wk = 2.0 * wk_ref[...]
    p_ref[0, 0] = (fec + bec) * wk
    p_ref[0, 1] = (bes - fes) * wk
    p_ref[0, 2] = (c2 * foc - s2 * fos + boc) * wk
    p_ref[0, 3] = (bos - c2 * fos - s2 * foc) * wk
    p_ref[0, 4] = (foc + c2 * boc - s2 * bos) * wk
    p_ref[0, 5] = (c2 * bos + s2 * boc - fos) * wk
    hm_ref[0] = jnp.concatenate([fen + ben, bon - fon], axis=0) * (2.0 / n)


def _hyena_filters(seq, pos_w1, pos_b1, freq1, pos_w2, pos_b2, freq2, pos_w3, dft_c, dft_s, ct):
    orders = 2
    half = seq // 2
    ch = pos_w3.shape[1] // (orders * 2)
    zt = _pos_features(seq)
    pdim = -(-zt.shape[1] // LANES) * LANES
    hid = -(-pos_w1.shape[1] // LANES) * LANES
    zt = np.pad(zt, ((0, 0), (0, pdim - zt.shape[1])))
    zt = np.stack([zt[0::2], zt[1::2]])
    pad2 = lambda a, r, c_: jnp.pad(a, ((0, r - a.shape[0]), (0, c_ - a.shape[1])))
    pos_w1 = pad2(pos_w1, pdim, hid)
    pos_w2 = pad2(pos_w2, hid, hid)
    pos_w3 = pad2(pos_w3, hid, pos_w3.shape[1])
    pos_b1, freq1, pos_b2, freq2 = (pad2(a.reshape(1, -1), 1, hid) for a in (pos_b1, freq1, pos_b2, freq2))
    t_col = np.linspace(0.0, 1.0, seq, dtype=np.float32)[:, None]
    t_col = np.stack([t_col[0::2], t_col[1::2]])
    min_decay = math.log(HY_DECAY_TARGET) / HY_SLOW_DECAY
    max_decay = math.log(HY_DECAY_TARGET) / HY_FAST_DECAY
    deltas = np.abs(np.linspace(min_decay, max_decay, ch, dtype=np.float32))[None, :]
    sgn = np.where(np.arange(half) % 2 == 0, 1.0, -1.0).astype(np.float32)[:, None]
    wk = np.full((half, 1), 2.0 / (2 * seq), np.float32)
    wk[0, 0] = 1.0 / (2 * seq)
    ang = 2.0 * np.pi * np.arange(half, dtype=np.float64)[:, None] / seq
    ncb = ch // ct
    const = lambda shape: pl.BlockSpec(shape, lambda o, c: (0,) * len(shape))
    return pl.pallas_call(
        _filter_kernel,
        grid=(orders, ncb),
        in_specs=[const((2, half, pdim)), const((pdim, hid)), const((1, hid)), const((1, hid)),
                  const((hid, hid)), const((1, hid)), const((1, hid)),
                  pl.BlockSpec((hid, ct), lambda o, c: (0, (2 * o) * ncb + c)),
                  pl.BlockSpec((hid, ct), lambda o, c: (0, (2 * o + 1) * ncb + c)),
                  const((2, half, 1)),
                  pl.BlockSpec((1, ct), lambda o, c: (0, c)),
                  const((half, 1)), const((half, 1)), const((half, 1)), const((half, 1)),
                  const((half, half)), const((half, half))],
        out_specs=[pl.BlockSpec((1, 6, half, ct), lambda o, c: (o, 0, 0, c)),
                   pl.BlockSpec((1, 2, ct), lambda o, c: (o, 0, c))],
        out_shape=[jax.ShapeDtypeStruct((orders, 6, half, ch), F32),
                   jax.ShapeDtypeStruct((orders, 2, ch), F32)],
        scratch_shapes=[pltpu.VMEM((2, half, hid), F32)],
        compiler_params=_cparams(("arbitrary", "arbitrary")),
        name="hyena_filter",
    )(jnp.asarray(zt), pos_w1, pos_b1, freq1, pos_w2, pos_b2, freq2, pos_w3, pos_w3, jnp.asarray(t_col),
      jnp.asarray(deltas), jnp.asarray(sgn), jnp.asarray(wk), jnp.asarray(np.cos(ang).astype(np.float32)),
      jnp.asarray(np.sin(ang).astype(np.float32)), dft_c, dft_s)


def _split_rows(x_ref):
    half = x_ref.shape[1] // 2
    return x_ref[0, pl.ds(0, half, stride=2), :], x_ref[0, pl.ds(1, half, stride=2), :]


def _short_conv_split(p_ref, w_ref, b_ref):
    pe, po = _split_rows(p_ref)
    half = pe.shape[0]
    pos = lax.broadcasted_iota(jnp.int32, pe.shape, 0) % (GRID_W // 2)
    before_even = jnp.where(pos == 0, 0.0, pltpu.roll(po, 1, axis=0))
    after_odd = jnp.where(pos == GRID_W // 2 - 1, 0.0, pltpu.roll(pe, half - 1, axis=0))
    w0, w1, w2 = w_ref[0:1, :], w_ref[1:2, :], w_ref[2:3, :]
    return (b_ref[...] + before_even * w0 + pe * w1 + po * w2,
            b_ref[...] + pe * w0 + po * w1 + after_odd * w2)


def _hyena_kernel(u_ref, g_ref, cwu_ref, cbu_ref, cwg_ref, cbg_ref, fwd_ref, inv_ref, p_ref, hm_ref,
                  bias_ref, sgn_ref, o_ref, spec_ref, *, conv_u, freq_tile):
    half = inv_ref.shape[0]
    ct = o_ref.shape[2]
    ue, uo = _short_conv_split(u_ref, cwu_ref, cbu_ref) if conv_u else _split_rows(u_ref)
    ub = jnp.concatenate([ue, uo], axis=1).astype(BF16)
    for ft in range(half // freq_tile):
        lo, hi = ft * freq_tile, (ft + 1) * freq_tile
        dc = _dot(fwd_ref[lo:hi, :], ub)
        ds = _dot(fwd_ref[half + lo:half + hi, :], ub)
        er, orr, es, os_ = dc[:, :ct], dc[:, ct:], ds[:, :ct], ds[:, ct:]
        p1r, p1i, p2r, p2i, p3r, p3i = (p_ref[0, i, lo:hi, :] for i in range(6))
        ar = er * p1r + es * p1i + orr * p2r + os_ * p2i
        br = er * p3r + es * p3i + orr * p1r + os_ * p1i
        ai = es * p1r - er * p1i + os_ * p2r - orr * p2i
        bi = es * p3r - er * p3i + os_ * p1r - orr * p1i
        spec_ref[lo:hi, :] = jnp.concatenate([ar, br], axis=1).astype(BF16)
        spec_ref[half + lo:half + hi, :] = jnp.concatenate([ai, bi], axis=1).astype(BF16)
    y = _dot(inv_ref[...], spec_ref[...])
    sgn = sgn_ref[...]
    en = jnp.sum(ue * sgn, axis=0, keepdims=True)
    on = jnp.sum(uo * sgn, axis=0, keepdims=True)
    hm_r, hm_i = hm_ref[0, 0:1, :], hm_ref[0, 1:2, :]
    bias = bias_ref[0]
    ge, go = _short_conv_split(g_ref, cwg_ref, cbg_ref)
    o_ref[0, pl.ds(0, half, stride=2), :] = ge * (y[:, :ct] + ue * bias + sgn * (en * hm_r + on * hm_i))
    o_ref[0, pl.ds(1, half, stride=2), :] = go * (y[:, ct:] + uo * bias + sgn * (on * hm_r - en * hm_i))


def _hyena_conv(u, u_col_blk, p_hy, gate_col_blk, conv_w, conv_b, dft_c, dft_s, ptab, hm, bias, order,
                conv_u, ct):
    bsz, seq, _ = p_hy.shape
    half = seq // 2
    ch = ptab.shape[3]
    ncb = ch // ct
    sgn = jnp.asarray(np.where(np.arange(half) % 2 == 0, 1.0, -1.0).astype(np.float32)[:, None])
    taps = conv_w.shape[0]
    cbias = conv_b.reshape(1, -1)
    const = lambda shape: pl.BlockSpec(shape, lambda c, b: (0,) * len(shape))
    return pl.pallas_call(
        functools.partial(_hyena_kernel, conv_u=conv_u, freq_tile=half // 4),
        grid=(ncb, bsz),
        in_specs=[pl.BlockSpec((1, seq, ct), lambda c, b: (b, 0, u_col_blk * ncb + c)),
                  pl.BlockSpec((1, seq, ct), lambda c, b: (b, 0, gate_col_blk * ncb + c)),
                  pl.BlockSpec((taps, ct), lambda c, b: (0, u_col_blk * ncb + c)),
                  pl.BlockSpec((1, ct), lambda c, b: (0, u_col_blk * ncb + c)),
                  pl.BlockSpec((taps, ct), lambda c, b: (0, gate_col_blk * ncb + c)),
                  pl.BlockSpec((1, ct), lambda c, b: (0, gate_col_blk * ncb + c)),
                  const((2 * half, half)), const((half, 2 * half)),
                  pl.BlockSpec((1, 6, half, ct), lambda c, b: (order, 0, 0, c)),
                  pl.BlockSpec((1, 2, ct), lambda c, b: (order, 0, c)),
                  pl.BlockSpec((1, 1, ct), lambda c, b: (order, 0, c)),
                  const((half, 1))],
        out_specs=pl.BlockSpec((1, seq, ct), lambda c, b: (b, 0, c)),
        out_shape=jax.ShapeDtypeStruct((bsz, seq, ch), F32),
        scratch_shapes=[pltpu.VMEM((2 * half, 2 * ct), BF16)],
        compiler_params=_cparams(("arbitrary", "arbitrary")),
        name=f"hyena_conv{order}",
    )(u, p_hy, conv_w, cbias, conv_w, cbias, jnp.concatenate([dft_c, dft_s], axis=0),
      jnp.concatenate([dft_c, dft_s], axis=1), ptab, hm, bias.reshape(bias.shape[0], 1, ch), sgn)


def _scan_consts():
    c, blk = SCAN_CHUNK, SCAN_BLOCK
    r = lax.broadcasted_iota(jnp.int32, (c, c), 0)
    s = lax.broadcasted_iota(jnp.int32, (c, c), 1)
    row = lax.broadcasted_iota(jnp.int32, (c, LANES), 0)
    fb = FAST_BLOCK

    def pair_mask(size, rev):
        same = (r // (2 * size)) == (s // (2 * size))
        hi_r, hi_s = r % (2 * size) >= size, s % (2 * size) >= size
        return same & ((~hi_r & hi_s) if rev else (hi_r & ~hi_s))

    return dict(
        cum_f=(s <= r).astype(F32), cum_b=(s >= r).astype(F32),
        m32_f=(r >= c // 2) & (s < c // 2), m32_b=(r < c // 2) & (s >= c // 2),
        m16_f=pair_mask(blk, False), m16_b=pair_mask(blk, True),
        m8_f=pair_mask(fb, False), m8_b=pair_mask(fb, True),
        md_f=(r // fb == s // fb) & (s <= r), md_b=(r // fb == s // fb) & (s >= r),
        sub=lax.broadcasted_iota(jnp.int32, (fb, LANES), 0),
        row=row, ones=jnp.ones((LANES, LANES), BF16))


def _state_update(st_ref, k, v, g, g_edge):
    kd = (k * jnp.exp(g_edge - g)).astype(BF16)
    st_ref[...] = st_ref[...] * jnp.exp(g_edge) + _dot_tn(v.astype(BF16), kd)


def _chunk_cumsum(logf, cst, reverse):
    return jnp.dot(cst["cum_b" if reverse else "cum_f"], logf, preferred_element_type=F32, precision=HIGHEST)


def _scan_chunk(st_ref, q, k, v, logf, cst, reverse):
    c, blk = SCAN_CHUNK, SCAN_BLOCK
    row = cst["row"]
    g = _chunk_cumsum(logf, cst, reverse)
    o = _dot_nt((q * jnp.exp(g)).astype(BF16), st_ref[...].astype(BF16))
    half = c // 2
    if not reverse:
        ref32 = g[half - 1:half, :]
        ref16 = jnp.where(row < half, g[blk - 1:blk, :], g[half + blk - 1:half + blk, :])
        t32, s32 = row >= half, row < half
        t16, s16 = row % (2 * blk) >= blk, row % (2 * blk) < blk
        g_edge = g[c - 1:c, :]
    else:
        ref32 = g[half:half + 1, :]
        ref16 = jnp.where(row < half, g[blk:blk + 1, :], g[half + blk:half + blk + 1, :])
        t32, s32 = row < half, row >= half
        t16, s16 = row % (2 * blk) < blk, row % (2 * blk) >= blk
        g_edge = g[0:1, :]
    a32 = jnp.where(t32, q * jnp.exp(jnp.minimum(g - ref32, 0.0)), 0.0).astype(BF16)
    b32 = jnp.where(s32, k * jnp.exp(jnp.minimum(ref32 - g, 0.0)), 0.0).astype(BF16)
    a16 = jnp.where(t16, q * jnp.exp(jnp.minimum(g - ref16, 0.0)), 0.0).astype(BF16)
    b16 = jnp.where(s16, k * jnp.exp(jnp.minimum(ref16 - g, 0.0)), 0.0).astype(BF16)
    att = (jnp.where(cst["m32_b" if reverse else "m32_f"], _dot_nt(a32, b32), 0.0)
           + jnp.where(cst["m16_b" if reverse else "m16_f"], _dot_nt(a16, b16), 0.0))
    o = o + _dot(att.astype(BF16), v.astype(BF16))
    pos = row % blk
    for d in range(blk):
        shift = d if not reverse else (c - d) % c
        if d == 0:
            ks, gs, vs = k, g, v
        else:
            ks = pltpu.roll(k, shift, axis=0)
            gs = pltpu.roll(g, shift, axis=0)
            vs = pltpu.roll(v, shift, axis=0)
        valid = (pos >= d) if not reverse else (pos + d < blk)
        w = jnp.where(valid, q * ks * jnp.exp(jnp.minimum(g - gs, 0.0)), 0.0)
        o = o + _dot(w.astype(BF16), cst["ones"]) * vs
    _state_update(st_ref, k, v, g, g_edge)
    return o


def _chunk_prepare(q, k, v, logf, cst, reverse):
    c, fb = SCAN_CHUNK, FAST_BLOCK
    nb = c // fb
    d = "b" if reverse else "f"
    sub = cst["sub"]
    blocks = lambda x: [x[fb * j:fb * (j + 1), :] for j in range(nb)]
    qs, ks = blocks(q), blocks(k)
    ps = []
    for p in blocks(logf):
        for s in (1, 2, 4):
            if reverse:
                p = p + jnp.where(sub + s < fb, pltpu.roll(p, fb - s, axis=0), 0.0)
            else:
                p = p + jnp.where(sub >= s, pltpu.roll(p, s, axis=0), 0.0)
        ps.append(p)
    last = 0 if reverse else fb - 1
    tot = [jnp.broadcast_to(p[last:last + 1, :], p.shape) for p in ps]
    decay = [jnp.exp(t) for t in tot]
    memo = {}

    def span(lo, hi):
        if hi <= lo:
            return None
        if (lo, hi) not in memo:
            if hi == nb and lo > 0:
                inner, last = span(lo + 1, hi), decay[lo]
            else:
                inner, last = span(lo, hi - 1), decay[hi - 1]
            memo[lo, hi] = last if inner is None else inner * last
        return memo[lo, hi]

    scale = lambda x, f: x if f is None else x * f
    qe = [qs[j] * jnp.exp(ps[j]) for j in range(nb)]
    kb = [ks[j] * jnp.exp(tot[j] - ps[j]) for j in range(nb)]
    zero = jnp.zeros_like(tot[0])
    cat = lambda parts: jnp.concatenate(parts, axis=0).astype(BF16)

    def level(n):
        a, b = [], []
        for j in range(nb):
            mid = (j // (2 * n)) * 2 * n + n
            later = j >= mid
            if later != reverse:
                a.append(scale(qe[j], span(j + 1, mid) if reverse else span(mid, j)))
                b.append(zero)
            else:
                a.append(zero)
                b.append(scale(kb[j], span(mid, j) if reverse else span(j + 1, mid)))
        return _dot_nt(cat(a), cat(b))

    bd = cat([ks[j] * jnp.exp(-ps[j]) for j in range(nb)])
    att = (level(nb // 2)
           + jnp.where(cst["m16_" + d], level(nb // 4), 0.0)
           + jnp.where(cst["m8_" + d], level(nb // 8), 0.0)
           + jnp.where(cst["md_" + d], _dot_nt(cat(qe), bd), 0.0))
    before = lambda j: span(j + 1, nb) if reverse else span(0, j)
    after = lambda j: span(0, j) if reverse else span(j + 1, nb)
    qd = cat([scale(qe[j], before(j)) for j in range(nb)])
    kd = cat([scale(kb[j], after(j)) for j in range(nb)])
    vb = v.astype(BF16)
    return att.astype(BF16), qd, vb, _dot_tn(vb, kd), span(0, nb)[0:1, :]


def _chunk_finish(st_ref, prepared):
    att, qd, vb, kv, decay = prepared
    o = _dot_nt(qd, st_ref[...].astype(BF16)) + _dot(att, vb)
    st_ref[...] = st_ref[...] * decay + kv
    return o


def _hgrn_kernel(q_ref, i_ref, kf_ref, kb_ref, g_ref, lf_ref, lbw_ref, ic_ref, zfc_ref, zbc_ref, lb_ref, ng_ref,
                 o_ref, of_ref, ob_ref, st_ref):
    seq = q_ref.shape[1]
    ctx_len = ic_ref.shape[1]
    hd = ng_ref.shape[1]
    heads = q_ref.shape[2] // hd
    c = SCAN_CHUNK
    n_chunks, n_ctx = seq // c, ctx_len // c
    cst = _scan_consts()
    lb = _gate_floor(lb_ref[...])
    lanes = [slice(h * hd, (h + 1) * hd) for h in range(heads)]
    st_f = [st_ref.at[2 * h] for h in range(heads)]
    st_b = [st_ref.at[2 * h + 1] for h in range(heads)]

    st_ref[...] = jnp.zeros_like(st_ref)

    def ctx_body(j, carry):
        rf = pl.multiple_of(j * c, c)
        rb = pl.multiple_of((n_ctx - 1 - j) * c, c)
        for h, ln in enumerate(lanes):
            for st, z_ref, r, lb_row, rev in ((st_f[h], zfc_ref, rf, lb[0:1, ln], False),
                                              (st_b[h], zbc_ref, rb, lb[1:2, ln], True)):
                k, logf = _forget_gate(z_ref[0, pl.ds(r, c), ln], lb_row)
                *_, kv, decay = _chunk_prepare(jnp.zeros_like(k), k, ic_ref[0, pl.ds(r, c), ln], logf, cst, rev)
                st[...] = st[...] * decay + kv
        return carry

    lax.fori_loop(0, n_ctx, ctx_body, 0)

    def rows(j):
        if isinstance(j, int):
            return j * c, (n_chunks - 1 - j) * c
        return pl.multiple_of(j * c, c), pl.multiple_of((n_chunks - 1 - j) * c, c)

    def operands(r, ln, k_ref, logf_ref):
        sl = (0, pl.ds(r, c), ln)
        return q_ref[sl], k_ref[sl], i_ref[sl], logf_ref[sl]

    def direct_body(j, carry):
        rf, rb = rows(j)
        for h, ln in enumerate(lanes):
            of_ref[pl.ds(rf, c), ln] = _scan_chunk(st_f[h], *operands(rf, ln, kf_ref, lf_ref), cst, False)
            ob_ref[pl.ds(rb, c), ln] = _scan_chunk(st_b[h], *operands(rb, ln, kb_ref, lbw_ref), cst, True)
        return carry

    def prepare(j):
        rf, rb = rows(j)
        out = []
        for ln in lanes:
            out.append(_chunk_prepare(*operands(rf, ln, kf_ref, lf_ref), cst, False))
            out.append(_chunk_prepare(*operands(rb, ln, kb_ref, lbw_ref), cst, True))
        return tuple(out)

    def finish(j, prepared):
        rf, rb = rows(j)
        for h, ln in enumerate(lanes):
            of_ref[pl.ds(rf, c), ln] = _chunk_finish(st_f[h], prepared[2 * h])
            ob_ref[pl.ds(rb, c), ln] = _chunk_finish(st_b[h], prepared[2 * h + 1])

    def factored_body(j, prepared):
        upcoming = prepare(j + 1)
        finish(j, prepared)
        return upcoming

    bounded = jnp.min(lb) >= math.exp(-MAX_FAST_EXPONENT / FAST_BLOCK)

    @pl.when(bounded)
    def _():
        looped = (n_chunks - 1) // SCAN_UNROLL * SCAN_UNROLL
        prepared = lax.fori_loop(0, looped, factored_body, prepare(0), unroll=SCAN_UNROLL)
        for j in range(looped, n_chunks - 1):
            prepared = factored_body(j, prepared)
        finish(n_chunks - 1, prepared)

    @pl.when(jnp.logical_not(bounded))
    def _():
        lax.fori_loop(0, n_chunks, direct_body, 0)

    for ln in lanes:
        o = of_ref[:, ln] + ob_ref[:, ln]
        y = o * lax.rsqrt(jnp.mean(o * o, axis=-1, keepdims=True) + EPS) * ng_ref[...]
        o_ref[0, :, ln] = y * _silu(g_ref[0, :, ln])


def _hgrn(p, col0, logf, pc, hg_lb, norm_g, heads_per_step):
    bsz, seq, _ = p.shape
    ctx_len = pc.shape[1]
    hd = norm_g.shape[0]
    w = heads_per_step * hd
    steps = HG_HEADS // heads_per_step
    lat = lambda grp: pl.BlockSpec((1, seq, w), lambda b, h: (b, 0, (col0 + grp * HG_HEADS) // heads_per_step + h))
    lgf = lambda d: pl.BlockSpec((1, seq, w), lambda b, h: (b, 0, d * steps + h))
    ctx = lambda grp: pl.BlockSpec((1, ctx_len, w), lambda b, h: (b, 0, grp * steps + h))
    return pl.pallas_call(
        _hgrn_kernel,
        grid=(bsz, steps),
        in_specs=[lat(0), lat(1), lat(2), lat(3), lat(4), lgf(0), lgf(1), ctx(0), ctx(1), ctx(2),
                  pl.BlockSpec((hg_lb.shape[0], 2, w), lambda b, h: (0, 0, h)),
                  pl.BlockSpec((1, hd), lambda b, h: (0, 0))],
        out_specs=pl.BlockSpec((1, seq, w), lambda b, h: (b, 0, h)),
        out_shape=jax.ShapeDtypeStruct((bsz, seq, HG_HEADS * hd), F32),
        scratch_shapes=[pltpu.VMEM((seq, w), F32), pltpu.VMEM((seq, w), F32),
                        pltpu.VMEM((2 * heads_per_step, hd, hd), F32)],
        compiler_params=_cparams(("arbitrary", "arbitrary")),
        name="hgrn_scan",
    )(p, p, p, p, p, logf, logf, pc, pc, pc, hg_lb, norm_g.reshape(1, hd))


def _outproj_kernel(yhy_ref, yhg_ref, x_ref, g1_ref, sh2_ref, sc2_ref, hyg_ref, n2g_ref, wt_ref, wb_ref, rw_ref,
                    x1_ref, h2_ref, aff_ref):
    yhy = yhy_ref[0]
    yn = yhy * lax.rsqrt(jnp.mean(yhy * yhy, axis=-1, keepdims=True) + EPS) * hyg_ref[...]
    mix = _dot(yn.astype(BF16), wt_ref[...]) + _dot(yhg_ref[0].astype(BF16), wb_ref[...])
    x1 = x_ref[0] + g1_ref[0] * mix
    x1_ref[0] = x1
    h2 = (x1 * lax.rsqrt(jnp.mean(x1 * x1, axis=-1, keepdims=True) + EPS) * n2g_ref[...]
          * (1.0 + sc2_ref[0]) + sh2_ref[0])
    h_hi = h2.astype(BF16)
    h2_ref[0] = h_hi
    h_lo = (h2 - h_hi.astype(F32)).astype(BF16)
    rw = rw_ref[...]
    r_hi = rw.astype(BF16)
    r_lo = (rw - r_hi.astype(F32)).astype(BF16)
    ne = rw.shape[0]
    both = _dot_nt(jnp.concatenate([r_hi, r_lo], axis=0), h_hi)
    logits = both[:ne] + both[ne:] + _dot_nt(r_hi, h_lo)
    ex = jnp.exp(logits - jnp.max(logits, axis=0, keepdims=True))
    aff_ref[0] = ex / jnp.sum(ex, axis=0, keepdims=True)


def _outproj(y_hy, y_hg, x, mods3, hy_norm_g, norm2_g, w_top, w_bot, router_wt, tm):
    bsz, seq, d = x.shape
    wh = y_hy.shape[2]
    ne = router_wt.shape[0]
    mod = lambda k: pl.BlockSpec((1, 1, d), lambda b, i: (b, 0, k))
    const = lambda shape: pl.BlockSpec(shape, lambda b, i: (0,) * len(shape))
    return pl.pallas_call(
        _outproj_kernel,
        grid=(bsz, seq // tm),
        in_specs=[pl.BlockSpec((1, tm, wh), lambda b, i: (b, i, 0)),
                  pl.BlockSpec((1, tm, wh), lambda b, i: (b, i, 0)),
                  pl.BlockSpec((1, tm, d), lambda b, i: (b, i, 0)),
                  mod(2), mod(3), mod(4),
                  const((1, wh)), const((1, d)), const((wh, d)), const((wh, d)), const((ne, d))],
        out_specs=[pl.BlockSpec((1, tm, d), lambda b, i: (b, i, 0)),
                   pl.BlockSpec((1, tm, d), lambda b, i: (b, i, 0)),
                   pl.BlockSpec((1, ne, tm), lambda b, i: (b, 0, i))],
        out_shape=[jax.ShapeDtypeStruct((bsz, seq, d), F32),
                   jax.ShapeDtypeStruct((bsz, seq, d), BF16),
                   jax.ShapeDtypeStruct((bsz, ne, seq), F32)],
        compiler_params=_cparams(("arbitrary", "arbitrary")),
        name="outproj_router",
    )(y_hy, y_hg, x, mods3, mods3, mods3, hy_norm_g.reshape(1, wh), norm2_g.reshape(1, d), w_top, w_bot, router_wt)


def _select_kernel(a_ref, slot_ref, *, cap, rows_per_expert):
    nr = a_ref.shape[1]
    affs = [a_ref[i] for i in range(a_ref.shape[0])]
    r = lax.broadcasted_iota(jnp.int32, (nr, nr), 0)
    s = lax.broadcasted_iota(jnp.int32, (nr, nr), 1)
    same = (r // rows_per_expert) == (s // rows_per_expert)
    grp = same.astype(BF16)
    before = (same & (s < r)).astype(BF16)
    li = lax.broadcasted_iota(jnp.int32, (LANES, LANES), 0)
    lj = lax.broadcasted_iota(jnp.int32, (LANES, LANES), 1)
    ones = jnp.ones((LANES, LANES), BF16)
    strict = (li < lj).astype(BF16)

    def row_total(ind):
        return _dot(ind, ones).astype(BF16)

    def expert_total(ind):
        return _dot(grp, row_total(ind))

    def prefix(ind):
        return _dot(ind, strict) + _dot(before, row_total(ind))

    ne = nr // rows_per_expert

    def count_at_least(a, cand):
        ind = jnp.where(a >= cand, 1.0, 0.0).reshape(ne, rows_per_expert, LANES)
        tot = jnp.sum(jnp.sum(ind, axis=1, keepdims=True), axis=2, keepdims=True)
        return jnp.broadcast_to(tot, ind.shape).reshape(nr, LANES)

    def search(i, thrs):
        bit = jnp.left_shift(jnp.int32(1), 30 - i)
        out = []
        for a, t in zip(affs, thrs):
            cand = t | bit
            out.append(jnp.where(count_at_least(a, pltpu.bitcast(cand, F32)) >= cap, cand, t))
        return tuple(out)

    thrs = lax.fori_loop(0, 31, search, tuple(jnp.zeros((nr, LANES), jnp.int32) for _ in affs))
    for i, (a, thr) in enumerate(zip(affs, thrs)):
        thr_f = pltpu.bitcast(thr, F32)
        gt = a > thr_f
        eq = a == thr_f
        need = cap - expert_total(jnp.where(gt, 1.0, 0.0).astype(BF16))
        eq_rank = prefix(jnp.where(eq, 1.0, 0.0).astype(BF16))
        sel = gt | (eq & (eq_rank < need))
        rank = prefix(jnp.where(sel, 1.0, 0.0).astype(BF16))
        slot_ref[i] = jnp.where(sel, rank, -1.0)


def _select(aff_t, cap, bt):
    bsz, ne, n = aff_t.shape
    rpe = n // LANES
    a2 = aff_t.reshape(bsz, ne * rpe, LANES)
    slot = pl.pallas_call(
        functools.partial(_select_kernel, cap=cap, rows_per_expert=rpe),
        grid=(bsz // bt,),
        in_specs=[pl.BlockSpec((bt, ne * rpe, LANES), lambda b: (b, 0, 0))],
        out_specs=pl.BlockSpec((bt, ne * rpe, LANES), lambda b: (b, 0, 0)),
        out_shape=jax.ShapeDtypeStruct((bsz, ne * rpe, LANES), F32),
        compiler_params=_cparams(("arbitrary",)),
        name="ec_select",
    )(a2)
    return slot.reshape(bsz, ne, n)


def _gather_kernel(slot_ref, h_ref, o_ref, *, cap):
    slot = slot_ref[0, 0]
    j = lax.broadcasted_iota(jnp.int32, (cap, slot.shape[1]), 0).astype(F32)
    onehot = jnp.where(slot == j, 1.0, 0.0).astype(BF16)
    o_ref[0, 0] = _dot(onehot, h_ref[0]).astype(BF16)


def _gather(slot4, h2, cap):
    bsz, ne, _, n = slot4.shape
    d = h2.shape[2]
    return pl.pallas_call(
        functools.partial(_gather_kernel, cap=cap),
        grid=(bsz, ne),
        in_specs=[pl.BlockSpec((1, 1, 1, n), lambda b, e: (b, e, 0, 0)),
                  pl.BlockSpec((1, n, d), lambda b, e: (b, 0, 0))],
        out_specs=pl.BlockSpec((1, 1, cap, d), lambda b, e: (b, e, 0, 0)),
        out_shape=jax.ShapeDtypeStruct((bsz, ne, cap, d), BF16),
        compiler_params=_cparams(("arbitrary", "arbitrary")),
        name="ec_gather",
    )(slot4, h2)


def _expert_kernel(x_ref, w1_ref, w3_ref, w2_ref, o_ref, acc_ref):
    f = pl.program_id(2)
    bt, _, cap, d = x_ref.shape
    x = x_ref[...].reshape(bt * cap, d)

    @pl.when(f == 0)
    def _():
        acc_ref[...] = jnp.zeros_like(acc_ref)

    w1, w3, w2 = w1_ref[0].astype(BF16), w3_ref[0].astype(BF16), w2_ref[0].astype(BF16)
    rows = bt * cap // EXPERT_ROW_TILES
    for r in range(EXPERT_ROW_TILES):
        sl = slice(r * rows, (r + 1) * rows)
        a = _dot(x[sl], w1)
        b = _dot(x[sl], w3)
        acc_ref[sl, :] += _dot((_silu(a) * b).astype(BF16), w2)

    @pl.when(f == pl.num_programs(2) - 1)
    def _():
        o_ref[...] = acc_ref[...].astype(BF16).reshape(bt, 1, cap, d)


def _experts(xg, w1, w3, w2, bt, ft):
    bsz, ne, cap, d = xg.shape
    ff = w1.shape[2]
    return pl.pallas_call(
        _expert_kernel,
        grid=(ne, bsz // bt, ff // ft),
        in_specs=[pl.BlockSpec((bt, 1, cap, d), lambda e, m, f: (m, e, 0, 0)),
                  pl.BlockSpec((1, d, ft), lambda e, m, f: (e, 0, f)),
                  pl.BlockSpec((1, d, ft), lambda e, m, f: (e, 0, f)),
                  pl.BlockSpec((1, ft, d), lambda e, m, f: (e, f, 0))],
        out_specs=pl.BlockSpec((bt, 1, cap, d), lambda e, m, f: (m, e, 0, 0)),
        out_shape=jax.ShapeDtypeStruct((bsz, ne, cap, d), BF16),
        scratch_shapes=[pltpu.VMEM((bt * cap, d), F32)],
        compiler_params=_cparams(("arbitrary", "arbitrary", "arbitrary")),
        name="ec_experts",
    )(xg, w1, w3, w2)


def _combine_kernel(y_ref, slot_ref, aff_ref, x1_ref, g2_ref, fg_ref, o_ref, *, cap):
    _, ne, _, d = y_ref.shape
    slot = slot_ref[0]
    aff = aff_ref[0]
    j = lax.broadcasted_iota(jnp.int32, (slot.shape[0], cap), 1).astype(F32)
    gated = jnp.concatenate([jnp.where(slot[:, e:e + 1] == j, aff[:, e:e + 1], 0.0).astype(BF16)
                             for e in range(ne)], axis=1)
    moe = _dot(gated, y_ref[0].reshape(ne * cap, d))
    x2 = x1_ref[0] + g2_ref[0] * moe
    o_ref[0] = x2 * lax.rsqrt(jnp.mean(x2 * x2, axis=-1, keepdims=True) + EPS) * fg_ref[...]


def _combine(y, slot_nt, aff_nt, x1, mods3, final_g, cap, tn):
    bsz, ne, _, d = y.shape
    n = x1.shape[1]
    return pl.pallas_call(
        functools.partial(_combine_kernel, cap=cap),
        grid=(bsz, n // tn),
        in_specs=[pl.BlockSpec((1, ne, cap, d), lambda b, t: (b, 0, 0, 0)),
                  pl.BlockSpec((1, tn, ne), lambda b, t: (b, t, 0)),
                  pl.BlockSpec((1, tn, ne), lambda b, t: (b, t, 0)),
                  pl.BlockSpec((1, tn, d), lambda b, t: (b, t, 0)),
                  pl.BlockSpec((1, 1, d), lambda b, t: (b, 0, 5)),
                  pl.BlockSpec((1, d), lambda b, t: (0, 0))],
        out_specs=pl.BlockSpec((1, tn, d), lambda b, t: (b, t, 0)),
        out_shape=jax.ShapeDtypeStruct((bsz, n, d), F32),
        compiler_params=_cparams(("arbitrary", "arbitrary")),
        name="ec_combine",
    )(y, slot_nt, aff_nt, x1, mods3, final_g.reshape(1, d))


def kernel(x, c, ctx, c_ctx, ada_w, ada_b, norm1_g, w_in, hy_conv_w, hy_conv_b, hy_pos_w1, hy_pos_b1, hy_freq1,
           hy_pos_w2, hy_pos_b2, hy_freq2, hy_pos_w3, hy_bias, hy_norm_g, hg_lb, hg_norm_g, w_out, norm2_g,
           router_w, exp_w1, exp_w3, exp_w2, final_g):
    bsz, seq, d = x.shape
    hyw = hy_norm_g.shape[1]
    hgw = w_in.shape[2] - 3 * hyw
    hg = hgw // 5
    assert hyw == hg, "column groups of the input projection are addressed with one width"
    cap = CAPACITY_FACTOR * seq // N_EXPERTS

    rows = -(-(bsz + 1) // 8) * 8
    s_rows = jnp.zeros((rows, d), F32).at[:bsz].set(c).at[bsz].set(c_ctx)
    mods3 = _ada(s_rows, ada_w[0], ada_b[0]).reshape(rows, 1, 6 * d)

    w_bf = w_in[0].astype(BF16)
    p, logf = _inproj_mix(x, mods3, norm1_g[0], w_bf, hg_lb, INPROJ_COL_TILE, hg, 3, 5, 6, (hg // HG_HEADS) ** -0.5)
    w_ctx = w_bf[:, 3 * hyw + hg:3 * hyw + 4 * hg]
    pc = _inproj(ctx, mods3, lambda b: bsz, norm1_g[0], w_ctx, 0, 1, ctx.shape[1], 3 * hg)

    dft_c, dft_s = _dft_tables(seq)
    ct = HYENA_CH_TILE
    ptab, hm = _hyena_filters(seq, hy_pos_w1[0], hy_pos_b1[0], hy_freq1[0], hy_pos_w2[0], hy_pos_b2[0],
                              hy_freq2[0], hy_pos_w3[0], dft_c, dft_s, FILTER_CH_TILE)
    z = _hyena_conv(p, 0, p, 1, hy_conv_w[0], hy_conv_b[0], dft_c, dft_s, ptab, hm, hy_bias[0], 0, True, ct)
    y_hy = _hyena_conv(z, 0, p, 2, hy_conv_w[0], hy_conv_b[0], dft_c, dft_s, ptab, hm, hy_bias[0], 1, False, ct)

    y_hg = _hgrn(p, 3 * hyw // (hg // HG_HEADS), logf, pc, hg_lb, hg_norm_g[0], SCAN_HEADS_PER_STEP)

    w_o = w_out[0].astype(BF16)
    x1, h2, aff_t = _outproj(y_hy, y_hg, x, mods3, hy_norm_g[0], norm2_g[0], w_o[:hyw], w_o[hyw:],
                             router_w[0].T, OUTPROJ_ROW_TILE)

    slot = _select(aff_t, cap, SELECT_SAMPLES)
    xg = _gather(slot.reshape(bsz, N_EXPERTS, 1, seq), h2, cap)
    y = _experts(xg, exp_w1[0], exp_w3[0], exp_w2[0], EXPERT_SAMPLES, EXPERT_FF_TILE)
    return _combine(y, jnp.swapaxes(slot, 1, 2), jnp.swapaxes(aff_t, 1, 2), x1, mods3, final_g, cap,
                    COMBINE_ROW_TILE)
```

```python
import functools
import math

import numpy as np
import jax
import jax.numpy as jnp
from jax import lax
from jax.experimental import pallas as pl
from jax.experimental.pallas import tpu as pltpu

F32 = jnp.float32
BF16 = jnp.bfloat16
HIGHEST = lax.Precision.HIGHEST

GRID_W = 64
HG_HEADS = 4
HY_BANDS = 16
HY_DECAY_TARGET = 1e-2
HY_FAST_DECAY = 0.3
HY_SLOW_DECAY = 1.5
HY_WINDOW_SHIFT = 0.05
N_EXPERTS = 16
CAPACITY_FACTOR = 2
EPS = 1e-6

SCAN_CHUNK = 64
SCAN_BLOCK = 16
FAST_BLOCK = 8
MAX_FAST_EXPONENT = 80.0
SCAN_UNROLL = 2
SCAN_HEADS_PER_STEP = 2
LANES = 128
VMEM_LIMIT = 56 * 1024 * 1024

INPROJ_COL_TILE = 1024
INPROJ_ROW_TILES = 4
FILTER_CH_TILE = 256
HYENA_CH_TILE = LANES
OUTPROJ_ROW_TILE = 1024
SELECT_SAMPLES = 8
EXPERT_SAMPLES = 8
EXPERT_FF_TILE = 512
EXPERT_ROW_TILES = 2
COMBINE_ROW_TILE = 512


def _cparams(sem):
    return pltpu.CompilerParams(dimension_semantics=sem, vmem_limit_bytes=VMEM_LIMIT)


def _dot(a, b):
    return jnp.dot(a, b, preferred_element_type=F32)


def _dot_nt(a, b):
    return lax.dot_general(a, b, (((1,), (1,)), ((), ())), preferred_element_type=F32)


def _dot_tn(a, b):
    return lax.dot_general(a, b, (((0,), (0,)), ((), ())), preferred_element_type=F32)


def _silu(x):
    return x * jax.nn.sigmoid(x)


def _ada_kernel(s_ref, w_ref, b_ref, o_ref):
    s = _silu(s_ref[...])
    o_ref[...] = jnp.dot(s, w_ref[...], preferred_element_type=F32, precision=HIGHEST) + b_ref[...]


def _ada(s_rows, ada_w, ada_b):
    rows, d = s_rows.shape
    n = ada_w.shape[1]
    tn = 1024
    return pl.pallas_call(
        _ada_kernel,
        grid=(n // tn,),
        in_specs=[pl.BlockSpec((rows, d), lambda j: (0, 0)),
                  pl.BlockSpec((d, tn), lambda j: (0, j)),
                  pl.BlockSpec((1, tn), lambda j: (0, j))],
        out_specs=pl.BlockSpec((rows, tn), lambda j: (0, j)),
        out_shape=jax.ShapeDtypeStruct((rows, n), F32),
        compiler_params=_cparams(("arbitrary",)),
        name="ada",
    )(s_rows, ada_w, ada_b.reshape(1, n))


def _inproj_kernel(x_ref, sh_ref, sc_ref, g_ref, w_ref, o_ref, h_scr):
    @pl.when(pl.program_id(2) == 0)
    def _():
        x = x_ref[0]
        y = x * lax.rsqrt(jnp.mean(x * x, axis=-1, keepdims=True) + EPS) * g_ref[...]
        h_scr[...] = (y * (1.0 + sc_ref[0]) + sh_ref[0]).astype(BF16)

    o_ref[0] = _dot(h_scr[...], w_ref[...])


def _inproj(x, mods3, mod_row, norm_g, w_bf, col_blk0, n_col_blk, tm, tn):
    bsz, seq, d = x.shape
    return pl.pallas_call(
        _inproj_kernel,
        grid=(bsz, seq // tm, n_col_blk),
        in_specs=[pl.BlockSpec((1, tm, d), lambda b, i, j: (b, i, 0)),
                  pl.BlockSpec((1, 1, d), lambda b, i, j: (mod_row(b), 0, 0)),
                  pl.BlockSpec((1, 1, d), lambda b, i, j: (mod_row(b), 0, 1)),
                  pl.BlockSpec((1, d), lambda b, i, j: (0, 0)),
                  pl.BlockSpec((d, tn), lambda b, i, j: (0, col_blk0 + j))],
        out_specs=pl.BlockSpec((1, tm, tn), lambda b, i, j: (b, i, j)),
        out_shape=jax.ShapeDtypeStruct((bsz, seq, n_col_blk * tn), F32),
        scratch_shapes=[pltpu.VMEM((tm, d), BF16)],
        compiler_params=_cparams(("arbitrary", "arbitrary", "arbitrary")),
        name="inproj",
    )(x, mods3, mods3, norm_g.reshape(1, d), w_bf)


def _gate_floor(lb_raw):
    e = jnp.exp(lb_raw - jnp.max(lb_raw, axis=0, keepdims=True))
    return e[0] / jnp.sum(e, axis=0)


def _forget_gate(z, lb):
    f = lb + (1.0 - lb) * jax.nn.sigmoid(z)
    return 1.0 - f, jnp.log(f)


def _inproj_mix_kernel(x_ref, sh_ref, sc_ref, g_ref, w_ref, lb_ref, o_ref, lf_ref, h_scr, *, n_col_blk, gw,
                       q_group, zf_group, zb_group, q_scale):
    j = pl.program_id(2)
    lb = _gate_floor(lb_ref[...])
    per_tile = o_ref.shape[2] // gw
    for jj in range(n_col_blk):
        @pl.when(j == jj)
        def _(jj=jj):
            rt = h_scr.shape[0] // INPROJ_ROW_TILES
            for loc in range(per_tile):
                grp = jj * per_tile + loc
                cols = slice(loc * gw, (loc + 1) * gw)
                for r in range(INPROJ_ROW_TILES):
                    rows = slice(r * rt, (r + 1) * rt)
                    if jj == 0 and loc == 0:
                        x = x_ref[0, rows, :]
                        y = x * lax.rsqrt(jnp.mean(x * x, axis=-1, keepdims=True) + EPS) * g_ref[...]
                        h_scr[rows, :] = (y * (1.0 + sc_ref[0]) + sh_ref[0]).astype(BF16)
                    part = _dot(h_scr[rows, :], w_ref[:, cols])
                    if grp == q_group:
                        part = _silu(part) * q_scale
                    elif grp in (zf_group, zb_group):
                        d = 0 if grp == zf_group else 1
                        part, lf_ref[0, rows, :] = _forget_gate(part, lb[d:d + 1, :])
                    o_ref[0, rows, cols] = part


def _inproj_mix(x, mods3, norm_g, w_bf, hg_lb, tn, gw, q_group, zf_group, zb_group, q_scale):
    bsz, seq, d = x.shape
    n_col_blk = w_bf.shape[1] // tn
    z_tile0 = zf_group * gw // tn
    assert zb_group * gw // tn == z_tile0 + 1 and zf_group * gw // tn != q_group * gw // tn
    return pl.pallas_call(
        functools.partial(_inproj_mix_kernel, n_col_blk=n_col_blk, gw=gw, q_group=q_group, zf_group=zf_group,
                          zb_group=zb_group, q_scale=q_scale),
        grid=(bsz, 1, n_col_blk),
        in_specs=[pl.BlockSpec((1, seq, d), lambda b, i, j: (b, 0, 0)),
                  pl.BlockSpec((1, 1, d), lambda b, i, j: (b, 0, 0)),
                  pl.BlockSpec((1, 1, d), lambda b, i, j: (b, 0, 1)),
                  pl.BlockSpec((1, d), lambda b, i, j: (0, 0)),
                  pl.BlockSpec((d, tn), lambda b, i, j: (0, j)),
                  pl.BlockSpec((hg_lb.shape[0], 2, gw), lambda b, i, j: (0, 0, 0))],
        out_specs=[pl.BlockSpec((1, seq, tn), lambda b, i, j: (b, 0, j)),
                   pl.BlockSpec((1, seq, gw), lambda b, i, j: (b, 0, jnp.maximum(j - z_tile0, 0)))],
        out_shape=[jax.ShapeDtypeStruct((bsz, seq, w_bf.shape[1]), F32),
                   jax.ShapeDtypeStruct((bsz, seq, 2 * gw), F32)],
        scratch_shapes=[pltpu.VMEM((seq, d), BF16)],
        compiler_params=_cparams(("arbitrary", "arbitrary", "arbitrary")),
        name="inproj_mix",
    )(x, mods3, mods3, norm_g.reshape(1, d), w_bf, hg_lb)


def _dft_tables(seq):
    half = seq // 2
    k = np.arange(half, dtype=np.int64)
    idx = (k[:, None] * k[None, :]) % seq
    ang = 2.0 * np.pi * np.arange(seq, dtype=np.float64) / seq
    cos_t = jnp.asarray(np.cos(ang)[idx].astype(np.float32))
    sin_t = jnp.asarray(np.sin(ang)[idx].astype(np.float32))
    return cos_t.astype(BF16), sin_t.astype(BF16)


def _pos_features(seq):
    t = np.linspace(0.0, 1.0, seq, dtype=np.float32)[:, None]
    w = (2.0 * math.pi * np.arange(seq, dtype=np.float32)[:, None] / seq).astype(np.float32)
    f = np.linspace(1e-4, HY_BANDS - 1, HY_BANDS, dtype=np.float32)[None, :]
    fw = (f * w).astype(np.float32)
    return np.concatenate([t, np.cos(fw), -np.sin(fw)], axis=-1).astype(np.float32)


def _filter_kernel(z_ref, w1_ref, b1_ref, f1_ref, w2_ref, b2_ref, f2_ref, w3f_ref, w3b_ref, t_ref, dl_ref,
                   sgn_ref, wk_ref, c2_ref, s2_ref, c_ref, s_ref, p_ref, hm_ref, hid_ref):
    half = c_ref.shape[0]
    n = 4 * half

    @pl.when((pl.program_id(0) == 0) & (pl.program_id(1) == 0))
    def _():
        for par in range(2):
            h = jnp.sin(f1_ref[...] * (jnp.dot(z_ref[par], w1_ref[...], preferred_element_type=F32,
                                               precision=HIGHEST) + b1_ref[...]))
            hid_ref[par] = jnp.sin(f2_ref[...] * (jnp.dot(h, w2_ref[...], preferred_element_type=F32,
                                                          precision=HIGHEST) + b2_ref[...]))

    taps = {}
    for par in range(2):
        h = hid_ref[par]
        window = jnp.exp(-t_ref[par] * dl_ref[...]) + HY_WINDOW_SHIFT
        taps["f", par] = jnp.dot(h, w3f_ref[...], preferred_element_type=F32, precision=HIGHEST) * window
        taps["b", par] = jnp.dot(h, w3b_ref[...], preferred_element_type=F32, precision=HIGHEST) * window
    row = lax.broadcasted_iota(jnp.int32, taps["b", 0].shape, 0)
    taps["b", 0] = jnp.where(row == 0, 0.0, taps["b", 0])
    inv = 1.0 / (sum(jnp.sum(jnp.abs(x), axis=0, keepdims=True) for x in taps.values()) + EPS)
    sgn = sgn_ref[...]
    spec = {}
    for key, x in taps.items():
        x = x * inv
        xb = x.astype(BF16)
        spec[key] = (_dot(c_ref[...], xb), _dot(s_ref[...], xb), jnp.sum(x * sgn, axis=0, keepdims=True))
    (fec, fes, fen), (foc, fos, fon) = spec["f", 0], spec["f", 1]
    (bec, bes, ben), (boc, bos, bon) = spec["b", 0], spec["b", 1]
    c2, s2 = c2_ref[...], s2_ref[...]
    wk = 2.0 * wk_ref[...]
    p_ref[0, 0] = (fec + bec) * wk
    p_ref[0, 1] = (bes - fes) * wk
    p_ref[0, 2] = (c2 * foc - s2 * fos + boc) * wk
    p_ref[0, 3] = (bos - c2 * fos - s2 * foc) * wk
    p_ref[0, 4] = (foc + c2 * boc - s2 * bos) * wk
    p_ref[0, 5] = (c2 * bos + s2 * boc - fos) * wk
    hm_ref[0] = jnp.concatenate([fen + ben, bon - fon], axis=0) * (2.0 / n)


def _hyena_filters(seq, pos_w1, pos_b1, freq1, pos_w2, pos_b2, freq2, pos_w3, dft_c, dft_s, ct):
    orders = 2
    half = seq // 2
    ch = pos_w3.shape[1] // (orders * 2)
    zt = _pos_features(seq)
    pdim = -(-zt.shape[1] // LANES) * LANES
    hid = -(-pos_w1.shape[1] // LANES) * LANES
    zt = np.pad(zt, ((0, 0), (0, pdim - zt.shape[1])))
    zt = np.stack([zt[0::2], zt[1::2]])
    pad2 = lambda a, r, c_: jnp.pad(a, ((0, r - a.shape[0]), (0, c_ - a.shape[1])))
    pos_w1 = pad2(pos_w1, pdim, hid)
    pos_w2 = pad2(pos_w2, hid, hid)
    pos_w3 = pad2(pos_w3, hid, pos_w3.shape[1])
    pos_b1, freq1, pos_b2, freq2 = (pad2(a.reshape(1, -1), 1, hid) for a in (pos_b1, freq1, pos_b2, freq2))
    t_col = np.linspace(0.0, 1.0, seq, dtype=np.float32)[:, None]
    t_col = np.stack([t_col[0::2], t_col[1::2]])
    min_decay = math.log(HY_DECAY_TARGET) / HY_SLOW_DECAY
    max_decay = math.log(HY_DECAY_TARGET) / HY_FAST_DECAY
    deltas = np.abs(np.linspace(min_decay, max_decay, ch, dtype=np.float32))[None, :]
    sgn = np.where(np.arange(half) % 2 == 0, 1.0, -1.0).astype(np.float32)[:, None]
    wk = np.full((half, 1), 2.0 / (2 * seq), np.float32)
    wk[0, 0] = 1.0 / (2 * seq)
    ang = 2.0 * np.pi * np.arange(half, dtype=np.float64)[:, None] / seq
    ncb = ch // ct
    const = lambda shape: pl.BlockSpec(shape, lambda o, c: (0,) * len(shape))
    return pl.pallas_call(
        _filter_kernel,
        grid=(orders, ncb),
        in_specs=[const((2, half, pdim)), const((pdim, hid)), const((1, hid)), const((1, hid)),
                  const((hid, hid)), const((1, hid)), const((1, hid)),
                  pl.BlockSpec((hid, ct), lambda o, c: (0, (2 * o) * ncb + c)),
                  pl.BlockSpec((hid, ct), lambda o, c: (0, (2 * o + 1) * ncb + c)),
                  const((2, half, 1)),
                  pl.BlockSpec((1, ct), lambda o, c: (0, c)),
                  const((half, 1)), const((half, 1)), const((half, 1)), const((half, 1)),
                  const((half, half)), const((half, half))],
        out_specs=[pl.BlockSpec((1, 6, half, ct), lambda o, c: (o, 0, 0, c)),
                   pl.BlockSpec((1, 2, ct), lambda o, c: (o, 0, c))],
        out_shape=[jax.ShapeDtypeStruct((orders, 6, half, ch), F32),
                   jax.ShapeDtypeStruct((orders, 2, ch), F32)],
        scratch_shapes=[pltpu.VMEM((2, half, hid), F32)],
        compiler_params=_cparams(("arbitrary", "arbitrary")),
        name="hyena_filter",
    )(jnp.asarray(zt), pos_w1, pos_b1, freq1, pos_w2, pos_b2, freq2, pos_w3, pos_w3, jnp.asarray(t_col),
      jnp.asarray(deltas), jnp.asarray(sgn), jnp.asarray(wk), jnp.asarray(np.cos(ang).astype(np.float32)),
      jnp.asarray(np.sin(ang).astype(np.float32)), dft_c, dft_s)


def _split_rows(x_ref):
    half = x_ref.shape[1] // 2
    return x_ref[0, pl.ds(0, half, stride=2), :], x_ref[0, pl.ds(1, half, stride=2), :]


def _short_conv_split(p_ref, w_ref, b_ref):
    pe, po = _split_rows(p_ref)
    half = pe.shape[0]
    pos = lax.broadcasted_iota(jnp.int32, pe.shape, 0) % (GRID_W // 2)
    before_even = jnp.where(pos == 0, 0.0, pltpu.roll(po, 1, axis=0))
    after_odd = jnp.where(pos == GRID_W // 2 - 1, 0.0, pltpu.roll(pe, half - 1, axis=0))
    w0, w1, w2 = w_ref[0:1, :], w_ref[1:2, :], w_ref[2:3, :]
    return (b_ref[...] + before_even * w0 + pe * w1 + po * w2,
            b_ref[...] + pe * w0 + po * w1 + after_odd * w2)


def _hyena_kernel(u_ref, g_ref, cwu_ref, cbu_ref, cwg_ref, cbg_ref, fwd_ref, inv_ref, p_ref, hm_ref,
                  bias_ref, sgn_ref, o_ref, spec_ref, *, conv_u, freq_tile):
    half = inv_ref.shape[0]
    ct = o_ref.shape[2]
    ue, uo = _short_conv_split(u_ref, cwu_ref, cbu_ref) if conv_u else _split_rows(u_ref)
    ub = jnp.concatenate([ue, uo], axis=1).astype(BF16)
    for ft in range(half // freq_tile):
        lo, hi = ft * freq_tile, (ft + 1) * freq_tile
        dc = _dot(fwd_ref[lo:hi, :], ub)
        ds = _dot(fwd_ref[half + lo:half + hi, :], ub)
        er, orr, es, os_ = dc[:, :ct], dc[:, ct:], ds[:, :ct], ds[:, ct:]
        p1r, p1i, p2r, p2i, p3r, p3i = (p_ref[0, i, lo:hi, :] for i in range(6))
        ar = er * p1r + es * p1i + orr * p2r + os_ * p2i
        br = er * p3r + es * p3i + orr * p1r + os_ * p1i
        ai = es * p1r - er * p1i + os_ * p2r - orr * p2i
        bi = es * p3r - er * p3i + os_ * p1r - orr * p1i
        spec_ref[lo:hi, :] = jnp.concatenate([ar, br], axis=1).astype(BF16)
        spec_ref[half + lo:half + hi, :] = jnp.concatenate([ai, bi], axis=1).astype(BF16)
    y = _dot(inv_ref[...], spec_ref[...])
    sgn = sgn_ref[...]
    en = jnp.sum(ue * sgn, axis=0, keepdims=True)
    on = jnp.sum(uo * sgn, axis=0, keepdims=True)
    hm_r, hm_i = hm_ref[0, 0:1, :], hm_ref[0, 1:2, :]
    bias = bias_ref[0]
    ge, go = _short_conv_split(g_ref, cwg_ref, cbg_ref)
    o_ref[0, pl.ds(0, half, stride=2), :] = ge * (y[:, :ct] + ue * bias + sgn * (en * hm_r + on * hm_i))
    o_ref[0, pl.ds(1, half, stride=2), :] = go * (y[:, ct:] + uo * bias + sgn * (on * hm_r - en * hm_i))


def _hyena_conv(u, u_col_blk, p_hy, gate_col_blk, conv_w, conv_b, dft_c, dft_s, ptab, hm, bias, order,
                conv_u, ct):
    bsz, seq, _ = p_hy.shape
    half = seq // 2
    ch = ptab.shape[3]
    ncb = ch // ct
    sgn = jnp.asarray(np.where(np.arange(half) % 2 == 0, 1.0, -1.0).astype(np.float32)[:, None])
    taps = conv_w.shape[0]
    cbias = conv_b.reshape(1, -1)
    const = lambda shape: pl.BlockSpec(shape, lambda c, b: (0,) * len(shape))
    return pl.pallas_call(
        functools.partial(_hyena_kernel, conv_u=conv_u, freq_tile=half // 4),
        grid=(ncb, bsz),
        in_specs=[pl.BlockSpec((1, seq, ct), lambda c, b: (b, 0, u_col_blk * ncb + c)),
                  pl.BlockSpec((1, seq, ct), lambda c, b: (b, 0, gate_col_blk * ncb + c)),
                  pl.BlockSpec((taps, ct), lambda c, b: (0, u_col_blk * ncb + c)),
                  pl.BlockSpec((1, ct), lambda c, b: (0, u_col_blk * ncb + c)),
                  pl.BlockSpec((taps, ct), lambda c, b: (0, gate_col_blk * ncb + c)),
                  pl.BlockSpec((1, ct), lambda c, b: (0, gate_col_blk * ncb + c)),
                  const((2 * half, half)), const((half, 2 * half)),
                  pl.BlockSpec((1, 6, half, ct), lambda c, b: (order, 0, 0, c)),
                  pl.BlockSpec((1, 2, ct), lambda c, b: (order, 0, c)),
                  pl.BlockSpec((1, 1, ct), lambda c, b: (order, 0, c)),
                  const((half, 1))],
        out_specs=pl.BlockSpec((1, seq, ct), lambda c, b: (b, 0, c)),
        out_shape=jax.ShapeDtypeStruct((bsz, seq, ch), F32),
        scratch_shapes=[pltpu.VMEM((2 * half, 2 * ct), BF16)],
        compiler_params=_cparams(("arbitrary", "arbitrary")),
        name=f"hyena_conv{order}",
    )(u, p_hy, conv_w, cbias, conv_w, cbias, jnp.concatenate([dft_c, dft_s], axis=0),
      jnp.concatenate([dft_c, dft_s], axis=1), ptab, hm, bias.reshape(bias.shape[0], 1, ch), sgn)


def _scan_consts():
    c, blk = SCAN_CHUNK, SCAN_BLOCK
    r = lax.broadcasted_iota(jnp.int32, (c, c), 0)
    s = lax.broadcasted_iota(jnp.int32, (c, c), 1)
    row = lax.broadcasted_iota(jnp.int32, (c, LANES), 0)
    fb = FAST_BLOCK

    def pair_mask(size, rev):
        same = (r // (2 * size)) == (s // (2 * size))
        hi_r, hi_s = r % (2 * size) >= size, s % (2 * size) >= size
        return same & ((~hi_r & hi_s) if rev else (hi_r & ~hi_s))

    return dict(
        cum_f=(s <= r).astype(F32), cum_b=(s >= r).astype(F32),
        m32_f=(r >= c // 2) & (s < c // 2), m32_b=(r < c // 2) & (s >= c // 2),
        m16_f=pair_mask(blk, False), m16_b=pair_mask(blk, True),
        m8_f=pair_mask(fb, False), m8_b=pair_mask(fb, True),
        md_f=(r // fb == s // fb) & (s <= r), md_b=(r // fb == s // fb) & (s >= r),
        sub=lax.broadcasted_iota(jnp.int32, (fb, LANES), 0),
        row=row, ones=jnp.ones((LANES, LANES), BF16))


def _state_update(st_ref, k, v, g, g_edge):
    kd = (k * jnp.exp(g_edge - g)).astype(BF16)
    st_ref[...] = st_ref[...] * jnp.exp(g_edge) + _dot_tn(v.astype(BF16), kd)


def _chunk_cumsum(logf, cst, reverse):
    return jnp.dot(cst["cum_b" if reverse else "cum_f"], logf, preferred_element_type=F32, precision=HIGHEST)


def _scan_chunk(st_ref, q, k, v, logf, cst, reverse):
    c, blk = SCAN_CHUNK, SCAN_BLOCK
    row = cst["row"]
    g = _chunk_cumsum(logf, cst, reverse)
    o = _dot_nt((q * jnp.exp(g)).astype(BF16), st_ref[...].astype(BF16))
    half = c // 2
    if not reverse:
        ref32 = g[half - 1:half, :]
        ref16 = jnp.where(row < half, g[blk - 1:blk, :], g[half + blk - 1:half + blk, :])
        t32, s32 = row >= half, row < half
        t16, s16 = row % (2 * blk) >= blk, row % (2 * blk) < blk
        g_edge = g[c - 1:c, :]
    else:
        ref32 = g[half:half + 1, :]
        ref16 = jnp.where(row < half, g[blk:blk + 1, :], g[half + blk:half + blk + 1, :])
        t32, s32 = row < half, row >= half
        t16, s16 = row % (2 * blk) < blk, row % (2 * blk) >= blk
        g_edge = g[0:1, :]
    a32 = jnp.where(t32, q * jnp.exp(jnp.minimum(g - ref32, 0.0)), 0.0).astype(BF16)
    b32 = jnp.where(s32, k * jnp.exp(jnp.minimum(ref32 - g, 0.0)), 0.0).astype(BF16)
    a16 = jnp.where(t16, q * jnp.exp(jnp.minimum(g - ref16, 0.0)), 0.0).astype(BF16)
    b16 = jnp.where(s16, k * jnp.exp(jnp.minimum(ref16 - g, 0.0)), 0.0).astype(BF16)
    att = (jnp.where(cst["m32_b" if reverse else "m32_f"], _dot_nt(a32, b32), 0.0)
           + jnp.where(cst["m16_b" if reverse else "m16_f"], _dot_nt(a16, b16), 0.0))
    o = o + _dot(att.astype(BF16), v.astype(BF16))
    pos = row % blk
    for d in range(blk):
        shift = d if not reverse else (c - d) % c
        if d == 0:
            ks, gs, vs = k, g, v
        else:
            ks = pltpu.roll(k, shift, axis=0)
            gs = pltpu.roll(g, shift, axis=0)
            vs = pltpu.roll(v, shift, axis=0)
        valid = (pos >= d) if not reverse else (pos + d < blk)
        w = jnp.where(valid, q * ks * jnp.exp(jnp.minimum(g - gs, 0.0)), 0.0)
        o = o + _dot(w.astype(BF16), cst["ones"]) * vs
    _state_update(st_ref, k, v, g, g_edge)
    return o


def _chunk_prepare(q, k, v, logf, cst, reverse):
    c, fb = SCAN_CHUNK, FAST_BLOCK
    nb = c // fb
    d = "b" if reverse else "f"
    sub = cst["sub"]
    blocks = lambda x: [x[fb * j:fb * (j + 1), :] for j in range(nb)]
    qs, ks = blocks(q), blocks(k)
    ps = []
    for p in blocks(logf):
        for s in (1, 2, 4):
            if reverse:
                p = p + jnp.where(sub + s < fb, pltpu.roll(p, fb - s, axis=0), 0.0)
            else:
                p = p + jnp.where(sub >= s, pltpu.roll(p, s, axis=0), 0.0)
        ps.append(p)
    last = 0 if reverse else fb - 1
    tot = [jnp.broadcast_to(p[last:last + 1, :], p.shape) for p in ps]
    decay = [jnp.exp(t) for t in tot]
    memo = {}

    def span(lo, hi):
        if hi <= lo:
            return None
        if (lo, hi) not in memo:
            if hi == nb and lo > 0:
                inner, last = span(lo + 1, hi), decay[lo]
            else:
                inner, last = span(lo, hi - 1), decay[hi - 1]
            memo[lo, hi] = last if inner is None else inner * last
        return memo[lo, hi]

    scale = lambda x, f: x if f is None else x * f
    qe = [qs[j] * jnp.exp(ps[j]) for j in range(nb)]
    kb = [ks[j] * jnp.exp(tot[j] - ps[j]) for j in range(nb)]
    zero = jnp.zeros_like(tot[0])
    cat = lambda parts: jnp.concatenate(parts, axis=0).astype(BF16)

    def level(n):
        a, b = [], []
        for j in range(nb):
            mid = (j // (2 * n)) * 2 * n + n
            later = j >= mid
            if later != reverse:
                a.append(scale(qe[j], span(j + 1, mid) if reverse else span(mid, j)))
                b.append(zero)
            else:
                a.append(zero)
                b.append(scale(kb[j], span(mid, j) if reverse else span(j + 1, mid)))
        return _dot_nt(cat(a), cat(b))

    bd = cat([ks[j] * jnp.exp(-ps[j]) for j in range(nb)])
    att = (level(nb // 2)
           + jnp.where(cst["m16_" + d], level(nb // 4), 0.0)
           + jnp.where(cst["m8_" + d], level(nb // 8), 0.0)
           + jnp.where(cst["md_" + d], _dot_nt(cat(qe), bd), 0.0))
    before = lambda j: span(j + 1, nb) if reverse else span(0, j)
    after = lambda j: span(0, j) if reverse else span(j + 1, nb)
    qd = cat([scale(qe[j], before(j)) for j in range(nb)])
    kd = cat([scale(kb[j], after(j)) for j in range(nb)])
    vb = v.astype(BF16)
    return att.astype(BF16), qd, vb, _dot_tn(vb, kd), span(0, nb)[0:1, :]


def _chunk_finish(st_ref, prepared):
    att, qd, vb, kv, decay = prepared
    o = _dot_nt(qd, st_ref[...].astype(BF16)) + _dot(att, vb)
    st_ref[...] = st_ref[...] * decay + kv
    return o


def _hgrn_kernel(q_ref, i_ref, kf_ref, kb_ref, g_ref, lf_ref, lbw_ref, ic_ref, zfc_ref, zbc_ref, lb_ref, ng_ref,
                 o_ref, of_ref, ob_ref, st_ref):
    seq = q_ref.shape[1]
    ctx_len = ic_ref.shape[1]
    hd = ng_ref.shape[1]
    heads = q_ref.shape[2] // hd
    c = SCAN_CHUNK
    n_chunks, n_ctx = seq // c, ctx_len // c
    cst = _scan_consts()
    lb = _gate_floor(lb_ref[...])
    lanes = [slice(h * hd, (h + 1) * hd) for h in range(heads)]
    st_f = [st_ref.at[2 * h] for h in range(heads)]
    st_b = [st_ref.at[2 * h + 1] for h in range(heads)]

    st_ref[...] = jnp.zeros_like(st_ref)

    def ctx_body(j, carry):
        rf = pl.multiple_of(j * c, c)
        rb = pl.multiple_of((n_ctx - 1 - j) * c, c)
        for h, ln in enumerate(lanes):
            for st, z_ref, r, lb_row, rev in ((st_f[h], zfc_ref, rf, lb[0:1, ln], False),
                                              (st_b[h], zbc_ref, rb, lb[1:2, ln], True)):
                k, logf = _forget_gate(z_ref[0, pl.ds(r, c), ln], lb_row)
                *_, kv, decay = _chunk_prepare(jnp.zeros_like(k), k, ic_ref[0, pl.ds(r, c), ln], logf, cst, rev)
                st[...] = st[...] * decay + kv
        return carry

    lax.fori_loop(0, n_ctx, ctx_body, 0)

    def rows(j):
        if isinstance(j, int):
            return j * c, (n_chunks - 1 - j) * c
        return pl.multiple_of(j * c, c), pl.multiple_of((n_chunks - 1 - j) * c, c)

    def operands(r, ln, k_ref, logf_ref):
        sl = (0, pl.ds(r, c), ln)
        return q_ref[sl], k_ref[sl], i_ref[sl], logf_ref[sl]

    def direct_body(j, carry):
        rf, rb = rows(j)
        for h, ln in enumerate(lanes):
            of_ref[pl.ds(rf, c), ln] = _scan_chunk(st_f[h], *operands(rf, ln, kf_ref, lf_ref), cst, False)
            ob_ref[pl.ds(rb, c), ln] = _scan_chunk(st_b[h], *operands(rb, ln, kb_ref, lbw_ref), cst, True)
        return carry

    def prepare(j):
        rf, rb = rows(j)
        out = []
        for ln in lanes:
            out.append(_chunk_prepare(*operands(rf, ln, kf_ref, lf_ref), cst, False))
            out.append(_chunk_prepare(*operands(rb, ln, kb_ref, lbw_ref), cst, True))
        return tuple(out)

    def finish(j, prepared):
        rf, rb = rows(j)
        for h, ln in enumerate(lanes):
            of_ref[pl.ds(rf, c), ln] = _chunk_finish(st_f[h], prepared[2 * h])
            ob_ref[pl.ds(rb, c), ln] = _chunk_finish(st_b[h], prepared[2 * h + 1])

    def factored_body(j, prepared):
        upcoming = prepare(j + 1)
        finish(j, prepared)
        return upcoming

    bounded = jnp.min(lb) >= math.exp(-MAX_FAST_EXPONENT / FAST_BLOCK)

    @pl.when(bounded)
    def _():
        looped = (n_chunks - 1) // SCAN_UNROLL * SCAN_UNROLL
        prepared = lax.fori_loop(0, looped, factored_body, prepare(0), unroll=SCAN_UNROLL)
        for j in range(looped, n_chunks - 1):
            prepared = factored_body(j, prepared)
        finish(n_chunks - 1, prepared)

    @pl.when(jnp.logical_not(bounded))
    def _():
        lax.fori_loop(0, n_chunks, direct_body, 0)

    for ln in lanes:
        o = of_ref[:, ln] + ob_ref[:, ln]
        y = o * lax.rsqrt(jnp.mean(o * o, axis=-1, keepdims=True) + EPS) * ng_ref[...]
        o_ref[0, :, ln] = y * _silu(g_ref[0, :, ln])


def _hgrn(p, col0, logf, pc, hg_lb, norm_g, heads_per_step):
    bsz, seq, _ = p.shape
    ctx_len = pc.shape[1]
    hd = norm_g.shape[0]
    w = heads_per_step * hd
    steps = HG_HEADS // heads_per_step
    lat = lambda grp: pl.BlockSpec((1, seq, w), lambda b, h: (b, 0, (col0 + grp * HG_HEADS) // heads_per_step + h))
    lgf = lambda d: pl.BlockSpec((1, seq, w), lambda b, h: (b, 0, d * steps + h))
    ctx = lambda grp: pl.BlockSpec((1, ctx_len, w), lambda b, h: (b, 0, grp * steps + h))
    return pl.pallas_call(
        _hgrn_kernel,
        grid=(bsz, steps),
        in_specs=[lat(0), lat(1), lat(2), lat(3), lat(4), lgf(0), lgf(1), ctx(0), ctx(1), ctx(2),
                  pl.BlockSpec((hg_lb.shape[0], 2, w), lambda b, h: (0, 0, h)),
                  pl.BlockSpec((1, hd), lambda b, h: (0, 0))],
        out_specs=pl.BlockSpec((1, seq, w), lambda b, h: (b, 0, h)),
        out_shape=jax.ShapeDtypeStruct((bsz, seq, HG_HEADS * hd), F32),
        scratch_shapes=[pltpu.VMEM((seq, w), F32), pltpu.VMEM((seq, w), F32),
                        pltpu.VMEM((2 * heads_per_step, hd, hd), F32)],
        compiler_params=_cparams(("arbitrary", "arbitrary")),
        name="hgrn_scan",
    )(p, p, p, p, p, logf, logf, pc, pc, pc, hg_lb, norm_g.reshape(1, hd))


def _outproj_kernel(yhy_ref, yhg_ref, x_ref, g1_ref, sh2_ref, sc2_ref, hyg_ref, n2g_ref, wt_ref, wb_ref, rw_ref,
                    x1_ref, h2_ref, aff_ref):
    yhy = yhy_ref[0]
    yn = yhy * lax.rsqrt(jnp.mean(yhy * yhy, axis=-1, keepdims=True) + EPS) * hyg_ref[...]
    mix = _dot(yn.astype(BF16), wt_ref[...]) + _dot(yhg_ref[0].astype(BF16), wb_ref[...])
    x1 = x_ref[0] + g1_ref[0] * mix
    x1_ref[0] = x1
    h2 = (x1 * lax.rsqrt(jnp.mean(x1 * x1, axis=-1, keepdims=True) + EPS) * n2g_ref[...]
          * (1.0 + sc2_ref[0]) + sh2_ref[0])
    h_hi = h2.astype(BF16)
    h2_ref[0] = h_hi
    h_lo = (h2 - h_hi.astype(F32)).astype(BF16)
    rw = rw_ref[...]
    r_hi = rw.astype(BF16)
    r_lo = (rw - r_hi.astype(F32)).astype(BF16)
    ne = rw.shape[0]
    both = _dot_nt(jnp.concatenate([r_hi, r_lo], axis=0), h_hi)
    logits = both[:ne] + both[ne:] + _dot_nt(r_hi, h_lo)
    ex = jnp.exp(logits - jnp.max(logits, axis=0, keepdims=True))
    aff_ref[0] = ex / jnp.sum(ex, axis=0, keepdims=True)


def _outproj(y_hy, y_hg, x, mods3, hy_norm_g, norm2_g, w_top, w_bot, router_wt, tm):
    bsz, seq, d = x.shape
    wh = y_hy.shape[2]
    ne = router_wt.shape[0]
    mod = lambda k: pl.BlockSpec((1, 1, d), lambda b, i: (b, 0, k))
    const = lambda shape: pl.BlockSpec(shape, lambda b, i: (0,) * len(shape))
    return pl.pallas_call(
        _outproj_kernel,
        grid=(bsz, seq // tm),
        in_specs=[pl.BlockSpec((1, tm, wh), lambda b, i: (b, i, 0)),
                  pl.BlockSpec((1, tm, wh), lambda b, i: (b, i, 0)),
                  pl.BlockSpec((1, tm, d), lambda b, i: (b, i, 0)),
                  mod(2), mod(3), mod(4),
                  const((1, wh)), const((1, d)), const((wh, d)), const((wh, d)), const((ne, d))],
        out_specs=[pl.BlockSpec((1, tm, d), lambda b, i: (b, i, 0)),
                   pl.BlockSpec((1, tm, d), lambda b, i: (b, i, 0)),
                   pl.BlockSpec((1, ne, tm), lambda b, i: (b, 0, i))],
        out_shape=[jax.ShapeDtypeStruct((bsz, seq, d), F32),
                   jax.ShapeDtypeStruct((bsz, seq, d), BF16),
                   jax.ShapeDtypeStruct((bsz, ne, seq), F32)],
        compiler_params=_cparams(("arbitrary", "arbitrary")),
        name="outproj_router",
    )(y_hy, y_hg, x, mods3, mods3, mods3, hy_norm_g.reshape(1, wh), norm2_g.reshape(1, d), w_top, w_bot, router_wt)


def _select_kernel(a_ref, slot_ref, *, cap, rows_per_expert):
    nr = a_ref.shape[1]
    affs = [a_ref[i] for i in range(a_ref.shape[0])]
    r = lax.broadcasted_iota(jnp.int32, (nr, nr), 0)
    s = lax.broadcasted_iota(jnp.int32, (nr, nr), 1)
    same = (r // rows_per_expert) == (s // rows_per_expert)
    grp = same.astype(BF16)
    before = (same & (s < r)).astype(BF16)
    li = lax.broadcasted_iota(jnp.int32, (LANES, LANES), 0)
    lj = lax.broadcasted_iota(jnp.int32, (LANES, LANES), 1)
    ones = jnp.ones((LANES, LANES), BF16)
    strict = (li < lj).astype(BF16)

    def row_total(ind):
        return _dot(ind, ones).astype(BF16)

    def expert_total(ind):
        return _dot(grp, row_total(ind))

    def prefix(ind):
        return _dot(ind, strict) + _dot(before, row_total(ind))

    ne = nr // rows_per_expert

    def count_at_least(a, cand):
        ind = jnp.where(a >= cand, 1.0, 0.0).reshape(ne, rows_per_expert, LANES)
        tot = jnp.sum(jnp.sum(ind, axis=1, keepdims=True), axis=2, keepdims=True)
        return jnp.broadcast_to(tot, ind.shape).reshape(nr, LANES)

    def search(i, thrs):
        bit = jnp.left_shift(jnp.int32(1), 30 - i)
        out = []
        for a, t in zip(affs, thrs):
            cand = t | bit
            out.append(jnp.where(count_at_least(a, pltpu.bitcast(cand, F32)) >= cap, cand, t))
        return tuple(out)

    thrs = lax.fori_loop(0, 31, search, tuple(jnp.zeros((nr, LANES), jnp.int32) for _ in affs))
    for i, (a, thr) in enumerate(zip(affs, thrs)):
        thr_f = pltpu.bitcast(thr, F32)
        gt = a > thr_f
        eq = a == thr_f
        need = cap - expert_total(jnp.where(gt, 1.0, 0.0).astype(BF16))
        eq_rank = prefix(jnp.where(eq, 1.0, 0.0).astype(BF16))
        sel = gt | (eq & (eq_rank < need))
        rank = prefix(jnp.where(sel, 1.0, 0.0).astype(BF16))
        slot_ref[i] = jnp.where(sel, rank, -1.0)


def _select(aff_t, cap, bt):
    bsz, ne, n = aff_t.shape
    rpe = n // LANES
    a2 = aff_t.reshape(bsz, ne * rpe, LANES)
    slot = pl.pallas_call(
        functools.partial(_select_kernel, cap=cap, rows_per_expert=rpe),
        grid=(bsz // bt,),
        in_specs=[pl.BlockSpec((bt, ne * rpe, LANES), lambda b: (b, 0, 0))],
        out_specs=pl.BlockSpec((bt, ne * rpe, LANES), lambda b: (b, 0, 0)),
        out_shape=jax.ShapeDtypeStruct((bsz, ne * rpe, LANES), F32),
        compiler_params=_cparams(("arbitrary",)),
        name="ec_select",
    )(a2)
    return slot.reshape(bsz, ne, n)


def _gather_kernel(slot_ref, h_ref, o_ref, *, cap):
    slot = slot_ref[0, 0]
    j = lax.broadcasted_iota(jnp.int32, (cap, slot.shape[1]), 0).astype(F32)
    onehot = jnp.where(slot == j, 1.0, 0.0).astype(BF16)
    o_ref[0, 0] = _dot(onehot, h_ref[0]).astype(BF16)


def _gather(slot4, h2, cap):
    bsz, ne, _, n = slot4.shape
    d = h2.shape[2]
    return pl.pallas_call(
        functools.partial(_gather_kernel, cap=cap),
        grid=(bsz, ne),
        in_specs=[pl.BlockSpec((1, 1, 1, n), lambda b, e: (b, e, 0, 0)),
                  pl.BlockSpec((1, n, d), lambda b, e: (b, 0, 0))],
        out_specs=pl.BlockSpec((1, 1, cap, d), lambda b, e: (b, e, 0, 0)),
        out_shape=jax.ShapeDtypeStruct((bsz, ne, cap, d), BF16),
        compiler_params=_cparams(("arbitrary", "arbitrary")),
        name="ec_gather",
    )(slot4, h2)


def _expert_kernel(x_ref, w1_ref, w3_ref, w2_ref, o_ref, acc_ref):
    f = pl.program_id(2)
    bt, _, cap, d = x_ref.shape
    x = x_ref[...].reshape(bt * cap, d)

    @pl.when(f == 0)
    def _():
        acc_ref[...] = jnp.zeros_like(acc_ref)

    w1, w3, w2 = w1_ref[0].astype(BF16), w3_ref[0].astype(BF16), w2_ref[0].astype(BF16)
    rows = bt * cap // EXPERT_ROW_TILES
    for r in range(EXPERT_ROW_TILES):
        sl = slice(r * rows, (r + 1) * rows)
        a = _dot(x[sl], w1)
        b = _dot(x[sl], w3)
        acc_ref[sl, :] += _dot((_silu(a) * b).astype(BF16), w2)

    @pl.when(f == pl.num_programs(2) - 1)
    def _():
        o_ref[...] = acc_ref[...].astype(BF16).reshape(bt, 1, cap, d)


def _experts(xg, w1, w3, w2, bt, ft):
    bsz, ne, cap, d = xg.shape
    ff = w1.shape[2]
    return pl.pallas_call(
        _expert_kernel,
        grid=(ne, bsz // bt, ff // ft),
        in_specs=[pl.BlockSpec((bt, 1, cap, d), lambda e, m, f: (m, e, 0, 0)),
                  pl.BlockSpec((1, d, ft), lambda e, m, f: (e, 0, f)),
                  pl.BlockSpec((1, d, ft), lambda e, m, f: (e, 0, f)),
                  pl.BlockSpec((1, ft, d), lambda e, m, f: (e, f, 0))],
        out_specs=pl.BlockSpec((bt, 1, cap, d), lambda e, m, f: (m, e, 0, 0)),
        out_shape=jax.ShapeDtypeStruct((bsz, ne, cap, d), BF16),
        scratch_shapes=[pltpu.VMEM((bt * cap, d), F32)],
        compiler_params=_cparams(("arbitrary", "arbitrary", "arbitrary")),
        name="ec_experts",
    )(xg, w1, w3, w2)


def _combine_kernel(y_ref, slot_ref, aff_ref, x1_ref, g2_ref, fg_ref, o_ref, *, cap):
    _, ne, _, d = y_ref.shape
    slot = slot_ref[0]
    aff = aff_ref[0]
    j = lax.broadcasted_iota(jnp.int32, (slot.shape[0], cap), 1).astype(F32)
    gated = jnp.concatenate([jnp.where(slot[:, e:e + 1] == j, aff[:, e:e + 1], 0.0).astype(BF16)
                             for e in range(ne)], axis=1)
    moe = _dot(gated, y_ref[0].reshape(ne * cap, d))
    x2 = x1_ref[0] + g2_ref[0] * moe
    o_ref[0] = x2 * lax.rsqrt(jnp.mean(x2 * x2, axis=-1, keepdims=True) + EPS) * fg_ref[...]


def _combine(y, slot_nt, aff_nt, x1, mods3, final_g, cap, tn):
    bsz, ne, _, d = y.shape
    n = x1.shape[1]
    return pl.pallas_call(
        functools.partial(_combine_kernel, cap=cap),
        grid=(bsz, n // tn),
        in_specs=[pl.BlockSpec((1, ne, cap, d), lambda b, t: (b, 0, 0, 0)),
                  pl.BlockSpec((1, tn, ne), lambda b, t: (b, t, 0)),
                  pl.BlockSpec((1, tn, ne), lambda b, t: (b, t, 0)),
                  pl.BlockSpec((1, tn, d), lambda b, t: (b, t, 0)),
                  pl.BlockSpec((1, 1, d), lambda b, t: (b, 0, 5)),
                  pl.BlockSpec((1, d), lambda b, t: (0, 0))],
        out_specs=pl.BlockSpec((1, tn, d), lambda b, t: (b, t, 0)),
        out_shape=jax.ShapeDtypeStruct((bsz, n, d), F32),
        compiler_params=_cparams(("arbitrary", "arbitrary")),
        name="ec_combine",
    )(y, slot_nt, aff_nt, x1, mods3, final_g.reshape(1, d))


def kernel(x, c, ctx, c_ctx, ada_w, ada_b, norm1_g, w_in, hy_conv_w, hy_conv_b, hy_pos_w1, hy_pos_b1, hy_freq1,
           hy_pos_w2, hy_pos_b2, hy_freq2, hy_pos_w3, hy_bias, hy_norm_g, hg_lb, hg_norm_g, w_out, norm2_g,
           router_w, exp_w1, exp_w3, exp_w2, final_g):
    bsz, seq, d = x.shape
    hyw = hy_norm_g.shape[1]
    hgw = w_in.shape[2] - 3 * hyw
    hg = hgw // 5
    assert hyw == hg, "column groups of the input projection are addressed with one width"
    cap = CAPACITY_FACTOR * seq // N_EXPERTS

    rows = -(-(bsz + 1) // 8) * 8
    s_rows = jnp.zeros((rows, d), F32).at[:bsz].set(c).at[bsz].set(c_ctx)
    mods3 = _ada(s_rows, ada_w[0], ada_b[0]).reshape(rows, 1, 6 * d)

    w_bf = w_in[0].astype(BF16)
    p, logf = _inproj_mix(x, mods3, norm1_g[0], w_bf, hg_lb, INPROJ_COL_TILE, hg, 3, 5, 6, (hg // HG_HEADS) ** -0.5)
    w_ctx = w_bf[:, 3 * hyw + hg:3 * hyw + 4 * hg]
    pc = _inproj(ctx, mods3, lambda b: bsz, norm1_g[0], w_ctx, 0, 1, ctx.shape[1], 3 * hg)

    dft_c, dft_s = _dft_tables(seq)
    ct = HYENA_CH_TILE
    ptab, hm = _hyena_filters(seq, hy_pos_w1[0], hy_pos_b1[0], hy_freq1[0], hy_pos_w2[0], hy_pos_b2[0],
                              hy_freq2[0], hy_pos_w3[0], dft_c, dft_s, FILTER_CH_TILE)
    z = _hyena_conv(p, 0, p, 1, hy_conv_w[0], hy_conv_b[0], dft_c, dft_s, ptab, hm, hy_bias[0], 0, True, ct)
    y_hy = _hyena_conv(z, 0, p, 2, hy_conv_w[0], hy_conv_b[0], dft_c, dft_s, ptab, hm, hy_bias[0], 1, False, ct)

    y_hg = _hgrn(p, 3 * hyw // (hg // HG_HEADS), logf, pc, hg_lb, hg_norm_g[0], SCAN_HEADS_PER_STEP)

    w_o = w_out[0].astype(BF16)
    x1, h2, aff_t = _outproj(y_hy, y_hg, x, mods3, hy_norm_g[0], norm2_g[0], w_o[:hyw], w_o[hyw:],
                             router_w[0].T, OUTPROJ_ROW_TILE)

    slot = _select(aff_t, cap, SELECT_SAMPLES)
    xg = _gather(slot.reshape(bsz, N_EXPERTS, 1, seq), h2, cap)
    y = _experts(xg, exp_w1[0], exp_w3[0], exp_w2[0], EXPERT_SAMPLES, EXPERT_FF_TILE)
    return _combine(y, jnp.swapaxes(slot, 1, 2), jnp.swapaxes(aff_t, 1, 2), x1, mods3, final_g, cap,
                    COMBINE_ROW_TILE)
```
